```python
import jax, jax.numpy as jnp
from jax import lax
import numpy as np

D_MODEL = 1024
BATCH = 32
SEQ = 256
DEPTH = 4
DEC_BATCH = 2
DEC_SEQ = 2048
PAST_LEN = 256

GRID_W = 64
WIN_H = 8
WIN_W = 16
H_A = 8
HD_A = 64
A_WIDTH = H_A * HD_A
H_B = 4
HD_B = 128
B_WIDTH = H_B * HD_B
D_FF = 2816
N_MOD = 9
CHUNK = 128
Q_BLOCK = 128
ROPE_THETA = 10000.0
EPS = 1e-6
IN_COLS = 3 * A_WIDTH + 4 * B_WIDTH + 4 * H_B + 2 * D_MODEL

kernel_name = "hybrid_natten_mlstm_diffusion_step"


def _rmsnorm(x, g):
    xf = x.astype(jnp.float32)
    y = xf * lax.rsqrt(jnp.mean(xf * xf, axis=-1, keepdims=True) + EPS)
    return (y * g.astype(jnp.float32)).astype(x.dtype)


def _ada(cond, w, b):
    m = jax.nn.silu(cond) @ w + b
    return m.reshape(m.shape[0], 1, N_MOD, D_MODEL)


def _modulate(h, shift, scale):
    return h * (1 + scale) + shift


def _ffn_sub(x, g, shift, scale, gate, w_up, w_dn):
    h = _modulate(_rmsnorm(x, g), shift, scale)
    a, u = jnp.split(h @ w_up, 2, axis=-1)
    return x + 0.5 * gate * ((jax.nn.silu(a) * u) @ w_dn)


def _mixer_inputs(x, g, shift, scale, w_in, b_gates, g_qn, g_kn):
    B, L = x.shape[:2]
    h = _modulate(_rmsnorm(x, g), shift, scale)
    p = h @ w_in
    sizes = (A_WIDTH,) * 3 + (B_WIDTH,) * 4 + (4 * H_B, 2 * D_MODEL)
    offs = np.cumsum(sizes)[:-1].tolist()
    qa, ka, va, qb, kb, vb, ob, gt, br = jnp.split(p, offs, axis=-1)
    qa = _rmsnorm(qa.reshape(B, L, H_A, HD_A), g_qn)
    ka = _rmsnorm(ka.reshape(B, L, H_A, HD_A), g_kn)
    va = va.reshape(B, L, H_A, HD_A)
    qb = qb.reshape(B, L, H_B, HD_B)
    kb = kb.reshape(B, L, H_B, HD_B)
    vb = vb.reshape(B, L, H_B, HD_B)
    return qa, ka, va, qb, kb, vb, ob, gt + b_gates, br


def _mixer_output(att, hb, ob, br, g_hn, w_pa, w_pb, w_o):
    B, L = att.shape[:2]
    hb = _rmsnorm(jnp.transpose(hb, (0, 2, 1, 3)), g_hn.reshape(H_B, HD_B))
    hb = hb.reshape(B, L, B_WIDTH).astype(ob.dtype) * jax.nn.sigmoid(ob)
    g_a, g_b = jnp.split(jax.nn.sigmoid(br), 2, axis=-1)
    y = g_a * (att.reshape(B, L, A_WIDTH) @ w_pa) + g_b * (hb @ w_pb)
    return y @ w_o


def _ctx_attention(q, k, v):
    B, P, H, d = q.shape
    nb = P // Q_BLOCK
    qblk = jnp.moveaxis(q.reshape(B, nb, Q_BLOCK, H, d), 1, 0)

    def blk(qi):
        s = jnp.einsum('bqhd,bkhd->bhqk', qi, k).astype(jnp.float32) * (d ** -0.5)
        p = jax.nn.softmax(s, axis=-1).astype(v.dtype)
        return jnp.einsum('bhqk,bkhd->bqhd', p, v)

    out = lax.map(blk, qblk)
    return jnp.moveaxis(out, 0, 1).reshape(B, P, H, d)


def _na_attention(q, k, v, k_ctx, v_ctx, rpb):
    B, L, H, d = q.shape
    rows = L // GRID_W
    kh = min(WIN_H, rows)
    r = jnp.arange(rows)
    start_r = jnp.clip(r - WIN_H // 2, 0, rows - kh)
    row_idx = start_r[:, None] + jnp.arange(kh)[None, :]
    cidx = jnp.arange(GRID_W)
    start_c = jnp.clip(cidx - WIN_W // 2, 0, GRID_W - WIN_W)
    col_ok = (cidx[None, :] >= start_c[:, None]) & (cidx[None, :] < start_c[:, None] + WIN_W)
    dr = row_idx - r[:, None] + (WIN_H - 1)
    dc = jnp.clip(cidx[None, :] - cidx[:, None] + (WIN_W - 1), 0, 2 * WIN_W - 2)
    bias = rpb.astype(jnp.float32)[:, dr[:, :, None, None], dc[None, None]]
    bias = jnp.transpose(bias, (0, 1, 3, 2, 4)).reshape(H, rows, GRID_W, kh * GRID_W)
    mask = jnp.broadcast_to(col_ok[:, None, :], (GRID_W, kh, GRID_W)).reshape(GRID_W, kh * GRID_W)
    qg = q.reshape(B, rows, GRID_W, H, d)
    kg = k.reshape(B, rows, GRID_W, H, d)[:, row_idx].reshape(B, rows, kh * GRID_W, H, d)
    vg = v.reshape(B, rows, GRID_W, H, d)[:, row_idx].reshape(B, rows, kh * GRID_W, H, d)
    scale = d ** -0.5
    s_win = jnp.einsum('brqhd,brkhd->bhrqk', qg, kg).astype(jnp.float32) * scale + bias[None]
    s_win = jnp.where(mask, s_win, -1e30)
    s_ctx = jnp.einsum('brqhd,bphd->bhrqp', qg, k_ctx).astype(jnp.float32) * scale
    p = jax.nn.softmax(jnp.concatenate([s_win, s_ctx], axis=-1), axis=-1).astype(v.dtype)
    nw = kh * GRID_W
    out = (jnp.einsum('bhrqk,brkhd->brqhd', p[..., :nw], vg)
           + jnp.einsum('bhrqp,bphd->brqhd', p[..., nw:], v_ctx))
    return out.reshape(B, L, H, d)


def _rope_tables(n_tokens):
    half = HD_B // 2
    nf = half // 2
    inv = ROPE_THETA ** (-jnp.arange(nf, dtype=jnp.float32) / nf)
    t = jnp.arange(n_tokens)
    row = (t // GRID_W).astype(jnp.float32)
    col = (t % GRID_W).astype(jnp.float32)
    ang = jnp.concatenate([row[:, None] * inv, col[:, None] * inv], axis=-1)
    return jnp.cos(ang)[:, None, :], jnp.sin(ang)[:, None, :]


def _apply_rope(x, cos, sin):
    half = x.shape[-1] // 2
    xf = x.astype(jnp.float32)
    x1, x2 = xf[..., :half], xf[..., half:]
    return jnp.concatenate([x1 * cos - x2 * sin, x1 * sin + x2 * cos], axis=-1).astype(x.dtype)


def _mlstm_scan(q, k, v, ig, lf, C0, n0, m0):
    B, H, L, d = q.shape
    T = min(CHUNK, L)
    nc = L // T
    chunk = lambda a: jnp.moveaxis(a.reshape(B, H, nc, T, *a.shape[3:]), 2, 0)
    tril = jnp.tril(jnp.ones((T, T), dtype=bool))

    def step(carry, xs):
        C, n, m = carry
        qc, kc, vc, ic, fc = xs
        b = jnp.cumsum(fc, axis=-1)
        Dm = jnp.where(tril, b[..., :, None] - b[..., None, :] + ic[..., None, :], -jnp.inf)
        inter = b + m[..., None]
        m_t = jnp.maximum(inter, jnp.max(Dm, axis=-1))
        w = jnp.exp(Dm - m_t[..., None])
        a = jnp.exp(inter - m_t)
        s = jnp.einsum('bhtd,bhsd->bhts', qc, kc) * w
        num = (a[..., None] * jnp.einsum('bhvk,bhtk->bhtv', C, qc)
               + jnp.einsum('bhts,bhsv->bhtv', s, vc))
        den = a * jnp.einsum('bhk,bhtk->bht', n, qc) + jnp.sum(s, axis=-1)
        h = num / jnp.maximum(jnp.abs(den), jnp.exp(-m_t))[..., None]
        bT = b[..., -1]
        gl = bT[..., None] - b + ic
        m_new = jnp.maximum(bT + m, jnp.max(gl, axis=-1))
        a_s = jnp.exp(bT + m - m_new)
        w_s = jnp.exp(gl - m_new[..., None])
        C_new = a_s[..., None, None] * C + jnp.einsum('bhs,bhsv,bhsk->bhvk', w_s, vc, kc)
        n_new = a_s[..., None] * n + jnp.einsum('bhs,bhsk->bhk', w_s, kc)
        return (C_new, n_new, m_new), h

    (Cf, nf, mf), hs = lax.scan(step, (C0, n0, m0),
                                (chunk(q), chunk(k), chunk(v), chunk(ig), chunk(lf)))
    h = jnp.moveaxis(hs, 0, 2).reshape(B, H, L, d)
    return h, Cf, nf, mf


def _mlstm_bidir(q, k, v, gates, C0, n0, m0):
    f32 = jnp.float32
    tr = lambda a: jnp.transpose(a, (0, 2, 1, 3)).astype(f32)
    q, k, v = tr(q), tr(k), tr(v)
    B, L = gates.shape[:2]
    g = jnp.transpose(gates.astype(f32).reshape(B, L, 4, H_B), (2, 0, 3, 1))
    ig_f, lf_f = g[0], jax.nn.log_sigmoid(g[1])
    ig_b, lf_b = g[2], jax.nn.log_sigmoid(g[3])
    C0, n0, m0 = C0.astype(f32), n0.astype(f32), m0.astype(f32)
    h_f, Cf, nf, mf = _mlstm_scan(q, k, v, ig_f, lf_f, C0[:, 0], n0[:, 0], m0[:, 0])
    fl = lambda a: jnp.flip(a, axis=2)
    h_b, Cb, nb, mb = _mlstm_scan(fl(q), fl(k), fl(v), fl(ig_b), fl(lf_b), C0[:, 1], n0[:, 1], m0[:, 1])
    h = h_f + fl(h_b)
    return h, jnp.stack([Cf, Cb], axis=1), jnp.stack([nf, nb], axis=1), jnp.stack([mf, mb], axis=1)


def setup_inputs(seed: int = 0) -> dict:
    key = jax.random.key(seed)
    ks = jax.random.split(key, 32)
    nrm = lambda k, shape, s=1.0: jax.random.normal(k, shape, jnp.float32) * s
    b_gates = jnp.concatenate([
        nrm(ks[24], (DEPTH, H_B), 0.1),
        3.0 + nrm(ks[25], (DEPTH, H_B), 0.5),
        nrm(ks[26], (DEPTH, H_B), 0.1),
        3.0 + nrm(ks[27], (DEPTH, H_B), 0.5)], axis=-1)
    return {
        "x_prompt": nrm(ks[0], (BATCH, SEQ, D_MODEL)),
        "x_sample": nrm(ks[1], (DEC_BATCH, DEC_SEQ, D_MODEL)),
        "cache_k": nrm(ks[2], (DEC_BATCH, DEPTH, PAST_LEN, H_A, HD_A)),
        "cache_v": nrm(ks[3], (DEC_BATCH, DEPTH, PAST_LEN, H_A, HD_A)),
        "state_C": nrm(ks[4], (DEC_BATCH, DEPTH, 2, H_B, HD_B, HD_B), 0.1),
        "state_n": nrm(ks[5], (DEC_BATCH, DEPTH, 2, H_B, HD_B), 0.3),
        "state_m": nrm(ks[6], (DEC_BATCH, DEPTH, 2, H_B), 0.5),
        "c": nrm(ks[7], (DEC_BATCH, D_MODEL)),
        "c_ctx": nrm(ks[8], (D_MODEL,)),
        "w_ada": nrm(ks[9], (DEPTH, D_MODEL, N_MOD * D_MODEL), D_MODEL ** -0.5),
        "b_ada": nrm(ks[10], (DEPTH, N_MOD * D_MODEL), 0.02),
        "g_norm": 1.0 + nrm(ks[11], (DEPTH, 3, D_MODEL), 0.05),
        "w_up1": nrm(ks[12], (DEPTH, D_MODEL, 2 * D_FF), D_MODEL ** -0.5),
        "w_dn1": nrm(ks[13], (DEPTH, D_FF, D_MODEL), D_FF ** -0.5),
        "w_in": nrm(ks[14], (DEPTH, D_MODEL, IN_COLS), D_MODEL ** -0.5),
        "b_gates": b_gates,
        "g_qn": 1.0 + nrm(ks[15], (DEPTH, HD_A), 0.05),
        "g_kn": 1.0 + nrm(ks[16], (DEPTH, HD_A), 0.05),
        "rpb": nrm(ks[17], (DEPTH, H_A, 2 * WIN_H - 1, 2 * WIN_W - 1), 0.1),
        "g_hn": 1.0 + nrm(ks[18], (DEPTH, B_WIDTH), 0.05),
        "w_pa": nrm(ks[19], (DEPTH, A_WIDTH, D_MODEL), A_WIDTH ** -0.5),
        "w_pb": nrm(ks[20], (DEPTH, B_WIDTH, D_MODEL), B_WIDTH ** -0.5),
        "w_o": nrm(ks[21], (DEPTH, D_MODEL, D_MODEL), D_MODEL ** -0.5),
        "w_up2": nrm(ks[22], (DEPTH, D_MODEL, 2 * D_FF), D_MODEL ** -0.5),
        "w_dn2": nrm(ks[23], (DEPTH, D_FF, D_MODEL), D_FF ** -0.5),
    }


def reference(x_prompt, x_sample, cache_k, cache_v, state_C, state_n, state_m, c, c_ctx,
              w_ada, b_ada, g_norm, w_up1, w_dn1, w_in, b_gates, g_qn, g_kn, rpb, g_hn,
              w_pa, w_pb, w_o, w_up2, w_dn2):
    B = x_prompt.shape[0]
    L = x_sample.shape[1]
    kscale = HD_B ** -0.5
    cos, sin = _rope_tables(L)
    zC = jnp.zeros((B, 2, H_B, HD_B, HD_B), jnp.float32)
    zn = jnp.zeros((B, 2, H_B, HD_B), jnp.float32)
    zm = jnp.zeros((B, 2, H_B), jnp.float32)
    xp, xs = x_prompt, x_sample
    ks_out, vs_out, Cs_out, ns_out, ms_out = [], [], [], [], []
    for l in range(DEPTH):
        mc = _ada(c_ctx[None], w_ada[l], b_ada[l])
        ml = _ada(c, w_ada[l], b_ada[l])

        xp = _ffn_sub(xp, g_norm[l, 0], mc[:, :, 0], mc[:, :, 1], mc[:, :, 2], w_up1[l], w_dn1[l])
        qa, ka, va, qb, kb, vb, ob, gt, br = _mixer_inputs(
            xp, g_norm[l, 1], mc[:, :, 3], mc[:, :, 4], w_in[l], b_gates[l], g_qn[l], g_kn[l])
        att = _ctx_attention(qa, ka, va)
        hb, Cc, nc_, mc_s = _mlstm_bidir(qb, kb * kscale, vb, gt, zC, zn, zm)
        xp = xp + mc[:, :, 5] * _mixer_output(att, hb, ob, br, g_hn[l], w_pa[l], w_pb[l], w_o[l])
        xp = _ffn_sub(xp, g_norm[l, 2], mc[:, :, 6], mc[:, :, 7], mc[:, :, 8], w_up2[l], w_dn2[l])
        ks_out.append(ka)
        vs_out.append(va)
        Cs_out.append(Cc)
        ns_out.append(nc_)
        ms_out.append(mc_s)

        xs = _ffn_sub(xs, g_norm[l, 0], ml[:, :, 0], ml[:, :, 1], ml[:, :, 2], w_up1[l], w_dn1[l])
        qa, ka, va, qb, kb, vb, ob, gt, br = _mixer_inputs(
            xs, g_norm[l, 1], ml[:, :, 3], ml[:, :, 4], w_in[l], b_gates[l], g_qn[l], g_kn[l])
        att = _na_attention(qa, ka, va, cache_k[:, l], cache_v[:, l], rpb[l])
        qb = _apply_rope(qb, cos, sin)
        kb = _apply_rope(kb, cos, sin)
        hb, _, _, _ = _mlstm_bidir(qb, kb * kscale, vb, gt, state_C[:, l], state_n[:, l], state_m[:, l])
        xs = xs + ml[:, :, 5] * _mixer_output(att, hb, ob, br, g_hn[l], w_pa[l], w_pb[l], w_o[l])
        xs = _ffn_sub(xs, g_norm[l, 2], ml[:, :, 6], ml[:, :, 7], ml[:, :, 8], w_up2[l], w_dn2[l])

    new_cache_k = jnp.stack(ks_out, axis=1)
    new_cache_v = jnp.stack(vs_out, axis=1)
    new_state_C = jnp.stack(Cs_out, axis=1)
    new_state_n = jnp.stack(ns_out, axis=1)
    new_state_m = jnp.stack(ms_out, axis=1)
    return (xp, xs, new_cache_k, new_cache_v, new_state_C, new_state_n, new_state_m)
```

```python
import functools

import jax
import jax.numpy as jnp
import numpy as np
from jax import lax
from jax.experimental import pallas as pl
from jax.experimental.pallas import tpu as pltpu

F32 = jnp.float32
BF16 = jnp.bfloat16

D_MODEL = 1024
DEPTH = 4
N_CTX_B, CTX_LEN = 32, 256
N_LAT_B, LAT_LEN = 2, 2048
N_CTX = N_CTX_B * CTX_LEN
N_LAT = N_LAT_B * LAT_LEN
N_TOK = N_CTX + N_LAT
GRID_W, WIN_H, WIN_W = 64, 8, 16
LAT_ROWS = LAT_LEN // GRID_W
H_A, HD_A = 8, 64
H_B, HD_B = 4, 128
A_WIDTH = H_A * HD_A
B_WIDTH = H_B * HD_B
D_FF = 2816
N_MOD = 9
CHUNK = 128
ROPE_THETA = 10000.0
EPS = 1e-6
KSCALE = HD_B ** -0.5
ASCALE = HD_A ** -0.5
N_GATE = 4 * H_B
MAIN_COLS = 3 * A_WIDTH + 4 * B_WIDTH + 2 * D_MODEL

LANES = 128
TM = 512
FF_CHUNK = 256
VMEM_LIMIT = 52 * 1024 * 1024


def _dot(a, b):
    return jnp.dot(a, b, preferred_element_type=F32)


def _dot_nt(a, b):
    return lax.dot_general(a, b, (((1,), (1,)), ((), ())), preferred_element_type=F32)


def _split2(x):
    hi = x.astype(BF16)
    lo = (x - hi.astype(F32)).astype(BF16)
    return hi, lo


def _split3(x):
    hi = x.astype(BF16)
    r1 = x - hi.astype(F32)
    mid = r1.astype(BF16)
    lo = (r1 - mid.astype(F32)).astype(BF16)
    return hi, mid, lo


def _norm_mod(x, g, shift, scale):
    ms = jnp.mean(x * x, axis=-1, keepdims=True)
    return (x * lax.rsqrt(ms + EPS) * g) * (1.0 + scale) + shift


def _params(sem):
    return pltpu.CompilerParams(dimension_semantics=sem, vmem_limit_bytes=VMEM_LIMIT)


def _resident(shape):
    nd = len(shape)
    return pl.BlockSpec(shape, lambda *_: (0,) * nd, pipeline_mode=pl.Buffered(1))


def _mod_set(i):
    n_ctx_tiles = N_CTX // TM
    return jnp.where(i < n_ctx_tiles, 0, 1 + (i - n_ctx_tiles) // (LAT_LEN // TM))


def _ada_kernel(c_ref, w_ref, b_ref, o_ref):
    c = c_ref[...]
    s = c * jax.nn.sigmoid(c)
    sh, sl = _split2(s)
    wh, wl = _split2(w_ref[0])
    o_ref[0] = _dot(sh, wh) + _dot(sl, wh) + _dot(sh, wl) + b_ref[0]


def _ada(cond, w_ada, b_ada):
    tn = 1024
    n_out = N_MOD * D_MODEL
    return pl.pallas_call(
        _ada_kernel,
        grid=(DEPTH, n_out // tn),
        in_specs=[
            pl.BlockSpec((8, D_MODEL), lambda l, j: (0, 0)),
            pl.BlockSpec((1, D_MODEL, tn), lambda l, j: (l, 0, j)),
            pl.BlockSpec((1, 1, tn), lambda l, j: (l, 0, j)),
        ],
        out_specs=pl.BlockSpec((1, 8, tn), lambda l, j: (l, 0, j)),
        out_shape=jax.ShapeDtypeStruct((DEPTH, 8, n_out), F32),
        compiler_params=_params(("parallel", "parallel")),
        name="ada",
    )(cond, w_ada, b_ada.reshape(DEPTH, 1, n_out))


def _ffn_kernel(x_ref, mod_ref, g_ref, wup_ref, wdn_ref, o_ref, *, row0):
    x = x_ref[...]
    h = _norm_mod(x, g_ref[...], mod_ref[0, row0:row0 + 1, :], mod_ref[0, row0 + 1:row0 + 2, :]).astype(BF16)
    acc = jnp.zeros(x.shape, F32)
    for j in range(D_FF // FF_CHUNK):
        c0 = j * FF_CHUNK
        a = _dot(h, wup_ref[:, c0:c0 + FF_CHUNK])
        u = _dot(h, wup_ref[:, D_FF + c0:D_FF + c0 + FF_CHUNK])
        g = (a * jax.nn.sigmoid(a) * u).astype(BF16)
        acc = acc + _dot(g, wdn_ref[c0:c0 + FF_CHUNK, :])
    o_ref[...] = x + 0.5 * mod_ref[0, row0 + 2:row0 + 3, :] * acc


def _ffn(x, mod_l, g, w_up, w_dn, row0):
    return pl.pallas_call(
        functools.partial(_ffn_kernel, row0=row0),
        grid=(N_TOK // TM,),
        in_specs=[
            pl.BlockSpec((TM, D_MODEL), lambda i: (i, 0)),
            pl.BlockSpec((1, N_MOD, D_MODEL), lambda i: (_mod_set(i), 0, 0)),
            _resident((1, D_MODEL)),
            _resident((D_MODEL, 2 * D_FF)),
            _resident((D_FF, D_MODEL)),
        ],
        out_specs=pl.BlockSpec((TM, D_MODEL), lambda i: (i, 0)),
        out_shape=jax.ShapeDtypeStruct((N_TOK, D_MODEL), F32),
        compiler_params=_params(("parallel",)),
        name="ffn",
    )(x, mod_l, g.reshape(1, D_MODEL), w_up, w_dn)


def _mixin_kernel(x_ref, mod_ref, g_ref, wm_ref, wg_ref, bg_ref, gq_ref, gk_ref, seg_ref, segt_ref,
                  cos_ref, sin_ref,
                  qa_ref, ka_ref, va_ref, qb_ref, kb_ref, vb_ref, ob_ref, gt_ref, br_ref):
    x = x_ref[...]
    h = _norm_mod(x, g_ref[...], mod_ref[0, 3:4, :], mod_ref[0, 4:5, :]).astype(BF16)

    def proj(c0, c1):
        return _dot(h, wm_ref[:, c0:c1])

    def head_norm(p, gain):
        hi, lo = _split2(p * p)
        ssum = _dot(hi, seg_ref[...]) + _dot(lo, seg_ref[...])
        rh, rl = _split2(lax.rsqrt(ssum * (1.0 / HD_A) + EPS))
        return p * (_dot(rh, segt_ref[...]) + _dot(rl, segt_ref[...])) * gain

    cosf = cos_ref[...]
    sinf = sin_ref[...]

    def rope(p):
        parts = []
        for hh in range(H_B):
            ph = p[:, HD_B * hh:HD_B * (hh + 1)]
            parts.append(ph * cosf + pltpu.roll(ph, HD_B // 2, 1) * sinf)
        return jnp.concatenate(parts, axis=1)

    c = 0
    qa_ref[...] = head_norm(proj(c, c + A_WIDTH), gq_ref[...]).astype(BF16)
    c += A_WIDTH
    ka_ref[...] = head_norm(proj(c, c + A_WIDTH), gk_ref[...])
    c += A_WIDTH
    va_ref[...] = proj(c, c + A_WIDTH)
    c += A_WIDTH
    qb_ref[...] = rope(proj(c, c + B_WIDTH)).astype(BF16)
    c += B_WIDTH
    kb_ref[...] = (rope(proj(c, c + B_WIDTH)) * KSCALE).astype(BF16)
    c += B_WIDTH
    vb_ref[...] = proj(c, c + B_WIDTH).astype(BF16)
    c += B_WIDTH
    ob_ref[...] = jax.nn.sigmoid(proj(c, c + B_WIDTH)).astype(BF16)
    c += B_WIDTH
    for j in range(2 * D_MODEL // 512):
        br_ref[:, 512 * j:512 * (j + 1)] = jax.nn.sigmoid(proj(c + 512 * j, c + 512 * (j + 1))).astype(BF16)

    pg = _dot(h, wg_ref[...]) + bg_ref[...]
    ls = jnp.minimum(pg, 0.0) - jnp.log1p(jnp.exp(-jnp.abs(pg)))
    lane = lax.broadcasted_iota(jnp.int32, pg.shape, 1)
    is_f = ((lane >= H_B) & (lane < 2 * H_B)) | ((lane >= 3 * H_B) & (lane < 4 * H_B))
    gt_ref[...] = jnp.where(is_f, ls, pg)


def _mixin(x, mod_l, g, w_main, w_gt, b_gt, gq, gk, seg, segt, cosf, sinf):
    tok = lambda w: pl.BlockSpec((TM, w), lambda i: (i, 0))
    out = lambda w, dt: jax.ShapeDtypeStruct((N_TOK, w), dt)
    return pl.pallas_call(
        _mixin_kernel,
        grid=(N_TOK // TM,),
        in_specs=[
            tok(D_MODEL),
            pl.BlockSpec((1, N_MOD, D_MODEL), lambda i: (_mod_set(i), 0, 0)),
            _resident((1, D_MODEL)),
            _resident((D_MODEL, MAIN_COLS)),
            _resident((D_MODEL, LANES)),
            _resident((1, LANES)),
            _resident((1, A_WIDTH)),
            _resident((1, A_WIDTH)),
            _resident((A_WIDTH, LANES)),
            _resident((LANES, A_WIDTH)),
            tok(HD_B),
            tok(HD_B),
        ],
        out_specs=[tok(A_WIDTH), tok(A_WIDTH), tok(A_WIDTH), tok(B_WIDTH), tok(B_WIDTH), tok(B_WIDTH),
                   tok(B_WIDTH), tok(LANES), tok(2 * D_MODEL)],
        out_shape=[out(A_WIDTH, BF16), out(A_WIDTH, F32), out(A_WIDTH, F32),
                   out(B_WIDTH, BF16), out(B_WIDTH, BF16), out(B_WIDTH, BF16),
                   out(B_WIDTH, BF16), out(LANES, F32), out(2 * D_MODEL, BF16)],
        compiler_params=_params(("parallel",)),
        name="mixin",
    )(x, mod_l, g.reshape(1, D_MODEL), w_main, w_gt, b_gt, gq, gk, seg, segt, cosf, sinf)


def _ctx_attn_kernel(q_ref, k_ref, v_ref, o_ref):
    q = q_ref[...]
    k = k_ref[...].astype(BF16)
    v = v_ref[...].astype(BF16)
    outs = []
    for hh in range(H_A):
        sl = slice(HD_A * hh, HD_A * (hh + 1))
        s = _dot_nt(q[:, sl], k[:, sl]) * ASCALE
        p = jnp.exp(s - jnp.max(s, axis=-1, keepdims=True))
        l = jnp.sum(p, axis=-1, keepdims=True)
        outs.append(_dot(p.astype(BF16), v[:, sl]) / l)
    o_ref[...] = jnp.concatenate(outs, axis=1).astype(BF16)


def _ctx_attn(qa, ka, va):
    blk = pl.BlockSpec((CTX_LEN, A_WIDTH), lambda b: (b, 0))
    return pl.pallas_call(
        _ctx_attn_kernel,
        grid=(N_CTX_B,),
        in_specs=[blk, blk, blk],
        out_specs=blk,
        out_shape=jax.ShapeDtypeStruct((N_CTX, A_WIDTH), BF16),
        compiler_params=_params(("parallel",)),
        name="ctx_attn",
    )(qa, ka, va)


def _win_start(r):
    return jnp.clip(r - WIN_H // 2, 0, LAT_ROWS - WIN_H)


def _na_kernel(q_ref, k_ref, v_ref, ck_ref, cv_ref, bias_ref, o_ref):
    r = pl.program_id(1)
    base = pl.multiple_of(_win_start(r) * GRID_W, GRID_W)
    n_win = WIN_H * GRID_W
    kw = k_ref[pl.ds(base, n_win), :].astype(BF16)
    vw = v_ref[pl.ds(base, n_win), :].astype(BF16)
    ck = ck_ref[0].astype(BF16)
    cv = cv_ref[0].astype(BF16)
    q = q_ref[...]
    qc = lax.broadcasted_iota(jnp.int32, (GRID_W, n_win), 0)
    kc = lax.broadcasted_iota(jnp.int32, (GRID_W, n_win), 1) & (GRID_W - 1)
    c0 = jnp.clip(qc - WIN_W // 2, 0, GRID_W - WIN_W)
    mask = (kc >= c0) & (kc < c0 + WIN_W)
    outs = []
    for hh in range(H_A):
        sl = slice(HD_A * hh, HD_A * (hh + 1))
        qh = q[:, sl]
        sw = jnp.where(mask, _dot_nt(qh, kw[:, sl]) * ASCALE + bias_ref[0, hh], -1e30)
        sx = _dot_nt(qh, ck[:, sl]) * ASCALE
        m = jnp.maximum(jnp.max(sw, axis=-1, keepdims=True), jnp.max(sx, axis=-1, keepdims=True))
        pw = jnp.exp(sw - m)
        px = jnp.exp(sx - m)
        l = jnp.sum(pw, axis=-1, keepdims=True) + jnp.sum(px, axis=-1, keepdims=True)
        outs.append((_dot(pw.astype(BF16), vw[:, sl]) + _dot(px.astype(BF16), cv[:, sl])) / l)
    o_ref[...] = jnp.concatenate(outs, axis=1).astype(BF16)


def _na_attn(qa, ka, va, ck, cv, bias_l):
    q_blocks0 = N_CTX // GRID_W
    kv_blocks0 = N_CTX // LAT_LEN
    kv = pl.BlockSpec((LAT_LEN, A_WIDTH), lambda b, r: (kv_blocks0 + b, 0))
    cache = pl.BlockSpec((1, ck.shape[1], A_WIDTH), lambda b, r: (b, 0, 0))
    return pl.pallas_call(
        _na_kernel,
        grid=(N_LAT_B, LAT_ROWS),
        in_specs=[
            pl.BlockSpec((GRID_W, A_WIDTH), lambda b, r: (q_blocks0 + b * LAT_ROWS + r, 0)),
            kv, kv, cache, cache,
            pl.BlockSpec((1, H_A, GRID_W, WIN_H * GRID_W),
                         lambda b, r: (_win_start(r) - r + WIN_H - 1, 0, 0, 0)),
        ],
        out_specs=pl.BlockSpec((GRID_W, A_WIDTH), lambda b, r: (b * LAT_ROWS + r, 0)),
        out_shape=jax.ShapeDtypeStruct((N_LAT, A_WIDTH), BF16),
        compiler_params=_params(("parallel", "arbitrary")),
        name="na_attn",
    )(qa, ka, va, ck, cv, bias_l)


def _mlstm_kernel(*refs, n_chunks, has_init, want_state):
    dirs = (refs[0:4], refs[4:8])
    pos = 8
    if has_init:
        c0_ref, n0_ref, m0_ref = refs[pos:pos + 3]
        pos += 3
    h_refs = refs[pos:pos + 2]
    pos += 2
    if want_state:
        cf_ref, nf_ref, mf_ref = refs[pos:pos + 3]
        pos += 3
    cs_ref, ns_ref, ms_ref = refs[pos:pos + 3]
    c = pl.program_id(1)
    n_state = 2 * H_B

    @pl.when(c == 0)
    def _():
        if has_init:
            for i in range(n_state):
                cs_ref[i] = c0_ref[0, i].T
            ns_ref[...] = n0_ref[0]
            ms_ref[...] = m0_ref[0]
        else:
            cs_ref[...] = jnp.zeros(cs_ref.shape, F32)
            ns_ref[...] = jnp.zeros(ns_ref.shape, F32)
            ms_ref[...] = jnp.zeros(ms_ref.shape, F32)

    row = lax.broadcasted_iota(jnp.int32, (CHUNK, CHUNK), 0)
    col = lax.broadcasted_iota(jnp.int32, (CHUNK, CHUNK), 1)
    for d in range(2):
        q_ref, k_ref, v_ref, g_ref = dirs[d]
        tri = (col <= row) if d == 0 else (col >= row)
        tri_b = jnp.where(tri, 1.0, 0.0).astype(BF16)
        gates = g_ref[...]
        g1, g2, g3 = _split3(gates)
        bsum = _dot(tri_b, g1) + _dot(tri_b, g2) + _dot(tri_b, g3)
        gates_t = gates.T
        bsum_t = bsum.T
        q = q_ref[...]
        v = v_ref[...]
        q32 = q.astype(F32)
        k32 = k_ref[...].astype(F32)
        v32 = v.astype(F32)
        hs = []
        for hh in range(H_B):
            i = d * H_B + hh
            ci = 2 * H_B * d + hh
            cf = ci + H_B
            sl = slice(HD_B * hh, HD_B * (hh + 1))
            qh = q[:, sl]
            kh32 = k32[:, sl]
            k_t = kh32.T.astype(BF16)
            b_col = bsum[:, cf:cf + 1]
            b_row = bsum_t[cf:cf + 1, :]
            i_row = gates_t[ci:ci + 1, :]
            i_col = gates[:, ci:ci + 1]
            m_prev = ms_ref[i:i + 1, 0:1]
            dm = jnp.where(tri, b_col - b_row + i_row, -jnp.inf)
            inter = b_col + m_prev
            m_t = jnp.maximum(inter, jnp.max(dm, axis=-1, keepdims=True))
            w = jnp.exp(dm - m_t)
            a = jnp.exp(inter - m_t)
            s = _dot(qh, k_t) * w
            c_kv = cs_ref[i]
            num = a * _dot(qh, c_kv.astype(BF16)) + _dot(s.astype(BF16), v[:, sl])
            n_row = ns_ref[i:i + 1, :]
            den = (a * jnp.sum(q32[:, sl] * n_row, axis=-1, keepdims=True)
                   + jnp.sum(s, axis=-1, keepdims=True))
            hs.append(num / jnp.maximum(jnp.abs(den), jnp.exp(-m_t)))
            b_end = b_col[CHUNK - 1:CHUNK, :] if d == 0 else b_col[0:1, :]
            gl = b_end - b_col + i_col
            m_new = jnp.maximum(b_end + m_prev, jnp.max(gl, axis=0, keepdims=True))
            a_s = jnp.exp(b_end + m_prev - m_new)
            w_s = jnp.exp(gl - m_new)
            cs_ref[i] = a_s * c_kv + _dot(k_t, (v32[:, sl] * w_s).astype(BF16))
            ns_ref[i:i + 1, :] = a_s * n_row + jnp.sum(w_s * kh32, axis=0, keepdims=True)
            ms_ref[i:i + 1, :] = jnp.broadcast_to(m_new, (1, LANES))
        h_refs[d][...] = jnp.concatenate(hs, axis=1)

    if want_state:
        @pl.when(c == n_chunks - 1)
        def _():
            for i in range(n_state):
                cf_ref[0, i] = cs_ref[i].T
            nf_ref[0] = ns_ref[...]
            mf_ref[0] = ms_ref[...]


def _mlstm(qb, kb, vb, gt, row0, n_batch, seq_len, init=None, want_state=False):
    n_chunks = seq_len // CHUNK
    blk0 = row0 // CHUNK
    n_state = 2 * H_B

    def fwd(w):
        return pl.BlockSpec((CHUNK, w), lambda b, c: (blk0 + b * n_chunks + c, 0))

    def bwd(w):
        return pl.BlockSpec((CHUNK, w), lambda b, c: (blk0 + b * n_chunks + n_chunks - 1 - c, 0))

    c_spec = pl.BlockSpec((1, n_state, HD_B, HD_B), lambda b, c: (b, 0, 0, 0))
    v_spec = pl.BlockSpec((1, n_state, LANES), lambda b, c: (b, 0, 0))
    in_specs = [fwd(B_WIDTH), fwd(B_WIDTH), fwd(B_WIDTH), fwd(LANES),
                bwd(B_WIDTH), bwd(B_WIDTH), bwd(B_WIDTH), bwd(LANES)]
    args = [qb, kb, vb, gt, qb, kb, vb, gt]
    if init is not None:
        in_specs += [c_spec, v_spec, v_spec]
        args += list(init)
    n_rows = n_batch * seq_len
    out_specs = [pl.BlockSpec((CHUNK, B_WIDTH), lambda b, c: (b * n_chunks + c, 0)),
                 pl.BlockSpec((CHUNK, B_WIDTH), lambda b, c: (b * n_chunks + n_chunks - 1 - c, 0))]
    out_shape = [jax.ShapeDtypeStruct((n_rows, B_WIDTH), F32)] * 2
    if want_state:
        out_specs += [c_spec, v_spec, v_spec]
        out_shape += [jax.ShapeDtypeStruct((n_batch, n_state, HD_B, HD_B), F32),
                      jax.ShapeDtypeStruct((n_batch, n_state, LANES), F32),
                      jax.ShapeDtypeStruct((n_batch, n_state, LANES), F32)]
    return pl.pallas_call(
        functools.partial(_mlstm_kernel, n_chunks=n_chunks, has_init=init is not None, want_state=want_state),
        grid=(n_batch, n_chunks),
        in_specs=in_specs,
        out_specs=out_specs,
        out_shape=out_shape,
        scratch_shapes=[pltpu.VMEM((n_state, HD_B, HD_B), F32),
                        pltpu.VMEM((n_state, LANES), F32),
                        pltpu.VMEM((n_state, LANES), F32)],
        compiler_params=_params(("parallel", "arbitrary")),
        name="mlstm",
    )(*args)


def _mixout_kernel(x_ref, mod_ref, attc_ref, attl_ref, hfc_ref, hbc_ref, hfl_ref, hbl_ref, ob_ref, br_ref,
                   ghn_ref, wpa_ref, wpb_ref, wo_ref, o_ref):
    is_ctx = pl.program_id(0) < N_CTX // TM
    att = jnp.where(is_ctx, attc_ref[...], attl_ref[...])
    hsum = jnp.where(is_ctx, hfc_ref[...] + hbc_ref[...], hfl_ref[...] + hbl_ref[...])
    parts = []
    for hh in range(H_B):
        seg = hsum[:, HD_B * hh:HD_B * (hh + 1)]
        parts.append(seg * lax.rsqrt(jnp.mean(seg * seg, axis=-1, keepdims=True) + EPS))
    hn = (jnp.concatenate(parts, axis=1) * ghn_ref[...]) * ob_ref[...].astype(F32)
    y = (br_ref[:, :D_MODEL].astype(F32) * _dot(att, wpa_ref[...])
         + br_ref[:, D_MODEL:].astype(F32) * _dot(hn.astype(BF16), wpb_ref[...]))
    o_ref[...] = x_ref[...] + mod_ref[0, 5:6, :] * _dot(y.astype(BF16), wo_ref[...])


def _mixout(x, mod_l, att_c, att_l, hf_c, hb_c, hf_l, hb_l, sob, sbr, g_hn, w_pa, w_pb, w_o):
    n_ctx_tiles = N_CTX // TM
    tok = lambda w: pl.BlockSpec((TM, w), lambda i: (i, 0))
    ctx = lambda w: pl.BlockSpec((TM, w), lambda i: (jnp.minimum(i, n_ctx_tiles - 1), 0))
    lat = lambda w: pl.BlockSpec((TM, w), lambda i: (jnp.maximum(i - n_ctx_tiles, 0), 0))
    return pl.pallas_call(
        _mixout_kernel,
        grid=(N_TOK // TM,),
        in_specs=[
            tok(D_MODEL),
            pl.BlockSpec((1, N_MOD, D_MODEL), lambda i: (_mod_set(i), 0, 0)),
            ctx(A_WIDTH), lat(A_WIDTH),
            ctx(B_WIDTH), ctx(B_WIDTH), lat(B_WIDTH), lat(B_WIDTH),
            tok(B_WIDTH), tok(2 * D_MODEL),
            _resident((1, B_WIDTH)),
            _resident((A_WIDTH, D_MODEL)),
            _resident((B_WIDTH, D_MODEL)),
            _resident((D_MODEL, D_MODEL)),
        ],
        out_specs=tok(D_MODEL),
        out_shape=jax.ShapeDtypeStruct((N_TOK, D_MODEL), F32),
        compiler_params=_params(("parallel",)),
        name="mixout",
    )(x, mod_l, att_c, att_l, hf_c, hb_c, hf_l, hb_l, sob, sbr, g_hn.reshape(1, B_WIDTH), w_pa, w_pb, w_o)


def _rope_tables():
    half = HD_B // 2
    nf = half // 2
    inv = ROPE_THETA ** (-jnp.arange(nf, dtype=F32) / nf)
    t = jnp.arange(LAT_LEN)
    rowp = (t // GRID_W).astype(F32)
    colp = (t % GRID_W).astype(F32)
    ang = jnp.concatenate([rowp[:, None] * inv, colp[:, None] * inv], axis=-1)
    cos, sin = jnp.cos(ang), jnp.sin(ang)
    cos_l = jnp.tile(jnp.concatenate([cos, cos], axis=-1), (N_LAT_B, 1))
    sin_l = jnp.tile(jnp.concatenate([-sin, sin], axis=-1), (N_LAT_B, 1))
    cosf = jnp.concatenate([jnp.ones((N_CTX, HD_B), F32), cos_l], axis=0)
    sinf = jnp.concatenate([jnp.zeros((N_CTX, HD_B), F32), sin_l], axis=0)
    return cosf, sinf


def _bias_tables(rpb):
    cidx = np.arange(GRID_W)
    dc = np.clip(cidx[None, :] - cidx[:, None] + (WIN_W - 1), 0, 2 * WIN_W - 2)
    t = rpb.astype(F32)[:, :, :, dc]
    tabs = []
    for d0 in range(WIN_H):
        w = jnp.transpose(t[:, :, d0:d0 + WIN_H], (0, 1, 3, 2, 4))
        tabs.append(w.reshape(DEPTH, H_A, GRID_W, WIN_H * GRID_W))
    return jnp.stack(tabs, axis=1)


def kernel(x_prompt, x_sample, cache_k, cache_v, state_C, state_n, state_m, c, c_ctx, w_ada, b_ada, g_norm,
           w_up1, w_dn1, w_in, b_gates, g_qn, g_kn, rpb, g_hn, w_pa, w_pb, w_o, w_up2, w_dn2):
    assert x_prompt.shape == (N_CTX_B, CTX_LEN, D_MODEL) and x_sample.shape == (N_LAT_B, LAT_LEN, D_MODEL)
    past_len = cache_k.shape[2]

    gate0 = 3 * A_WIDTH + 4 * B_WIDTH
    w_main = jnp.concatenate([w_in[..., :gate0], w_in[..., gate0 + N_GATE:]], axis=-1).astype(BF16)
    w_gt = jnp.pad(w_in[..., gate0:gate0 + N_GATE], ((0, 0), (0, 0), (0, LANES - N_GATE))).astype(BF16)
    b_gt = jnp.pad(b_gates, ((0, 0), (0, LANES - N_GATE))).reshape(DEPTH, 1, LANES)
    w_up1b, w_dn1b, w_up2b, w_dn2b = (w.astype(BF16) for w in (w_up1, w_dn1, w_up2, w_dn2))
    w_pab, w_pbb, w_ob = (w.astype(BF16) for w in (w_pa, w_pb, w_o))
    gq = jnp.tile(g_qn, (1, H_A)).reshape(DEPTH, 1, A_WIDTH)
    gk = jnp.tile(g_kn, (1, H_A)).reshape(DEPTH, 1, A_WIDTH)
    seg_np = (np.arange(A_WIDTH)[:, None] // HD_A == np.arange(LANES)[None, :]).astype(np.float32)
    seg = jnp.asarray(seg_np, BF16)
    segt = jnp.asarray(seg_np.T, BF16)
    cosf, sinf = _rope_tables()
    bias = _bias_tables(rpb)

    cond = jnp.concatenate([c_ctx[None], c, jnp.zeros((8 - 1 - N_LAT_B, D_MODEL), F32)], axis=0)
    mod = _ada(cond, w_ada, b_ada).reshape(DEPTH, 8, N_MOD, D_MODEL)[:, :1 + N_LAT_B]

    n_state = 2 * H_B
    x = jnp.concatenate([x_prompt.reshape(N_CTX, D_MODEL), x_sample.reshape(N_LAT, D_MODEL)], axis=0)
    ks_out, vs_out, cs_out, ns_out, ms_out = [], [], [], [], []
    for l in range(DEPTH):
        mod_l = mod[l]
        x = _ffn(x, mod_l, g_norm[l, 0], w_up1b[l], w_dn1b[l], 0)
        qa, ka, va, qb, kb, vb, sob, gt, sbr = _mixin(
            x, mod_l, g_norm[l, 1], w_main[l], w_gt[l], b_gt[l], gq[l], gk[l], seg, segt, cosf, sinf)
        att_c = _ctx_attn(qa, ka, va)
        att_l = _na_attn(qa, ka, va,
                         cache_k[:, l].reshape(N_LAT_B, past_len, A_WIDTH),
                         cache_v[:, l].reshape(N_LAT_B, past_len, A_WIDTH), bias[l])
        hf_c, hb_c, c_fin, n_fin, m_fin = _mlstm(qb, kb, vb, gt, 0, N_CTX_B, CTX_LEN, want_state=True)
        init = (state_C[:, l].reshape(N_LAT_B, n_state, HD_B, HD_B),
                state_n[:, l].reshape(N_LAT_B, n_state, HD_B),
                jnp.broadcast_to(state_m[:, l].reshape(N_LAT_B, n_state, 1), (N_LAT_B, n_state, LANES)))
        hf_l, hb_l = _mlstm(qb, kb, vb, gt, N_CTX, N_LAT_B, LAT_LEN, init=init)
        x = _mixout(x, mod_l, att_c, att_l, hf_c, hb_c, hf_l, hb_l, sob, sbr, g_hn[l], w_pab[l], w_pbb[l], w_ob[l])
        x = _ffn(x, mod_l, g_norm[l, 2], w_up2b[l], w_dn2b[l], 6)
        ks_out.append(ka[:N_CTX].reshape(N_CTX_B, CTX_LEN, H_A, HD_A))
        vs_out.append(va[:N_CTX].reshape(N_CTX_B, CTX_LEN, H_A, HD_A))
        cs_out.append(c_fin.reshape(N_CTX_B, 2, H_B, HD_B, HD_B))
        ns_out.append(n_fin.reshape(N_CTX_B, 2, H_B, HD_B))
        ms_out.append(m_fin[:, :, 0].reshape(N_CTX_B, 2, H_B))

    return (x[:N_CTX].reshape(N_CTX_B, CTX_LEN, D_MODEL),
            x[N_CTX:].reshape(N_LAT_B, LAT_LEN, D_MODEL),
            jnp.stack(ks_out, axis=1), jnp.stack(vs_out, axis=1),
            jnp.stack(cs_out, axis=1), jnp.stack(ns_out, axis=1), jnp.stack(ms_out, axis=1))
```

```python
import functools

import jax
import jax.numpy as jnp
import numpy as np
from jax import lax
from jax.experimental import pallas as pl
from jax.experimental.pallas import tpu as pltpu

F32 = jnp.float32
BF16 = jnp.bfloat16

D_MODEL = 1024
DEPTH = 4
N_CTX_B, CTX_LEN = 32, 256
N_LAT_B, LAT_LEN = 2, 2048
N_CTX = N_CTX_B * CTX_LEN
N_LAT = N_LAT_B * LAT_LEN
N_TOK = N_CTX + N_LAT
GRID_W, WIN_H, WIN_W = 64, 8, 16
LAT_ROWS = LAT_LEN // GRID_W
H_A, HD_A = 8, 64
H_B, HD_B = 4, 128
A_WIDTH = H_A * HD_A
B_WIDTH = H_B * HD_B
D_FF = 2816
N_MOD = 9
CHUNK = 128
ROPE_THETA = 10000.0
EPS = 1e-6
KSCALE = HD_B ** -0.5
ASCALE = HD_A ** -0.5
N_GATE = 4 * H_B
MAIN_COLS = 3 * A_WIDTH + 4 * B_WIDTH + 2 * D_MODEL

LANES = 128
TM = 512
FF_CHUNK = 256
VMEM_LIMIT = 52 * 1024 * 1024


def _dot(a, b):
    return jnp.dot(a, b, preferred_element_type=F32)


def _dot_nt(a, b):
    return lax.dot_general(a, b, (((1,), (1,)), ((), ())), preferred_element_type=F32)


def _split2(x):
    hi = x.astype(BF16)
    lo = (x - hi.astype(F32)).astype(BF16)
    return hi, lo


def _split3(x):
    hi = x.astype(BF16)
    r1 = x - hi.astype(F32)
    mid = r1.astype(BF16)
    lo = (r1 - mid.astype(F32)).astype(BF16)
    return hi, mid, lo


def _norm_mod(x, g, shift, scale):
    ms = jnp.mean(x * x, axis=-1, keepdims=True)
    return (x * lax.rsqrt(ms + EPS) * g) * (1.0 + scale) + shift


def _params(sem):
    return pltpu.CompilerParams(dimension_semantics=sem, vmem_limit_bytes=VMEM_LIMIT)


def _resident(shape):
    nd = len(shape)
    return pl.BlockSpec(shape, lambda *_: (0,) * nd, pipeline_mode=pl.Buffered(1))


def _mod_set(i):
    n_ctx_tiles = N_CTX // TM
    return jnp.where(i < n_ctx_tiles, 0, 1 + (i - n_ctx_tiles) // (LAT_LEN // TM))


def _ada_kernel(c_ref, w_ref, b_ref, o_ref):
    c = c_ref[...]
    s = c * jax.nn.sigmoid(c)
    sh, sl = _split2(s)
    wh, wl = _split2(w_ref[0])
    o_ref[0] = _dot(sh, wh) + _dot(sl, wh) + _dot(sh, wl) + b_ref[0]


def _ada(cond, w_ada, b_ada):
    tn = 1024
    n_out = N_MOD * D_MODEL
    return pl.pallas_call(
        _ada_kernel,
        grid=(DEPTH, n_out // tn),
        in_specs=[
            pl.BlockSpec((8, D_MODEL), lambda l, j: (0, 0)),
            pl.BlockSpec((1, D_MODEL, tn), lambda l, j: (l, 0, j)),
            pl.BlockSpec((1, 1, tn), lambda l, j: (l, 0, j)),
        ],
        out_specs=pl.BlockSpec((1, 8, tn), lambda l, j: (l, 0, j)),
        out_shape=jax.ShapeDtypeStruct((DEPTH, 8, n_out), F32),
        compiler_params=_params(("parallel", "parallel")),
        name="ada",
    )(cond, w_ada, b_ada.reshape(DEPTH, 1, n_out))


def _tok_spec(w):
    return pl.BlockSpec((TM, w), lambda i: (i, 0))


def _ctx_spec(w):
    return pl.BlockSpec((TM, w), lambda i: (jnp.minimum(i, N_CTX // TM - 1), 0))


def _lat_spec(w):
    return pl.BlockSpec((TM, w), lambda i: (jnp.maximum(i - N_CTX // TM, 0), 0))


def _ffn_kernel(*refs, row0, split_in, split_out):
    n_x = 2 if split_in else 1
    mod_ref, g_ref, wup_ref, wdn_ref = refs[n_x:n_x + 4]
    o_refs = refs[n_x + 4:]
    is_ctx = pl.program_id(0) < N_CTX // TM
    x = jnp.where(is_ctx, refs[0][...], refs[1][...]) if split_in else refs[0][...]
    h = _norm_mod(x, g_ref[...], mod_ref[0, row0:row0 + 1, :], mod_ref[0, row0 + 1:row0 + 2, :]).astype(BF16)
    acc = jnp.zeros(x.shape, F32)
    for j in range(D_FF // FF_CHUNK):
        c0 = j * FF_CHUNK
        a = _dot(h, wup_ref[:, c0:c0 + FF_CHUNK])
        u = _dot(h, wup_ref[:, D_FF + c0:D_FF + c0 + FF_CHUNK])
        g = (a * jax.nn.sigmoid(a) * u).astype(BF16)
        acc = acc + _dot(g, wdn_ref[c0:c0 + FF_CHUNK, :])
    y = x + 0.5 * mod_ref[0, row0 + 2:row0 + 3, :] * acc
    if split_out:
        @pl.when(is_ctx)
        def _():
            o_refs[0][...] = y

        @pl.when(jnp.logical_not(is_ctx))
        def _():
            o_refs[1][...] = y
    else:
        o_refs[0][...] = y


def _ffn(xs, mod_l, g, w_up, w_dn, row0, split_out=False):
    split_in = isinstance(xs, tuple)
    xs = xs if split_in else (xs,)
    x_specs = [_ctx_spec(D_MODEL), _lat_spec(D_MODEL)] if split_in else [_tok_spec(D_MODEL)]
    if split_out:
        out_specs = [_ctx_spec(D_MODEL), _lat_spec(D_MODEL)]
        out_shape = [jax.ShapeDtypeStruct((N_CTX, D_MODEL), F32), jax.ShapeDtypeStruct((N_LAT, D_MODEL), F32)]
    else:
        out_specs = _tok_spec(D_MODEL)
        out_shape = jax.ShapeDtypeStruct((N_TOK, D_MODEL), F32)
    return pl.pallas_call(
        functools.partial(_ffn_kernel, row0=row0, split_in=split_in, split_out=split_out),
        grid=(N_TOK // TM,),
        in_specs=x_specs + [
            pl.BlockSpec((1, N_MOD, D_MODEL), lambda i: (_mod_set(i), 0, 0)),
            _resident((1, D_MODEL)),
            _resident((D_MODEL, 2 * D_FF)),
            _resident((D_FF, D_MODEL)),
        ],
        out_specs=out_specs,
        out_shape=out_shape,
        compiler_params=_params(("arbitrary",)),
        name="ffn",
    )(*xs, mod_l, g.reshape(1, D_MODEL), w_up, w_dn)


def _mixin_kernel(x_ref, mod_ref, g_ref, wm_ref, wg_ref, bg_ref, gq_ref, gk_ref, seg_ref, segt_ref,
                  cos_ref, sin_ref,
                  qa_ref, ka_ref, va_ref, kc_ref, vc_ref, qb_ref, kb_ref, vb_ref, ob_ref, gt_ref, br_ref):
    is_ctx = pl.program_id(0) < N_CTX // TM
    x = x_ref[...]
    h = _norm_mod(x, g_ref[...], mod_ref[0, 3:4, :], mod_ref[0, 4:5, :]).astype(BF16)

    def proj(c0, c1):
        return _dot(h, wm_ref[:, c0:c1])

    def head_norm(p, gain):
        hi, lo = _split2(p * p)
        ssum = _dot(hi, seg_ref[...]) + _dot(lo, seg_ref[...])
        rh, rl = _split2(lax.rsqrt(ssum * (1.0 / HD_A) + EPS))
        return p * (_dot(rh, segt_ref[...]) + _dot(rl, segt_ref[...])) * gain

    cosf = cos_ref[...]
    sinf = sin_ref[...]

    def rope(p):
        parts = []
        for hh in range(H_B):
            ph = p[:, HD_B * hh:HD_B * (hh + 1)]
            parts.append(ph * cosf + pltpu.roll(ph, HD_B // 2, 1) * sinf)
        return jnp.concatenate(parts, axis=1)

    c = 0
    qa_ref[...] = (head_norm(proj(c, c + A_WIDTH), gq_ref[...]) * ASCALE).astype(BF16)
    c += A_WIDTH
    ka = head_norm(proj(c, c + A_WIDTH), gk_ref[...])
    c += A_WIDTH
    va = proj(c, c + A_WIDTH)
    c += A_WIDTH
    ka_ref[...] = ka.astype(BF16)
    va_ref[...] = va.astype(BF16)

    @pl.when(is_ctx)
    def _():
        kc_ref[...] = ka
        vc_ref[...] = va

    qb_ref[...] = rope(proj(c, c + B_WIDTH)).astype(BF16)
    c += B_WIDTH
    kb_ref[...] = (rope(proj(c, c + B_WIDTH)) * KSCALE).astype(BF16)
    c += B_WIDTH
    vb_ref[...] = proj(c, c + B_WIDTH).astype(BF16)
    c += B_WIDTH
    ob_ref[...] = jax.nn.sigmoid(proj(c, c + B_WIDTH)).astype(BF16)
    c += B_WIDTH
    for j in range(2 * D_MODEL // 512):
        br_ref[:, 512 * j:512 * (j + 1)] = jax.nn.sigmoid(proj(c + 512 * j, c + 512 * (j + 1))).astype(BF16)

    pg = _dot(h, wg_ref[...]) + bg_ref[...]
    ls = jnp.minimum(pg, 0.0) - jnp.log1p(jnp.exp(-jnp.abs(pg)))
    lane = lax.broadcasted_iota(jnp.int32, pg.shape, 1)
    is_f = ((lane >= H_B) & (lane < 2 * H_B)) | ((lane >= 3 * H_B) & (lane < 4 * H_B))
    gt_ref[...] = jnp.where(is_f, ls, pg)


def _mixin(x, mod_l, g, w_main, w_gt, b_gt, gq, gk, seg, segt, cosf, sinf):
    tok = _tok_spec
    out = lambda w, dt: jax.ShapeDtypeStruct((N_TOK, w), dt)
    ctx_out = jax.ShapeDtypeStruct((N_CTX, A_WIDTH), F32)
    return pl.pallas_call(
        _mixin_kernel,
        grid=(N_TOK // TM,),
        in_specs=[
            tok(D_MODEL),
            pl.BlockSpec((1, N_MOD, D_MODEL), lambda i: (_mod_set(i), 0, 0)),
            _resident((1, D_MODEL)),
            _resident((D_MODEL, MAIN_COLS)),
            _resident((D_MODEL, LANES)),
            _resident((1, LANES)),
            _resident((1, A_WIDTH)),
            _resident((1, A_WIDTH)),
            _resident((A_WIDTH, LANES)),
            _resident((LANES, A_WIDTH)),
            tok(HD_B),
            tok(HD_B),
        ],
        out_specs=[tok(A_WIDTH), tok(A_WIDTH), tok(A_WIDTH), _ctx_spec(A_WIDTH), _ctx_spec(A_WIDTH),
                   tok(B_WIDTH), tok(B_WIDTH), tok(B_WIDTH), tok(B_WIDTH), tok(LANES), tok(2 * D_MODEL)],
        out_shape=[out(A_WIDTH, BF16), out(A_WIDTH, BF16), out(A_WIDTH, BF16), ctx_out, ctx_out,
                   out(B_WIDTH, BF16), out(B_WIDTH, BF16), out(B_WIDTH, BF16),
                   out(B_WIDTH, BF16), out(LANES, F32), out(2 * D_MODEL, BF16)],
        compiler_params=_params(("arbitrary",)),
        name="mixin",
    )(x, mod_l, g.reshape(1, D_MODEL), w_main, w_gt, b_gt, gq, gk, seg, segt, cosf, sinf)


def _ctx_attn_kernel(q_ref, k_ref, v_ref, o_ref):
    q = q_ref[...]
    k = k_ref[...]
    v = v_ref[...]
    outs = []
    for hh in range(H_A):
        sl = slice(HD_A * hh, HD_A * (hh + 1))
        s = _dot_nt(q[:, sl], k[:, sl])
        p = jnp.exp(s - jnp.max(s, axis=-1, keepdims=True))
        l = jnp.sum(p, axis=-1, keepdims=True)
        outs.append(_dot(p.astype(BF16), v[:, sl]) / l)
    o_ref[...] = jnp.concatenate(outs, axis=1).astype(BF16)


def _ctx_attn(qa, ka, va):
    blk = pl.BlockSpec((CTX_LEN, A_WIDTH), lambda b: (b, 0))
    return pl.pallas_call(
        _ctx_attn_kernel,
        grid=(N_CTX_B,),
        in_specs=[blk, blk, blk],
        out_specs=blk,
        out_shape=jax.ShapeDtypeStruct((N_CTX, A_WIDTH), BF16),
        compiler_params=_params(("parallel",)),
        name="ctx_attn",
    )(qa, ka, va)


NA_RB = WIN_H // 2
NA_KR = NA_RB + WIN_H - 1
NA_Q = NA_RB * GRID_W
NA_K = NA_KR * GRID_W
NA_BLOCKS = LAT_ROWS // NA_RB
assert LAT_ROWS % NA_RB == 0 and NA_BLOCKS >= 3 and N_CTX % N_LAT == 0


def _na_key_row0(j):
    return jnp.clip(NA_RB * j - WIN_H // 2, 0, LAT_ROWS - NA_KR)


def _na_kernel(q_ref, k_ref, v_ref, ck_ref, cv_ref, bias_ref, o_ref):
    j = pl.program_id(0)
    b = pl.program_id(1)
    base = pl.multiple_of(b * LAT_LEN + _na_key_row0(j) * GRID_W, GRID_W)
    kw = k_ref[pl.ds(base, NA_K), :]
    vw = v_ref[pl.ds(base, NA_K), :]
    ck = ck_ref[b].astype(BF16)
    cv = cv_ref[b].astype(BF16)
    q = q_ref[...]
    outs = []
    for hh in range(H_A):
        sl = slice(HD_A * hh, HD_A * (hh + 1))
        qh = q[:, sl]
        sw = _dot_nt(qh, kw[:, sl]) + bias_ref[0, hh]
        sx = _dot_nt(qh, ck[:, sl])
        m = jnp.maximum(jnp.max(sw, axis=-1, keepdims=True), jnp.max(sx, axis=-1, keepdims=True))
        pw = jnp.exp(sw - m)
        px = jnp.exp(sx - m)
        l = jnp.sum(pw, axis=-1, keepdims=True) + jnp.sum(px, axis=-1, keepdims=True)
        outs.append((_dot(pw.astype(BF16), vw[:, sl]) + _dot(px.astype(BF16), cv[:, sl])) / l)
    o_ref[...] = jnp.concatenate(outs, axis=1).astype(BF16)


def _na_attn(qa, ka, va, ck, cv, bias_l):
    lat_kv = pl.BlockSpec((N_LAT, A_WIDTH), lambda j, b: (N_CTX // N_LAT, 0), pipeline_mode=pl.Buffered(1))
    cache = _resident(ck.shape)
    bias_class = lambda j: jnp.where(j == 0, 0, jnp.where(j == NA_BLOCKS - 1, 2, 1))
    return pl.pallas_call(
        _na_kernel,
        grid=(NA_BLOCKS, N_LAT_B),
        in_specs=[
            pl.BlockSpec((NA_Q, A_WIDTH), lambda j, b: (N_CTX // NA_Q + b * NA_BLOCKS + j, 0)),
            lat_kv, lat_kv, cache, cache,
            pl.BlockSpec((1, H_A, NA_Q, NA_K), lambda j, b: (bias_class(j), 0, 0, 0)),
        ],
        out_specs=pl.BlockSpec((NA_Q, A_WIDTH), lambda j, b: (b * NA_BLOCKS + j, 0)),
        out_shape=jax.ShapeDtypeStruct((N_LAT, A_WIDTH), BF16),
        compiler_params=_params(("arbitrary", "arbitrary")),
        name="na_attn",
    )(qa, ka, va, ck, cv, bias_l)


def _mlstm_kernel(*refs, n_chunks, has_init, want_state):
    dirs = (refs[0:4], refs[4:8])
    pos = 8
    if has_init:
        c0_ref, n0_ref, m0_ref = refs[pos:pos + 3]
        pos += 3
    h_refs = refs[pos:pos + 2]
    pos += 2
    if want_state:
        cf_ref, nf_ref, mf_ref = refs[pos:pos + 3]
        pos += 3
    cs_ref, ns_ref, ms_ref = refs[pos:pos + 3]
    c = pl.program_id(1)

    @pl.when(c == 0)
    def _():
        if has_init:
            cs_ref[...] = c0_ref[0]
            ns_ref[...] = n0_ref[0]
            ms_ref[...] = m0_ref[0]
        else:
            cs_ref[...] = jnp.zeros(cs_ref.shape, F32)
            ns_ref[...] = jnp.zeros(ns_ref.shape, F32)
            ms_ref[...] = jnp.zeros(ms_ref.shape, F32)

    row = lax.broadcasted_iota(jnp.int32, (CHUNK, CHUNK), 0)
    col = lax.broadcasted_iota(jnp.int32, (CHUNK, CHUNK), 1)
    for d in range(2):
        q_ref, k_ref, v_ref, g_ref = dirs[d]
        vis = (row <= col) if d == 0 else (row >= col)
        cum_b = jnp.where((col <= row) if d == 0 else (col >= row), 1.0, 0.0).astype(BF16)
        gates = g_ref[...]
        g1, g2, g3 = _split3(gates)
        bsum = _dot(cum_b, g1) + _dot(cum_b, g2) + _dot(cum_b, g3)
        gates_t = gates.T
        bsum_t = bsum.T
        q = q_ref[...]
        k = k_ref[...]
        v32 = v_ref[...].astype(F32)
        for hh in range(H_B):
            i = d * H_B + hh
            ci = 2 * H_B * d + hh
            cf = ci + H_B
            sl = slice(HD_B * hh, HD_B * (hh + 1))
            qh = q[:, sl]
            kh = k[:, sl]
            v_t = v32[:, sl].T
            b_row = bsum_t[cf:cf + 1, :]
            i_row = gates_t[ci:ci + 1, :]
            u_col = gates[:, ci:ci + 1] - bsum[:, cf:cf + 1]
            m_prev = ms_ref[i:i + 1, :]
            dm = jnp.where(vis, u_col + b_row, -jnp.inf)
            inter = b_row + m_prev
            m_t = jnp.maximum(inter, jnp.max(dm, axis=0, keepdims=True))
            a = jnp.exp(inter - m_t)
            p = _dot_nt(kh, qh) * jnp.exp(dm - m_t)
            c_vk = cs_ref[i]
            n_row = ns_ref[i:i + 1, :]
            nh, nl = _split2(jnp.broadcast_to(n_row, (8, HD_B)))
            qn = (_dot_nt(nh, qh) + _dot_nt(nl, qh))[0:1, :]
            den = a * qn + jnp.sum(p, axis=0, keepdims=True)
            num = a * _dot_nt(c_vk.astype(BF16), qh) + _dot(v_t.astype(BF16), p.astype(BF16))
            h_t = num * (1.0 / jnp.maximum(jnp.abs(den), jnp.exp(-m_t)))
            h_refs[d][:, sl] = h_t.T
            b_end = jnp.broadcast_to(b_row[:, CHUNK - 1:CHUNK] if d == 0 else b_row[:, 0:1], (1, CHUNK))
            gl = b_end - b_row + i_row
            m_new = jnp.maximum(b_end + m_prev, jnp.max(gl, axis=-1, keepdims=True))
            a_s = jnp.exp(b_end + m_prev - m_new)
            w_s = jnp.exp(gl - m_new)
            cs_ref[i] = a_s * c_vk + _dot((v_t * w_s).astype(BF16), kh)
            wh, wl = _split2(jnp.broadcast_to(w_s, (8, CHUNK)))
            ns_ref[i:i + 1, :] = a_s * n_row + (_dot(wh, kh) + _dot(wl, kh))[0:1, :]
            ms_ref[i:i + 1, :] = m_new

    if want_state:
        @pl.when(c == n_chunks - 1)
        def _():
            cf_ref[0] = cs_ref[...]
            nf_ref[0] = ns_ref[...]
            mf_ref[0] = ms_ref[...]


def _mlstm(qb, kb, vb, gt, row0, n_batch, seq_len, init=None, want_state=False):
    n_chunks = seq_len // CHUNK
    blk0 = row0 // CHUNK
    n_state = 2 * H_B

    def fwd(w):
        return pl.BlockSpec((CHUNK, w), lambda b, c: (blk0 + b * n_chunks + c, 0))

    def bwd(w):
        return pl.BlockSpec((CHUNK, w), lambda b, c: (blk0 + b * n_chunks + n_chunks - 1 - c, 0))

    c_spec = pl.BlockSpec((1, n_state, HD_B, HD_B), lambda b, c: (b, 0, 0, 0))
    v_spec = pl.BlockSpec((1, n_state, LANES), lambda b, c: (b, 0, 0))
    in_specs = [fwd(B_WIDTH), fwd(B_WIDTH), fwd(B_WIDTH), fwd(LANES),
                bwd(B_WIDTH), bwd(B_WIDTH), bwd(B_WIDTH), bwd(LANES)]
    args = [qb, kb, vb, gt, qb, kb, vb, gt]
    if init is not None:
        in_specs += [c_spec, v_spec, v_spec]
        args += list(init)
    n_rows = n_batch * seq_len
    out_specs = [pl.BlockSpec((CHUNK, B_WIDTH), lambda b, c: (b * n_chunks + c, 0)),
                 pl.BlockSpec((CHUNK, B_WIDTH), lambda b, c: (b * n_chunks + n_chunks - 1 - c, 0))]
    out_shape = [jax.ShapeDtypeStruct((n_rows, B_WIDTH), F32)] * 2
    if want_state:
        out_specs += [c_spec, v_spec, v_spec]
        out_shape += [jax.ShapeDtypeStruct((n_batch, n_state, HD_B, HD_B), F32),
                      jax.ShapeDtypeStruct((n_batch, n_state, LANES), F32),
                      jax.ShapeDtypeStruct((n_batch, n_state, LANES), F32)]
    return pl.pallas_call(
        functools.partial(_mlstm_kernel, n_chunks=n_chunks, has_init=init is not None, want_state=want_state),
        grid=(n_batch, n_chunks),
        in_specs=in_specs,
        out_specs=out_specs,
        out_shape=out_shape,
        scratch_shapes=[pltpu.VMEM((n_state, HD_B, HD_B), F32),
                        pltpu.VMEM((n_state, LANES), F32),
                        pltpu.VMEM((n_state, LANES), F32)],
        compiler_params=_params(("parallel", "arbitrary")),
        name="mlstm",
    )(*args)


def _mixout_kernel(x_ref, mod_ref, attc_ref, attl_ref, hfc_ref, hbc_ref, hfl_ref, hbl_ref, ob_ref, br_ref,
                   ghn_ref, wpa_ref, wpb_ref, wo_ref, o_ref):
    is_ctx = pl.program_id(0) < N_CTX // TM
    att = jnp.where(is_ctx, attc_ref[...], attl_ref[...])
    hsum = jnp.where(is_ctx, hfc_ref[...] + hbc_ref[...], hfl_ref[...] + hbl_ref[...])
    parts = []
    for hh in range(H_B):
        seg = hsum[:, HD_B * hh:HD_B * (hh + 1)]
        parts.append(seg * lax.rsqrt(jnp.mean(seg * seg, axis=-1, keepdims=True) + EPS))
    hn = (jnp.concatenate(parts, axis=1) * ghn_ref[...]) * ob_ref[...].astype(F32)
    y = (br_ref[:, :D_MODEL].astype(F32) * _dot(att, wpa_ref[...])
         + br_ref[:, D_MODEL:].astype(F32) * _dot(hn.astype(BF16), wpb_ref[...]))
    o_ref[...] = x_ref[...] + mod_ref[0, 5:6, :] * _dot(y.astype(BF16), wo_ref[...])


def _mixout(x, mod_l, att_c, att_l, hf_c, hb_c, hf_l, hb_l, sob, sbr, g_hn, w_pa, w_pb, w_o):
    n_ctx_tiles = N_CTX // TM
    tok = lambda w: pl.BlockSpec((TM, w), lambda i: (i, 0))
    ctx = lambda w: pl.BlockSpec((TM, w), lambda i: (jnp.minimum(i, n_ctx_tiles - 1), 0))
    lat = lambda w: pl.BlockSpec((TM, w), lambda i: (jnp.maximum(i - n_ctx_tiles, 0), 0))
    return pl.pallas_call(
        _mixout_kernel,
        grid=(N_TOK // TM,),
        in_specs=[
            tok(D_MODEL),
            pl.BlockSpec((1, N_MOD, D_MODEL), lambda i: (_mod_set(i), 0, 0)),
            ctx(A_WIDTH), lat(A_WIDTH),
            ctx(B_WIDTH), ctx(B_WIDTH), lat(B_WIDTH), lat(B_WIDTH),
            tok(B_WIDTH), tok(2 * D_MODEL),
            _resident((1, B_WIDTH)),
            _resident((A_WIDTH, D_MODEL)),
            _resident((B_WIDTH, D_MODEL)),
            _resident((D_MODEL, D_MODEL)),
        ],
        out_specs=tok(D_MODEL),
        out_shape=jax.ShapeDtypeStruct((N_TOK, D_MODEL), F32),
        compiler_params=_params(("parallel",)),
        name="mixout",
    )(x, mod_l, att_c, att_l, hf_c, hb_c, hf_l, hb_l, sob, sbr, g_hn.reshape(1, B_WIDTH), w_pa, w_pb, w_o)


def _rope_tables():
    half = HD_B // 2
    nf = half // 2
    inv = ROPE_THETA ** (-jnp.arange(nf, dtype=F32) / nf)
    t = jnp.arange(LAT_LEN)
    rowp = (t // GRID_W).astype(F32)
    colp = (t % GRID_W).astype(F32)
    ang = jnp.concatenate([rowp[:, None] * inv, colp[:, None] * inv], axis=-1)
    cos, sin = jnp.cos(ang), jnp.sin(ang)
    cos_l = jnp.tile(jnp.concatenate([cos, cos], axis=-1), (N_LAT_B, 1))
    sin_l = jnp.tile(jnp.concatenate([-sin, sin], axis=-1), (N_LAT_B, 1))
    cosf = jnp.concatenate([jnp.ones((N_CTX, HD_B), F32), cos_l], axis=0)
    sinf = jnp.concatenate([jnp.zeros((N_CTX, HD_B), F32), sin_l], axis=0)
    return cosf, sinf


def _bias_tables(rpb):
    cidx = np.arange(GRID_W)
    start_c = np.clip(cidx - WIN_W // 2, 0, GRID_W - WIN_W)
    col_ok = (cidx[None, :] >= start_c[:, None]) & (cidx[None, :] < start_c[:, None] + WIN_W)
    dc = np.clip(cidx[None, :] - cidx[:, None] + (WIN_W - 1), 0, 2 * WIN_W - 2)
    tabs = []
    for j in (0, 1, NA_BLOCKS - 1):
        r = NA_RB * j + np.arange(NA_RB)
        kr = int(np.clip(NA_RB * j - WIN_H // 2, 0, LAT_ROWS - NA_KR)) + np.arange(NA_KR)
        start_r = np.clip(r - WIN_H // 2, 0, LAT_ROWS - WIN_H)
        row_ok = (kr[None, :] >= start_r[:, None]) & (kr[None, :] < start_r[:, None] + WIN_H)
        dr = np.clip(kr[None, :] - r[:, None] + (WIN_H - 1), 0, 2 * WIN_H - 2)
        shape = (NA_RB, GRID_W, NA_KR, GRID_W)
        dr_f = np.broadcast_to(dr[:, None, :, None], shape)
        dc_f = np.broadcast_to(dc[None, :, None, :], shape)
        ok = row_ok[:, None, :, None] & col_ok[None, :, None, :]
        t = jnp.where(ok, rpb.astype(F32)[:, :, dr_f, dc_f], -1e30)
        tabs.append(t.reshape(DEPTH, H_A, NA_Q, NA_K))
    return jnp.stack(tabs, axis=1)


def kernel(x_prompt, x_sample, cache_k, cache_v, state_C, state_n, state_m, c, c_ctx, w_ada, b_ada, g_norm,
           w_up1, w_dn1, w_in, b_gates, g_qn, g_kn, rpb, g_hn, w_pa, w_pb, w_o, w_up2, w_dn2):
    assert x_prompt.shape == (N_CTX_B, CTX_LEN, D_MODEL) and x_sample.shape == (N_LAT_B, LAT_LEN, D_MODEL)
    past_len = cache_k.shape[2]

    gate0 = 3 * A_WIDTH + 4 * B_WIDTH
    w_main = jnp.concatenate([w_in[..., :gate0], w_in[..., gate0 + N_GATE:]], axis=-1).astype(BF16)
    w_gt = jnp.pad(w_in[..., gate0:gate0 + N_GATE], ((0, 0), (0, 0), (0, LANES - N_GATE))).astype(BF16)
    b_gt = jnp.pad(b_gates, ((0, 0), (0, LANES - N_GATE))).reshape(DEPTH, 1, LANES)
    w_up1b, w_dn1b, w_up2b, w_dn2b = (w.astype(BF16) for w in (w_up1, w_dn1, w_up2, w_dn2))
    w_pab, w_pbb, w_ob = (w.astype(BF16) for w in (w_pa, w_pb, w_o))
    gq = jnp.tile(g_qn, (1, H_A)).reshape(DEPTH, 1, A_WIDTH)
    gk = jnp.tile(g_kn, (1, H_A)).reshape(DEPTH, 1, A_WIDTH)
    seg_np = (np.arange(A_WIDTH)[:, None] // HD_A == np.arange(LANES)[None, :]).astype(np.float32)
    seg = jnp.asarray(seg_np, BF16)
    segt = jnp.asarray(seg_np.T, BF16)
    cosf, sinf = _rope_tables()
    bias = _bias_tables(rpb)

    cond = jnp.concatenate([c_ctx[None], c, jnp.zeros((8 - 1 - N_LAT_B, D_MODEL), F32)], axis=0)
    mod = _ada(cond, w_ada, b_ada).reshape(DEPTH, 8, N_MOD, D_MODEL)[:, :1 + N_LAT_B]

    n_state = 2 * H_B
    x = (x_prompt.reshape(N_CTX, D_MODEL), x_sample.reshape(N_LAT, D_MODEL))
    ks_out, vs_out, cs_out, ns_out, ms_out = [], [], [], [], []
    for l in range(DEPTH):
        mod_l = mod[l]
        x = _ffn(x, mod_l, g_norm[l, 0], w_up1b[l], w_dn1b[l], 0)
        qa, ka, va, k_ctx, v_ctx, qb, kb, vb, sob, gt, sbr = _mixin(
            x, mod_l, g_norm[l, 1], w_main[l], w_gt[l], b_gt[l], gq[l], gk[l], seg, segt, cosf, sinf)
        att_c = _ctx_attn(qa, ka, va)
        att_l = _na_attn(qa, ka, va,
                         cache_k[:, l].reshape(N_LAT_B, past_len, A_WIDTH),
                         cache_v[:, l].reshape(N_LAT_B, past_len, A_WIDTH), bias[l])
        hf_c, hb_c, c_fin, n_fin, m_fin = _mlstm(qb, kb, vb, gt, 0, N_CTX_B, CTX_LEN, want_state=True)
        init = (state_C[:, l].reshape(N_LAT_B, n_state, HD_B, HD_B),
                state_n[:, l].reshape(N_LAT_B, n_state, HD_B),
                jnp.broadcast_to(state_m[:, l].reshape(N_LAT_B, n_state, 1), (N_LAT_B, n_state, LANES)))
        hf_l, hb_l = _mlstm(qb, kb, vb, gt, N_CTX, N_LAT_B, LAT_LEN, init=init)
        x = _mixout(x, mod_l, att_c, att_l, hf_c, hb_c, hf_l, hb_l, sob, sbr, g_hn[l], w_pab[l], w_pbb[l], w_ob[l])
        x = _ffn(x, mod_l, g_norm[l, 2], w_up2b[l], w_dn2b[l], 6, split_out=(l == DEPTH - 1))
        ks_out.append(k_ctx.reshape(N_CTX_B, CTX_LEN, H_A, HD_A))
        vs_out.append(v_ctx.reshape(N_CTX_B, CTX_LEN, H_A, HD_A))
        cs_out.append(c_fin.reshape(N_CTX_B, 2, H_B, HD_B, HD_B))
        ns_out.append(n_fin.reshape(N_CTX_B, 2, H_B, HD_B))
        ms_out.append(m_fin[:, :, 0].reshape(N_CTX_B, 2, H_B))

    return (x[0].reshape(N_CTX_B, CTX_LEN, D_MODEL),
            x[1].reshape(N_LAT_B, LAT_LEN, D_MODEL),
            jnp.stack(ks_out, axis=1), jnp.stack(vs_out, axis=1),
            jnp.stack(cs_out, axis=1), jnp.stack(ns_out, axis=1), jnp.stack(ms_out, axis=1))
```

```python
import functools

import jax
import jax.numpy as jnp
import numpy as np
from jax import lax
from jax.experimental import pallas as pl
from jax.experimental.pallas import tpu as pltpu

F32 = jnp.float32
BF16 = jnp.bfloat16

D_MODEL = 1024
DEPTH = 4
N_CTX_B, CTX_LEN = 32, 256
N_LAT_B, LAT_LEN = 2, 2048
N_CTX = N_CTX_B * CTX_LEN
N_LAT = N_LAT_B * LAT_LEN
N_TOK = N_CTX + N_LAT
GRID_W, WIN_H, WIN_W = 64, 8, 16
LAT_ROWS = LAT_LEN // GRID_W
H_A, HD_A = 8, 64
H_B, HD_B = 4, 128
A_WIDTH = H_A * HD_A
B_WIDTH = H_B * HD_B
D_FF = 2816
N_MOD = 9
CHUNK = 128
ROPE_THETA = 10000.0
EPS = 1e-6
KSCALE = HD_B ** -0.5
ASCALE = HD_A ** -0.5
N_GATE = 4 * H_B
QKV_COLS = 3 * A_WIDTH + 4 * B_WIDTH
N_STATE = 2 * H_B
MOD_ROWS = 8

LANES = 128
TM = 512
FF_CHUNK = 256
VMEM_LIMIT = 52 * 1024 * 1024

NA_RB = WIN_H // 2
NA_KR = NA_RB + WIN_H - 1
NA_Q = NA_RB * GRID_W
NA_K = NA_KR * GRID_W
NA_BLOCKS = LAT_ROWS // NA_RB
assert LAT_ROWS % NA_RB == 0 and NA_BLOCKS >= 3 and N_CTX % N_LAT == 0 and 1 + N_LAT_B <= MOD_ROWS


def _dot(a, b):
    return jnp.dot(a, b, preferred_element_type=F32)


def _dot_nt(a, b):
    return lax.dot_general(a, b, (((1,), (1,)), ((), ())), preferred_element_type=F32)


def _split2(x):
    hi = x.astype(BF16)
    lo = (x - hi.astype(F32)).astype(BF16)
    return hi, lo


def _split3(x):
    hi = x.astype(BF16)
    r1 = x - hi.astype(F32)
    mid = r1.astype(BF16)
    lo = (r1 - mid.astype(F32)).astype(BF16)
    return hi, mid, lo


def _norm_mod(x, g, shift, scale):
    ms = jnp.mean(x * x, axis=-1, keepdims=True)
    return (x * lax.rsqrt(ms + EPS) * g) * (1.0 + scale) + shift


def _params(sem):
    return pltpu.CompilerParams(dimension_semantics=sem, vmem_limit_bytes=VMEM_LIMIT)


def _resident(shape):
    nd = len(shape)
    return pl.BlockSpec(shape, lambda *_: (0,) * nd, pipeline_mode=pl.Buffered(1))


def _layer(shape, l):
    nd = len(shape)
    return pl.BlockSpec((None,) + tuple(shape), lambda *_: (l,) + (0,) * nd, pipeline_mode=pl.Buffered(1))


def _tok_spec(w):
    return pl.BlockSpec((TM, w), lambda i: (i, 0))


def _ctx_spec(w):
    return pl.BlockSpec((TM, w), lambda i: (jnp.minimum(i, N_CTX // TM - 1), 0))


def _lat_spec(w):
    return pl.BlockSpec((TM, w), lambda i: (jnp.maximum(i - N_CTX // TM, 0), 0))


def _mod_row(mod_ref, k):
    i = pl.program_id(0)
    n_ctx_tiles = N_CTX // TM
    mod_set = jnp.where(i < n_ctx_tiles, 0, 1 + (i - n_ctx_tiles) // (LAT_LEN // TM))
    return mod_ref[pl.ds(mod_set, 1), D_MODEL * k:D_MODEL * (k + 1)]


def _ada_kernel(c_ref, w_ref, b_ref, o_ref):
    c = c_ref[...]
    s = c * jax.nn.sigmoid(c)
    sh, sl = _split2(s)
    wh, wl = _split2(w_ref[0])
    o_ref[0] = _dot(sh, wh) + _dot(sl, wh) + _dot(sh, wl) + b_ref[0]


def _ada(cond, w_ada, b_ada):
    tn = 1024
    n_out = N_MOD * D_MODEL
    return pl.pallas_call(
        _ada_kernel,
        grid=(DEPTH, n_out // tn),
        in_specs=[
            pl.BlockSpec((MOD_ROWS, D_MODEL), lambda l, j: (0, 0)),
            pl.BlockSpec((1, D_MODEL, tn), lambda l, j: (l, 0, j)),
            pl.BlockSpec((1, 1, tn), lambda l, j: (l, 0, j)),
        ],
        out_specs=pl.BlockSpec((1, MOD_ROWS, tn), lambda l, j: (l, 0, j)),
        out_shape=jax.ShapeDtypeStruct((DEPTH, MOD_ROWS, n_out), F32),
        compiler_params=_params(("parallel", "parallel")),
        name="ada",
    )(cond, w_ada, b_ada.reshape(DEPTH, 1, n_out))


def _ffn_kernel(*refs, sub, split_in, split_out):
    n_x = 2 if split_in else 1
    mod_ref, g_ref, wup_ref, wdn_ref = refs[n_x:n_x + 4]
    o_refs = refs[n_x + 4:]
    is_ctx = pl.program_id(0) < N_CTX // TM
    x = jnp.where(is_ctx, refs[0][...], refs[1][...]) if split_in else refs[0][...]
    h = _norm_mod(x, g_ref[sub:sub + 1, :], _mod_row(mod_ref, 3 * sub), _mod_row(mod_ref, 3 * sub + 1)).astype(BF16)
    acc = jnp.zeros(x.shape, F32)
    for j in range(D_FF // FF_CHUNK):
        c0 = j * FF_CHUNK
        a = _dot(h, wup_ref[:, c0:c0 + FF_CHUNK])
        u = _dot(h, wup_ref[:, D_FF + c0:D_FF + c0 + FF_CHUNK])
        g = (a * jax.nn.sigmoid(a) * u).astype(BF16)
        acc = acc + _dot(g, wdn_ref[c0:c0 + FF_CHUNK, :])
    y = x + 0.5 * _mod_row(mod_ref, 3 * sub + 2) * acc
    if split_out:
        @pl.when(is_ctx)
        def _():
            o_refs[0][...] = y

        @pl.when(jnp.logical_not(is_ctx))
        def _():
            o_refs[1][...] = y
    else:
        o_refs[0][...] = y


def _ffn(xs, mod, g_norm, w_up, w_dn, l, sub, split_out=False):
    split_in = isinstance(xs, tuple)
    xs = xs if split_in else (xs,)
    x_specs = [_ctx_spec(D_MODEL), _lat_spec(D_MODEL)] if split_in else [_tok_spec(D_MODEL)]
    if split_out:
        out_specs = [_ctx_spec(D_MODEL), _lat_spec(D_MODEL)]
        out_shape = [jax.ShapeDtypeStruct((N_CTX, D_MODEL), F32), jax.ShapeDtypeStruct((N_LAT, D_MODEL), F32)]
    else:
        out_specs = _tok_spec(D_MODEL)
        out_shape = jax.ShapeDtypeStruct((N_TOK, D_MODEL), F32)
    return pl.pallas_call(
        functools.partial(_ffn_kernel, sub=sub, split_in=split_in, split_out=split_out),
        grid=(N_TOK // TM,),
        in_specs=x_specs + [
            _layer((MOD_ROWS, N_MOD * D_MODEL), l),
            _layer((3, D_MODEL), l),
            _layer((D_MODEL, 2 * D_FF), l),
            _layer((D_FF, D_MODEL), l),
        ],
        out_specs=out_specs,
        out_shape=out_shape,
        compiler_params=_params(("arbitrary",)),
        name="ffn",
    )(*xs, mod, g_norm, w_up, w_dn)


def _mixin_kernel(x_ref, mod_ref, g_ref, wa_ref, wbr_ref, wg_ref, bg_ref, gq_ref, gk_ref, seg_ref, segt_ref,
                  cos_ref, sin_ref,
                  qa_ref, ka_ref, va_ref, kc_ref, vc_ref, qb_ref, kb_ref, vb_ref, ob_ref, gt_ref, br_ref):
    is_ctx = pl.program_id(0) < N_CTX // TM
    x = x_ref[...]
    h = _norm_mod(x, g_ref[1:2, :], _mod_row(mod_ref, 3), _mod_row(mod_ref, 4)).astype(BF16)

    def proj(c0, c1):
        return _dot(h, wa_ref[:, c0:c1])

    def head_norm(p, gain):
        hi, lo = _split2(p * p)
        ssum = _dot(hi, seg_ref[...]) + _dot(lo, seg_ref[...])
        rh, rl = _split2(lax.rsqrt(ssum * (1.0 / HD_A) + EPS))
        return p * (_dot(rh, segt_ref[...]) + _dot(rl, segt_ref[...])) * gain

    cosf = cos_ref[...]
    sinf = sin_ref[...]

    def rope(p):
        parts = []
        for hh in range(H_B):
            ph = p[:, HD_B * hh:HD_B * (hh + 1)]
            parts.append(ph * cosf + pltpu.roll(ph, HD_B // 2, 1) * sinf)
        return jnp.concatenate(parts, axis=1)

    c = 0
    qa_ref[...] = (head_norm(proj(c, c + A_WIDTH), gq_ref[...]) * ASCALE).astype(BF16)
    c += A_WIDTH
    ka = head_norm(proj(c, c + A_WIDTH), gk_ref[...])
    c += A_WIDTH
    va = proj(c, c + A_WIDTH)
    c += A_WIDTH
    ka_ref[...] = ka.astype(BF16)
    va_ref[...] = va.astype(BF16)

    @pl.when(is_ctx)
    def _():
        kc_ref[...] = ka
        vc_ref[...] = va

    qb_ref[...] = rope(proj(c, c + B_WIDTH)).astype(BF16)
    c += B_WIDTH
    kb_ref[...] = (rope(proj(c, c + B_WIDTH)) * KSCALE).astype(BF16)
    c += B_WIDTH
    vb_ref[...] = proj(c, c + B_WIDTH).astype(BF16)
    c += B_WIDTH
    ob_ref[...] = jax.nn.sigmoid(proj(c, c + B_WIDTH)).astype(BF16)
    for j in range(2 * D_MODEL // 512):
        sl = slice(512 * j, 512 * (j + 1))
        br_ref[:, sl] = jax.nn.sigmoid(_dot(h, wbr_ref[:, sl])).astype(BF16)

    pg = _dot(h, wg_ref[...]) + bg_ref[...]
    ls = jnp.minimum(pg, 0.0) - jnp.log1p(jnp.exp(-jnp.abs(pg)))
    lane = lax.broadcasted_iota(jnp.int32, pg.shape, 1)
    is_f = ((lane >= H_B) & (lane < 2 * H_B)) | ((lane >= 3 * H_B) & (lane < 4 * H_B))
    gt_ref[...] = jnp.where(is_f, ls, pg)


def _mixin(x, mod, g_norm, w_a, w_br, w_gt, b_gt, gq, gk, seg, segt, cosf, sinf, l):
    tok = _tok_spec
    out = lambda w, dt: jax.ShapeDtypeStruct((N_TOK, w), dt)
    ctx_out = jax.ShapeDtypeStruct((N_CTX, A_WIDTH), F32)
    return pl.pallas_call(
        _mixin_kernel,
        grid=(N_TOK // TM,),
        in_specs=[
            tok(D_MODEL),
            _layer((MOD_ROWS, N_MOD * D_MODEL), l),
            _layer((3, D_MODEL), l),
            _layer((D_MODEL, QKV_COLS), l),
            _layer((D_MODEL, 2 * D_MODEL), l),
            _layer((D_MODEL, LANES), l),
            _layer((1, LANES), l),
            _layer((1, A_WIDTH), l),
            _layer((1, A_WIDTH), l),
            _resident((A_WIDTH, LANES)),
            _resident((LANES, A_WIDTH)),
            tok(HD_B),
            tok(HD_B),
        ],
        out_specs=[tok(A_WIDTH), tok(A_WIDTH), tok(A_WIDTH), _ctx_spec(A_WIDTH), _ctx_spec(A_WIDTH),
                   tok(B_WIDTH), tok(B_WIDTH), tok(B_WIDTH), tok(B_WIDTH), tok(LANES), tok(2 * D_MODEL)],
        out_shape=[out(A_WIDTH, BF16), out(A_WIDTH, BF16), out(A_WIDTH, BF16), ctx_out, ctx_out,
                   out(B_WIDTH, BF16), out(B_WIDTH, BF16), out(B_WIDTH, BF16),
                   out(B_WIDTH, BF16), out(LANES, F32), out(2 * D_MODEL, BF16)],
        compiler_params=_params(("arbitrary",)),
        name="mixin",
    )(x, mod, g_norm, w_a, w_br, w_gt, b_gt, gq, gk, seg, segt, cosf, sinf)


def _ctx_attn_kernel(q_ref, k_ref, v_ref, o_ref):
    q = q_ref[...]
    k = k_ref[...]
    v = v_ref[...]
    outs = []
    for hh in range(H_A):
        sl = slice(HD_A * hh, HD_A * (hh + 1))
        s = _dot_nt(q[:, sl], k[:, sl])
        p = jnp.exp(s - jnp.max(s, axis=-1, keepdims=True))
        l = jnp.sum(p, axis=-1, keepdims=True)
        outs.append(_dot(p.astype(BF16), v[:, sl]) / l)
    o_ref[...] = jnp.concatenate(outs, axis=1).astype(BF16)


def _ctx_attn(qa, ka, va):
    blk = pl.BlockSpec((CTX_LEN, A_WIDTH), lambda b: (b, 0))
    return pl.pallas_call(
        _ctx_attn_kernel,
        grid=(N_CTX_B,),
        in_specs=[blk, blk, blk],
        out_specs=blk,
        out_shape=jax.ShapeDtypeStruct((N_CTX, A_WIDTH), BF16),
        compiler_params=_params(("parallel",)),
        name="ctx_attn",
    )(qa, ka, va)


def _na_key_row0(j):
    return jnp.clip(NA_RB * j - WIN_H // 2, 0, LAT_ROWS - NA_KR)


def _na_kernel(q_ref, k_ref, v_ref, ck_ref, cv_ref, bias_ref, o_ref):
    j = pl.program_id(0)
    b = pl.program_id(1)
    base = pl.multiple_of(b * LAT_LEN + _na_key_row0(j) * GRID_W, GRID_W)
    kw = k_ref[pl.ds(base, NA_K), :]
    vw = v_ref[pl.ds(base, NA_K), :]
    ck = ck_ref[b].astype(BF16)
    cv = cv_ref[b].astype(BF16)
    q = q_ref[...]
    outs = []
    for hh in range(H_A):
        sl = slice(HD_A * hh, HD_A * (hh + 1))
        qh = q[:, sl]
        sw = _dot_nt(qh, kw[:, sl]) + bias_ref[hh]
        sx = _dot_nt(qh, ck[:, sl])
        m = jnp.maximum(jnp.max(sw, axis=-1, keepdims=True), jnp.max(sx, axis=-1, keepdims=True))
        pw = jnp.exp(sw - m)
        px = jnp.exp(sx - m)
        l = jnp.sum(pw, axis=-1, keepdims=True) + jnp.sum(px, axis=-1, keepdims=True)
        outs.append((_dot(pw.astype(BF16), vw[:, sl]) + _dot(px.astype(BF16), cv[:, sl])) / l)
    o_ref[...] = jnp.concatenate(outs, axis=1).astype(BF16)


def _na_attn(qa, ka, va, ck, cv, bias, l):
    past = ck.shape[2]
    lat_kv = pl.BlockSpec((N_LAT, A_WIDTH), lambda j, b: (N_CTX // N_LAT, 0), pipeline_mode=pl.Buffered(1))
    cache = pl.BlockSpec((N_LAT_B, None, past, A_WIDTH), lambda j, b: (0, l, 0, 0), pipeline_mode=pl.Buffered(1))
    bias_class = lambda j: jnp.where(j == 0, 0, jnp.where(j == NA_BLOCKS - 1, 2, 1))
    return pl.pallas_call(
        _na_kernel,
        grid=(NA_BLOCKS, N_LAT_B),
        in_specs=[
            pl.BlockSpec((NA_Q, A_WIDTH), lambda j, b: (N_CTX // NA_Q + b * NA_BLOCKS + j, 0)),
            lat_kv, lat_kv, cache, cache,
            pl.BlockSpec((None, None, H_A, NA_Q, NA_K), lambda j, b: (l, bias_class(j), 0, 0, 0)),
        ],
        out_specs=pl.BlockSpec((NA_Q, A_WIDTH), lambda j, b: (b * NA_BLOCKS + j, 0)),
        out_shape=jax.ShapeDtypeStruct((N_LAT, A_WIDTH), BF16),
        compiler_params=_params(("arbitrary", "arbitrary")),
        name="na_attn",
    )(qa, ka, va, ck, cv, bias)


def _na_class_geometry():
    out = []
    for j in (0, 1, NA_BLOCKS - 1):
        r = NA_RB * j + np.arange(NA_RB)
        kr = int(np.clip(NA_RB * j - WIN_H // 2, 0, LAT_ROWS - NA_KR)) + np.arange(NA_KR)
        start_r = np.clip(r - WIN_H // 2, 0, LAT_ROWS - WIN_H)
        row_ok = (kr[None, :] >= start_r[:, None]) & (kr[None, :] < start_r[:, None] + WIN_H)
        dr = kr[None, :] - r[:, None] + (WIN_H - 1)
        out.append((dr, row_ok))
    return out


def _bias_kernel(r_ref, o_ref):
    qc = lax.broadcasted_iota(jnp.int32, (GRID_W, GRID_W), 0)
    kc = lax.broadcasted_iota(jnp.int32, (GRID_W, GRID_W), 1)
    c0 = jnp.clip(qc - WIN_W // 2, 0, GRID_W - WIN_W)
    col_ok = (kc >= c0) & (kc < c0 + WIN_W)
    neg = jnp.full((GRID_W, GRID_W), -1e30, F32)
    toep = []
    for dr in range(2 * WIN_H - 1):
        rows = jnp.broadcast_to(r_ref[0, 0, dr:dr + 1, :], (GRID_W, LANES))
        t = pltpu.roll(rows, 0, 1, stride=1, stride_axis=0)[:, :GRID_W]
        toep.append(jnp.where(col_ok, t, neg))
    for cls, (dr_tab, row_ok) in enumerate(_na_class_geometry()):
        for a in range(NA_RB):
            for jj in range(NA_KR):
                o_ref[0, cls, 0, GRID_W * a:GRID_W * (a + 1), GRID_W * jj:GRID_W * (jj + 1)] = (
                    toep[int(dr_tab[a, jj])] if row_ok[a, jj] else neg)


def _bias_tables(rpb):
    n_dr, n_dc = 2 * WIN_H - 1, 2 * WIN_W - 1
    assert rpb.shape == (DEPTH, H_A, n_dr, n_dc)
    rows = jnp.concatenate([rpb[..., WIN_W - 1:], jnp.zeros((DEPTH, H_A, n_dr, LANES - n_dc), F32),
                            rpb[..., :WIN_W - 1]], axis=-1)
    rows = jnp.pad(rows, ((0, 0), (0, 0), (0, 16 - n_dr), (0, 0)))
    return pl.pallas_call(
        _bias_kernel,
        grid=(DEPTH, H_A),
        in_specs=[pl.BlockSpec((1, 1, 16, LANES), lambda l, h: (l, h, 0, 0))],
        out_specs=pl.BlockSpec((1, 3, 1, NA_Q, NA_K), lambda l, h: (l, 0, h, 0, 0)),
        out_shape=jax.ShapeDtypeStruct((DEPTH, 3, H_A, NA_Q, NA_K), F32),
        compiler_params=_params(("arbitrary", "arbitrary")),
        name="na_bias",
    )(rows)


def _mlstm_kernel(*refs, n_chunks, has_init, want_state):
    dirs = (refs[0:4], refs[4:8])
    pos = 8
    if has_init:
        c0_ref, n0_ref, m0_ref = refs[pos:pos + 3]
        pos += 3
    h_refs = refs[pos:pos + 2]
    pos += 2
    if want_state:
        cf_ref, nf_ref, mf_ref = refs[pos:pos + 3]
        pos += 3
    cs_ref, ns_ref, ms_ref = refs[pos:pos + 3]
    c = pl.program_id(1)

    @pl.when(c == 0)
    def _():
        if has_init:
            cs_ref[...] = c0_ref[0]
            ns_ref[...] = n0_ref[0]
            ms_ref[...] = m0_ref[0]
        else:
            cs_ref[...] = jnp.zeros(cs_ref.shape, F32)
            ns_ref[...] = jnp.zeros(ns_ref.shape, F32)
            ms_ref[...] = jnp.zeros(ms_ref.shape, F32)

    row = lax.broadcasted_iota(jnp.int32, (CHUNK, CHUNK), 0)
    col = lax.broadcasted_iota(jnp.int32, (CHUNK, CHUNK), 1)
    for d in range(2):
        q_ref, k_ref, v_ref, g_ref = dirs[d]
        vis = (row <= col) if d == 0 else (row >= col)
        cum_b = jnp.where((col <= row) if d == 0 else (col >= row), 1.0, 0.0).astype(BF16)
        gates = g_ref[...]
        g1, g2, g3 = _split3(gates)
        bsum = _dot(cum_b, g1) + _dot(cum_b, g2) + _dot(cum_b, g3)
        gates_t = gates.T
        bsum_t = bsum.T
        q = q_ref[...]
        k = k_ref[...]
        v32 = v_ref[...].astype(F32)
        for hh in range(H_B):
            i = d * H_B + hh
            ci = 2 * H_B * d + hh
            cf = ci + H_B
            sl = slice(HD_B * hh, HD_B * (hh + 1))
            qh = q[:, sl]
            kh = k[:, sl]
            v_t = v32[:, sl].T
            b_row = bsum_t[cf:cf + 1, :]
            i_row = gates_t[ci:ci + 1, :]
            u_col = gates[:, ci:ci + 1] - bsum[:, cf:cf + 1]
            m_prev = ms_ref[i:i + 1, :]
            dm = jnp.where(vis, u_col + b_row, -jnp.inf)
            inter = b_row + m_prev
            m_t = jnp.maximum(inter, jnp.max(dm, axis=0, keepdims=True))
            a = jnp.exp(inter - m_t)
            p = _dot_nt(kh, qh) * jnp.exp(dm - m_t)
            c_vk = cs_ref[i]
            n_row = ns_ref[i:i + 1, :]
            nh, nl = _split2(jnp.broadcast_to(n_row, (8, HD_B)))
            qn = (_dot_nt(nh, qh) + _dot_nt(nl, qh))[0:1, :]
            den = a * qn + jnp.sum(p, axis=0, keepdims=True)
            num = a * _dot_nt(c_vk.astype(BF16), qh) + _dot(v_t.astype(BF16), p.astype(BF16))
            h_t = num * (1.0 / jnp.maximum(jnp.abs(den), jnp.exp(-m_t)))
            h_refs[d][:, sl] = h_t.T
            b_end = jnp.broadcast_to(b_row[:, CHUNK - 1:CHUNK] if d == 0 else b_row[:, 0:1], (1, CHUNK))
            gl = b_end - b_row + i_row
            m_new = jnp.maximum(b_end + m_prev, jnp.max(gl, axis=-1, keepdims=True))
            a_s = jnp.exp(b_end + m_prev - m_new)
            w_s = jnp.exp(gl - m_new)
            cs_ref[i] = a_s * c_vk + _dot((v_t * w_s).astype(BF16), kh)
            wh, wl = _split2(jnp.broadcast_to(w_s, (8, CHUNK)))
            ns_ref[i:i + 1, :] = a_s * n_row + (_dot(wh, kh) + _dot(wl, kh))[0:1, :]
            ms_ref[i:i + 1, :] = m_new

    if want_state:
        @pl.when(c == n_chunks - 1)
        def _():
            cf_ref[0] = cs_ref[...]
            nf_ref[0] = ns_ref[...]
            mf_ref[0] = ms_ref[...]


def _mlstm(qb, kb, vb, gt, row0, n_batch, seq_len, init=None, init_layer=None, want_state=False):
    n_chunks = seq_len // CHUNK
    blk0 = row0 // CHUNK

    def fwd(w):
        return pl.BlockSpec((CHUNK, w), lambda b, c: (blk0 + b * n_chunks + c, 0))

    def bwd(w):
        return pl.BlockSpec((CHUNK, w), lambda b, c: (blk0 + b * n_chunks + n_chunks - 1 - c, 0))

    in_specs = [fwd(B_WIDTH), fwd(B_WIDTH), fwd(B_WIDTH), fwd(LANES),
                bwd(B_WIDTH), bwd(B_WIDTH), bwd(B_WIDTH), bwd(LANES)]
    args = [qb, kb, vb, gt, qb, kb, vb, gt]
    if init is not None:
        li = init_layer
        in_specs += [pl.BlockSpec((1, None, N_STATE, HD_B, HD_B), lambda b, c: (b, li, 0, 0, 0)),
                     pl.BlockSpec((1, None, N_STATE, LANES), lambda b, c: (b, li, 0, 0)),
                     pl.BlockSpec((1, None, N_STATE, LANES), lambda b, c: (b, li, 0, 0))]
        args += list(init)
    n_rows = n_batch * seq_len
    out_specs = [pl.BlockSpec((CHUNK, B_WIDTH), lambda b, c: (b * n_chunks + c, 0)),
                 pl.BlockSpec((CHUNK, B_WIDTH), lambda b, c: (b * n_chunks + n_chunks - 1 - c, 0))]
    out_shape = [jax.ShapeDtypeStruct((n_rows, B_WIDTH), F32)] * 2
    if want_state:
        out_specs += [pl.BlockSpec((1, N_STATE, HD_B, HD_B), lambda b, c: (b, 0, 0, 0)),
                      pl.BlockSpec((1, N_STATE, LANES), lambda b, c: (b, 0, 0)),
                      pl.BlockSpec((1, N_STATE, LANES), lambda b, c: (b, 0, 0))]
        out_shape += [jax.ShapeDtypeStruct((n_batch, N_STATE, HD_B, HD_B), F32),
                      jax.ShapeDtypeStruct((n_batch, N_STATE, LANES), F32),
                      jax.ShapeDtypeStruct((n_batch, N_STATE, LANES), F32)]
    return pl.pallas_call(
        functools.partial(_mlstm_kernel, n_chunks=n_chunks, has_init=init is not None, want_state=want_state),
        grid=(n_batch, n_chunks),
        in_specs=in_specs,
        out_specs=out_specs,
        out_shape=out_shape,
        scratch_shapes=[pltpu.VMEM((N_STATE, HD_B, HD_B), F32),
                        pltpu.VMEM((N_STATE, LANES), F32),
                        pltpu.VMEM((N_STATE, LANES), F32)],
        compiler_params=_params(("parallel", "arbitrary")),
        name="mlstm",
    )(*args)


def _mixout_kernel(x_ref, mod_ref, attc_ref, attl_ref, hfc_ref, hbc_ref, hfl_ref, hbl_ref, ob_ref, br_ref,
                   ghn_ref, wpa_ref, wpb_ref, wo_ref, o_ref):
    is_ctx = pl.program_id(0) < N_CTX // TM
    att = jnp.where(is_ctx, attc_ref[...], attl_ref[...])
    hsum = jnp.where(is_ctx, hfc_ref[...] + hbc_ref[...], hfl_ref[...] + hbl_ref[...])
    parts = []
    for hh in range(H_B):
        seg = hsum[:, HD_B * hh:HD_B * (hh + 1)]
        parts.append(seg * lax.rsqrt(jnp.mean(seg * seg, axis=-1, keepdims=True) + EPS))
    hn = (jnp.concatenate(parts, axis=1) * ghn_ref[...]) * ob_ref[...].astype(F32)
    y = (br_ref[:, :D_MODEL].astype(F32) * _dot(att, wpa_ref[...])
         + br_ref[:, D_MODEL:].astype(F32) * _dot(hn.astype(BF16), wpb_ref[...]))
    o_ref[...] = x_ref[...] + _mod_row(mod_ref, 5) * _dot(y.astype(BF16), wo_ref[...])


def _mixout(x, mod, att_c, att_l, hf_c, hb_c, hf_l, hb_l, sob, sbr, g_hn, w_pa, w_pb, w_o, l):
    tok, ctx, lat = _tok_spec, _ctx_spec, _lat_spec
    return pl.pallas_call(
        _mixout_kernel,
        grid=(N_TOK // TM,),
        in_specs=[
            tok(D_MODEL),
            _layer((MOD_ROWS, N_MOD * D_MODEL), l),
            ctx(A_WIDTH), lat(A_WIDTH),
            ctx(B_WIDTH), ctx(B_WIDTH), lat(B_WIDTH), lat(B_WIDTH),
            tok(B_WIDTH), tok(2 * D_MODEL),
            _layer((1, B_WIDTH), l),
            _layer((A_WIDTH, D_MODEL), l),
            _layer((B_WIDTH, D_MODEL), l),
            _layer((D_MODEL, D_MODEL), l),
        ],
        out_specs=tok(D_MODEL),
        out_shape=jax.ShapeDtypeStruct((N_TOK, D_MODEL), F32),
        compiler_params=_params(("parallel",)),
        name="mixout",
    )(x, mod, att_c, att_l, hf_c, hb_c, hf_l, hb_l, sob, sbr, g_hn, w_pa, w_pb, w_o)


def _rope_tables():
    half = HD_B // 2
    nf = half // 2
    inv = ROPE_THETA ** (-jnp.arange(nf, dtype=F32) / nf)
    t = jnp.arange(LAT_LEN)
    rowp = (t // GRID_W).astype(F32)
    colp = (t % GRID_W).astype(F32)
    ang = jnp.concatenate([rowp[:, None] * inv, colp[:, None] * inv], axis=-1)
    cos, sin = jnp.cos(ang), jnp.sin(ang)
    cos_l = jnp.tile(jnp.concatenate([cos, cos], axis=-1), (N_LAT_B, 1))
    sin_l = jnp.tile(jnp.concatenate([-sin, sin], axis=-1), (N_LAT_B, 1))
    cosf = jnp.concatenate([jnp.ones((N_CTX, HD_B), F32), cos_l], axis=0)
    sinf = jnp.concatenate([jnp.zeros((N_CTX, HD_B), F32), sin_l], axis=0)
    return cosf, sinf


def kernel(x_prompt, x_sample, cache_k, cache_v, state_C, state_n, state_m, c, c_ctx, w_ada, b_ada, g_norm,
           w_up1, w_dn1, w_in, b_gates, g_qn, g_kn, rpb, g_hn, w_pa, w_pb, w_o, w_up2, w_dn2):
    assert x_prompt.shape == (N_CTX_B, CTX_LEN, D_MODEL) and x_sample.shape == (N_LAT_B, LAT_LEN, D_MODEL)
    past_len = cache_k.shape[2]

    w_a = w_in[..., :QKV_COLS].astype(BF16)
    w_br = w_in[..., QKV_COLS + N_GATE:].astype(BF16)
    w_gt = jnp.pad(w_in[..., QKV_COLS:QKV_COLS + N_GATE], ((0, 0), (0, 0), (0, LANES - N_GATE))).astype(BF16)
    b_gt = jnp.pad(b_gates, ((0, 0), (0, LANES - N_GATE))).reshape(DEPTH, 1, LANES)
    w_up1b, w_dn1b, w_up2b, w_dn2b = (w.astype(BF16) for w in (w_up1, w_dn1, w_up2, w_dn2))
    w_pab, w_pbb, w_ob = (w.astype(BF16) for w in (w_pa, w_pb, w_o))
    gq = jnp.tile(g_qn, (1, H_A)).reshape(DEPTH, 1, A_WIDTH)
    gk = jnp.tile(g_kn, (1, H_A)).reshape(DEPTH, 1, A_WIDTH)
    ghn = g_hn.reshape(DEPTH, 1, B_WIDTH)
    seg_np = (np.arange(A_WIDTH)[:, None] // HD_A == np.arange(LANES)[None, :]).astype(np.float32)
    seg = jnp.asarray(seg_np, BF16)
    segt = jnp.asarray(seg_np.T, BF16)
    cosf, sinf = _rope_tables()
    bias = _bias_tables(rpb)
    ck = cache_k.reshape(N_LAT_B, DEPTH, past_len, A_WIDTH)
    cv = cache_v.reshape(N_LAT_B, DEPTH, past_len, A_WIDTH)
    init = (state_C.reshape(N_LAT_B, DEPTH, N_STATE, HD_B, HD_B),
            state_n.reshape(N_LAT_B, DEPTH, N_STATE, HD_B),
            jnp.broadcast_to(state_m.reshape(N_LAT_B, DEPTH, N_STATE, 1), (N_LAT_B, DEPTH, N_STATE, LANES)))

    cond = jnp.concatenate([c_ctx[None], c, jnp.zeros((MOD_ROWS - 1 - N_LAT_B, D_MODEL), F32)], axis=0)
    mod = _ada(cond, w_ada, b_ada)

    x = (x_prompt.reshape(N_CTX, D_MODEL), x_sample.reshape(N_LAT, D_MODEL))
    ks_out, vs_out, cs_out, ns_out, ms_out = [], [], [], [], []
    for l in range(DEPTH):
        x = _ffn(x, mod, g_norm, w_up1b, w_dn1b, l, 0)
        qa, ka, va, k_ctx, v_ctx, qb, kb, vb, sob, gt, sbr = _mixin(
            x, mod, g_norm, w_a, w_br, w_gt, b_gt, gq, gk, seg, segt, cosf, sinf, l)
        att_c = _ctx_attn(qa, ka, va)
        att_l = _na_attn(qa, ka, va, ck, cv, bias, l)
        hf_c, hb_c, c_fin, n_fin, m_fin = _mlstm(qb, kb, vb, gt, 0, N_CTX_B, CTX_LEN, want_state=True)
        hf_l, hb_l = _mlstm(qb, kb, vb, gt, N_CTX, N_LAT_B, LAT_LEN, init=init, init_layer=l)
        x = _mixout(x, mod, att_c, att_l, hf_c, hb_c, hf_l, hb_l, sob, sbr, ghn, w_pab, w_pbb, w_ob, l)
        x = _ffn(x, mod, g_norm, w_up2b, w_dn2b, l, 2, split_out=(l == DEPTH - 1))
        ks_out.append(k_ctx.reshape(N_CTX_B, CTX_LEN, H_A, HD_A))
        vs_out.append(v_ctx.reshape(N_CTX_B, CTX_LEN, H_A, HD_A))
        cs_out.append(c_fin.reshape(N_CTX_B, 2, H_B, HD_B, HD_B))
        ns_out.append(n_fin.reshape(N_CTX_B, 2, H_B, HD_B))
        ms_out.append(m_fin[:, :, 0].reshape(N_CTX_B, 2, H_B))

    return (x[0].reshape(N_CTX_B, CTX_LEN, D_MODEL),
            x[1].reshape(N_LAT_B, LAT_LEN, D_MODEL),
            jnp.stack(ks_out, axis=1), jnp.stack(vs_out, axis=1),
            jnp.stack(cs_out, axis=1), jnp.stack(ns_out, axis=1), jnp.stack(ms_out, axis=1))
```

```python
import functools

import jax
import jax.numpy as jnp
import numpy as np
from jax import lax
from jax.experimental import pallas as pl
from jax.experimental.pallas import tpu as pltpu

F32 = jnp.float32
BF16 = jnp.bfloat16

D_MODEL = 1024
DEPTH = 4
N_CTX_B, CTX_LEN = 32, 256
N_LAT_B, LAT_LEN = 2, 2048
N_CTX = N_CTX_B * CTX_LEN
N_LAT = N_LAT_B * LAT_LEN
N_TOK = N_CTX + N_LAT
GRID_W, WIN_H, WIN_W = 64, 8, 16
LAT_ROWS = LAT_LEN // GRID_W
H_A, HD_A = 8, 64
H_B, HD_B = 4, 128
A_WIDTH = H_A * HD_A
B_WIDTH = H_B * HD_B
D_FF = 2816
N_MOD = 9
CHUNK = 256
ROPE_THETA = 10000.0
EPS = 1e-6
KSCALE = HD_B ** -0.5
ASCALE = HD_A ** -0.5
N_GATE = 4 * H_B
QKV_COLS = 3 * A_WIDTH + 4 * B_WIDTH
N_STATE = 2 * H_B
STATE_TAIL = 16
MOD_ROWS = 8

LANES = 128
TM = 512
FF_CHUNK = 256
VMEM_LIMIT = 52 * 1024 * 1024

NA_RB = WIN_H // 2
NA_KR = NA_RB + WIN_H - 1
NA_Q = NA_RB * GRID_W
NA_K = NA_KR * GRID_W
NA_BLOCKS = LAT_ROWS // NA_RB
assert LAT_ROWS % NA_RB == 0 and NA_BLOCKS >= 3 and N_CTX % N_LAT == 0 and 1 + N_LAT_B <= MOD_ROWS


def _dot(a, b):
    return jnp.dot(a, b, preferred_element_type=F32)


def _dot_nt(a, b):
    return lax.dot_general(a, b, (((1,), (1,)), ((), ())), preferred_element_type=F32)


def _split2(x):
    hi = x.astype(BF16)
    lo = (x - hi.astype(F32)).astype(BF16)
    return hi, lo


def _split3(x):
    hi = x.astype(BF16)
    r1 = x - hi.astype(F32)
    mid = r1.astype(BF16)
    lo = (r1 - mid.astype(F32)).astype(BF16)
    return hi, mid, lo


def _norm_mod(x, g, shift, scale):
    ms = jnp.mean(x * x, axis=-1, keepdims=True)
    return (x * lax.rsqrt(ms + EPS) * g) * (1.0 + scale) + shift


def _params(sem):
    return pltpu.CompilerParams(dimension_semantics=sem, vmem_limit_bytes=VMEM_LIMIT)


def _resident(shape):
    nd = len(shape)
    return pl.BlockSpec(shape, lambda *_: (0,) * nd, pipeline_mode=pl.Buffered(1))


def _layer(shape, l):
    nd = len(shape)
    return pl.BlockSpec((None,) + tuple(shape), lambda *_: (l,) + (0,) * nd, pipeline_mode=pl.Buffered(1))


def _tok_spec(w):
    return pl.BlockSpec((TM, w), lambda i: (i, 0))


def _ctx_spec(w):
    return pl.BlockSpec((TM, w), lambda i: (jnp.minimum(i, N_CTX // TM - 1), 0))


def _lat_spec(w):
    return pl.BlockSpec((TM, w), lambda i: (jnp.maximum(i - N_CTX // TM, 0), 0))


def _mod_row(mod_ref, k):
    i = pl.program_id(0)
    n_ctx_tiles = N_CTX // TM
    mod_set = jnp.where(i < n_ctx_tiles, 0, 1 + (i - n_ctx_tiles) // (LAT_LEN // TM))
    return mod_ref[pl.ds(mod_set, 1), D_MODEL * k:D_MODEL * (k + 1)]


def _ada_kernel(c_ref, w_ref, b_ref, o_ref):
    c = c_ref[...]
    s = c * jax.nn.sigmoid(c)
    sh, sl = _split2(s)
    wh, wl = _split2(w_ref[0])
    o_ref[0] = _dot(sh, wh) + _dot(sl, wh) + _dot(sh, wl) + b_ref[0]


def _ada(cond, w_ada, b_ada):
    tn = 1024
    n_out = N_MOD * D_MODEL
    return pl.pallas_call(
        _ada_kernel,
        grid=(DEPTH, n_out // tn),
        in_specs=[
            pl.BlockSpec((MOD_ROWS, D_MODEL), lambda l, j: (0, 0)),
            pl.BlockSpec((1, D_MODEL, tn), lambda l, j: (l, 0, j)),
            pl.BlockSpec((1, 1, tn), lambda l, j: (l, 0, j)),
        ],
        out_specs=pl.BlockSpec((1, MOD_ROWS, tn), lambda l, j: (l, 0, j)),
        out_shape=jax.ShapeDtypeStruct((DEPTH, MOD_ROWS, n_out), F32),
        compiler_params=_params(("parallel", "parallel")),
        name="ada",
    )(cond, w_ada, b_ada.reshape(DEPTH, 1, n_out))


def _ffn_kernel(*refs, sub, split_in, split_out):
    n_x = 2 if split_in else 1
    mod_ref, g_ref, wup_ref, wdn_ref = refs[n_x:n_x + 4]
    o_refs = refs[n_x + 4:]
    is_ctx = pl.program_id(0) < N_CTX // TM
    x = jnp.where(is_ctx, refs[0][...], refs[1][...]) if split_in else refs[0][...]
    h = _norm_mod(x, g_ref[sub:sub + 1, :], _mod_row(mod_ref, 3 * sub), _mod_row(mod_ref, 3 * sub + 1)).astype(BF16)
    acc = jnp.zeros(x.shape, F32)
    for j in range(D_FF // FF_CHUNK):
        c0 = j * FF_CHUNK
        a = _dot(h, wup_ref[:, c0:c0 + FF_CHUNK])
        u = _dot(h, wup_ref[:, D_FF + c0:D_FF + c0 + FF_CHUNK])
        g = (a * jax.nn.sigmoid(a) * u).astype(BF16)
        acc = acc + _dot(g, wdn_ref[c0:c0 + FF_CHUNK, :])
    y = x + 0.5 * _mod_row(mod_ref, 3 * sub + 2) * acc
    if split_out:
        @pl.when(is_ctx)
        def _():
            o_refs[0][...] = y

        @pl.when(jnp.logical_not(is_ctx))
        def _():
            o_refs[1][...] = y
    else:
        o_refs[0][...] = y


def _ffn(xs, mod, g_norm, w_up, w_dn, l, sub, split_out=False):
    split_in = isinstance(xs, tuple)
    xs = xs if split_in else (xs,)
    x_specs = [_ctx_spec(D_MODEL), _lat_spec(D_MODEL)] if split_in else [_tok_spec(D_MODEL)]
    if split_out:
        out_specs = [_ctx_spec(D_MODEL), _lat_spec(D_MODEL)]
        out_shape = [jax.ShapeDtypeStruct((N_CTX, D_MODEL), F32), jax.ShapeDtypeStruct((N_LAT, D_MODEL), F32)]
    else:
        out_specs = _tok_spec(D_MODEL)
        out_shape = jax.ShapeDtypeStruct((N_TOK, D_MODEL), F32)
    return pl.pallas_call(
        functools.partial(_ffn_kernel, sub=sub, split_in=split_in, split_out=split_out),
        grid=(N_TOK // TM,),
        in_specs=x_specs + [
            _layer((MOD_ROWS, N_MOD * D_MODEL), l),
            _layer((3, D_MODEL), l),
            _layer((D_MODEL, 2 * D_FF), l),
            _layer((D_FF, D_MODEL), l),
        ],
        out_specs=out_specs,
        out_shape=out_shape,
        compiler_params=_params(("arbitrary",)),
        name="ffn",
    )(*xs, mod, g_norm, w_up, w_dn)


N_MIXIN_IN = 13


def _mixin_kernel(*refs, n_alias):
    (x_ref, mod_ref, g_ref, wa_ref, wbr_ref, wg_ref, bg_ref, gq_ref, gk_ref, seg_ref, segt_ref,
     cos_ref, sin_ref) = refs[:N_MIXIN_IN]
    (qa_ref, ka_ref, va_ref, kc_ref, vc_ref, qb_ref, kb_ref, vb_ref, ob_ref, gt_ref,
     br_ref) = refs[N_MIXIN_IN + n_alias:]
    is_ctx = pl.program_id(0) < N_CTX // TM
    x = x_ref[...]
    h = _norm_mod(x, g_ref[1:2, :], _mod_row(mod_ref, 3), _mod_row(mod_ref, 4)).astype(BF16)

    def proj(c0, c1):
        return _dot(h, wa_ref[:, c0:c1])

    def head_sums(p):
        hi, lo = _split2(p * p)
        return _dot(hi, seg_ref[...]) + _dot(lo, seg_ref[...])

    def head_scale(ssum):
        rh, rl = _split2(lax.rsqrt(ssum * (1.0 / HD_A) + EPS))
        return _dot(rh, segt_ref[...]) + _dot(rl, segt_ref[...])

    cosf = cos_ref[...]
    sinf = sin_ref[...]

    def rope(p):
        parts = []
        for hh in range(H_B):
            ph = p[:, HD_B * hh:HD_B * (hh + 1)]
            parts.append(ph * cosf + pltpu.roll(ph, HD_B // 2, 1) * sinf)
        return jnp.concatenate(parts, axis=1)

    cols = [0]
    for w in (A_WIDTH,) * 3 + (B_WIDTH,) * 4:
        cols.append(cols[-1] + w)
    p_qa = proj(cols[0], cols[1])
    p_ka = proj(cols[1], cols[2])
    s_qa = head_sums(p_qa)
    s_ka = head_sums(p_ka)
    va = proj(cols[2], cols[3])
    p_qb = proj(cols[3], cols[4])
    r_qa = head_scale(s_qa)
    r_ka = head_scale(s_ka)
    p_kb = proj(cols[4], cols[5])
    p_vb = proj(cols[5], cols[6])
    p_ob = proj(cols[6], cols[7])

    qa_ref[...] = (p_qa * r_qa * gq_ref[...] * ASCALE).astype(BF16)
    ka = p_ka * r_ka * gk_ref[...]
    ka_ref[...] = ka.astype(BF16)
    va_ref[...] = va.astype(BF16)

    @pl.when(is_ctx)
    def _():
        kc_ref[...] = ka.reshape(kc_ref.shape)
        vc_ref[...] = va.reshape(vc_ref.shape)

    qb_ref[...] = rope(p_qb).astype(BF16)
    kb_ref[...] = (rope(p_kb) * KSCALE).astype(BF16)
    vb_ref[...] = p_vb.astype(BF16)
    ob_ref[...] = jax.nn.sigmoid(p_ob).astype(BF16)
    for j in range(2 * D_MODEL // 512):
        sl = slice(512 * j, 512 * (j + 1))
        br_ref[:, sl] = jax.nn.sigmoid(_dot(h, wbr_ref[:, sl])).astype(BF16)

    pg = _dot(h, wg_ref[...]) + bg_ref[...]
    ls = jnp.minimum(pg, 0.0) - jnp.log1p(jnp.exp(-jnp.abs(pg)))
    lane = lax.broadcasted_iota(jnp.int32, pg.shape, 1)
    is_f = ((lane >= H_B) & (lane < 2 * H_B)) | ((lane >= 3 * H_B) & (lane < 4 * H_B))
    gt_ref[...] = jnp.where(is_f, ls, pg)


def _mixin(x, mod, g_norm, w_a, w_br, w_gt, b_gt, gq, gk, seg, segt, cosf, sinf, l, caches=None):
    tok = _tok_spec
    out = lambda w, dt: jax.ShapeDtypeStruct((N_TOK, w), dt)
    seqs = TM // CTX_LEN
    ctx_out = jax.ShapeDtypeStruct((N_CTX_B, DEPTH, CTX_LEN, A_WIDTH), F32)
    cache_spec = pl.BlockSpec((seqs, None, CTX_LEN, A_WIDTH),
                              lambda i: (jnp.minimum(i, N_CTX // TM - 1), l, 0, 0))
    caches = () if caches is None else tuple(caches)
    return pl.pallas_call(
        functools.partial(_mixin_kernel, n_alias=len(caches)),
        grid=(N_TOK // TM,),
        input_output_aliases={N_MIXIN_IN + j: 3 + j for j in range(len(caches))},
        in_specs=[
            tok(D_MODEL),
            _layer((MOD_ROWS, N_MOD * D_MODEL), l),
            _layer((3, D_MODEL), l),
            _layer((D_MODEL, QKV_COLS), l),
            _layer((D_MODEL, 2 * D_MODEL), l),
            _layer((D_MODEL, LANES), l),
            _layer((1, LANES), l),
            _layer((1, A_WIDTH), l),
            _layer((1, A_WIDTH), l),
            _resident((A_WIDTH, LANES)),
            _resident((LANES, A_WIDTH)),
            tok(HD_B),
            tok(HD_B),
        ] + [pl.BlockSpec(memory_space=pl.ANY)] * len(caches),
        out_specs=[tok(A_WIDTH), tok(A_WIDTH), tok(A_WIDTH), cache_spec, cache_spec,
                   tok(B_WIDTH), tok(B_WIDTH), tok(B_WIDTH), tok(B_WIDTH), tok(LANES), tok(2 * D_MODEL)],
        out_shape=[out(A_WIDTH, BF16), out(A_WIDTH, BF16), out(A_WIDTH, BF16), ctx_out, ctx_out,
                   out(B_WIDTH, BF16), out(B_WIDTH, BF16), out(B_WIDTH, BF16),
                   out(B_WIDTH, BF16), out(LANES, F32), out(2 * D_MODEL, BF16)],
        compiler_params=_params(("arbitrary",)),
        name="mixin",
    )(x, mod, g_norm, w_a, w_br, w_gt, b_gt, gq, gk, seg, segt, cosf, sinf, *caches)


def _ctx_attn_kernel(q_ref, k_ref, v_ref, o_ref):
    q = q_ref[...]
    k = k_ref[...]
    v = v_ref[...]
    outs = []
    for hh in range(H_A):
        sl = slice(HD_A * hh, HD_A * (hh + 1))
        s = _dot_nt(q[:, sl], k[:, sl])
        p = jnp.exp(s - jnp.max(s, axis=-1, keepdims=True))
        l = jnp.sum(p, axis=-1, keepdims=True)
        outs.append(_dot(p.astype(BF16), v[:, sl]) / l)
    o_ref[...] = jnp.concatenate(outs, axis=1).astype(BF16)


def _ctx_attn(qa, ka, va):
    blk = pl.BlockSpec((CTX_LEN, A_WIDTH), lambda b: (b, 0))
    return pl.pallas_call(
        _ctx_attn_kernel,
        grid=(N_CTX_B,),
        in_specs=[blk, blk, blk],
        out_specs=blk,
        out_shape=jax.ShapeDtypeStruct((N_CTX, A_WIDTH), BF16),
        compiler_params=_params(("parallel",)),
        name="ctx_attn",
    )(qa, ka, va)


def _na_key_row0(j):
    return jnp.clip(NA_RB * j - WIN_H // 2, 0, LAT_ROWS - NA_KR)


def _na_kernel(q_ref, k_ref, v_ref, ck_ref, cv_ref, bias_ref, o_ref):
    j = pl.program_id(0)
    b = pl.program_id(1)
    base = pl.multiple_of(b * LAT_LEN + _na_key_row0(j) * GRID_W, GRID_W)
    kw = k_ref[pl.ds(base, NA_K), :]
    vw = v_ref[pl.ds(base, NA_K), :]
    ck = ck_ref[b].astype(BF16)
    cv = cv_ref[b].astype(BF16)
    q = q_ref[...]
    outs = []
    for hh in range(H_A):
        sl = slice(HD_A * hh, HD_A * (hh + 1))
        qh = q[:, sl]
        sw = _dot_nt(qh, kw[:, sl]) + bias_ref[hh]
        sx = _dot_nt(qh, ck[:, sl])
        m = jnp.maximum(jnp.max(sw, axis=-1, keepdims=True), jnp.max(sx, axis=-1, keepdims=True))
        pw = jnp.exp(sw - m)
        px = jnp.exp(sx - m)
        l = jnp.sum(pw, axis=-1, keepdims=True) + jnp.sum(px, axis=-1, keepdims=True)
        outs.append((_dot(pw.astype(BF16), vw[:, sl]) + _dot(px.astype(BF16), cv[:, sl])) / l)
    o_ref[...] = jnp.concatenate(outs, axis=1).astype(BF16)


def _na_attn(qa, ka, va, ck, cv, bias, l):
    past = ck.shape[2]
    lat_kv = pl.BlockSpec((N_LAT, A_WIDTH), lambda j, b: (N_CTX // N_LAT, 0), pipeline_mode=pl.Buffered(1))
    cache = pl.BlockSpec((N_LAT_B, None, past, A_WIDTH), lambda j, b: (0, l, 0, 0), pipeline_mode=pl.Buffered(1))
    bias_class = lambda j: jnp.where(j == 0, 0, jnp.where(j == NA_BLOCKS - 1, 2, 1))
    return pl.pallas_call(
        _na_kernel,
        grid=(NA_BLOCKS, N_LAT_B),
        in_specs=[
            pl.BlockSpec((NA_Q, A_WIDTH), lambda j, b: (N_CTX // NA_Q + b * NA_BLOCKS + j, 0)),
            lat_kv, lat_kv, cache, cache,
            pl.BlockSpec((None, None, H_A, NA_Q, NA_K), lambda j, b: (l, bias_class(j), 0, 0, 0)),
        ],
        out_specs=pl.BlockSpec((NA_Q, A_WIDTH), lambda j, b: (b * NA_BLOCKS + j, 0)),
        out_shape=jax.ShapeDtypeStruct((N_LAT, A_WIDTH), BF16),
        compiler_params=_params(("arbitrary", "arbitrary")),
        name="na_attn",
    )(qa, ka, va, ck, cv, bias)


def _na_class_geometry():
    out = []
    for j in (0, 1, NA_BLOCKS - 1):
        r = NA_RB * j + np.arange(NA_RB)
        kr = int(np.clip(NA_RB * j - WIN_H // 2, 0, LAT_ROWS - NA_KR)) + np.arange(NA_KR)
        start_r = np.clip(r - WIN_H // 2, 0, LAT_ROWS - WIN_H)
        row_ok = (kr[None, :] >= start_r[:, None]) & (kr[None, :] < start_r[:, None] + WIN_H)
        dr = kr[None, :] - r[:, None] + (WIN_H - 1)
        out.append((dr, row_ok))
    return out


def _bias_kernel(r_ref, o_ref):
    qc = lax.broadcasted_iota(jnp.int32, (GRID_W, GRID_W), 0)
    kc = lax.broadcasted_iota(jnp.int32, (GRID_W, GRID_W), 1)
    c0 = jnp.clip(qc - WIN_W // 2, 0, GRID_W - WIN_W)
    col_ok = (kc >= c0) & (kc < c0 + WIN_W)
    neg = jnp.full((GRID_W, GRID_W), -1e30, F32)
    toep = []
    for dr in range(2 * WIN_H - 1):
        rows = jnp.broadcast_to(r_ref[0, 0, dr:dr + 1, :], (GRID_W, LANES))
        t = pltpu.roll(rows, 0, 1, stride=1, stride_axis=0)[:, :GRID_W]
        toep.append(jnp.where(col_ok, t, neg))
    for cls, (dr_tab, row_ok) in enumerate(_na_class_geometry()):
        for a in range(NA_RB):
            for jj in range(NA_KR):
                o_ref[0, cls, 0, GRID_W * a:GRID_W * (a + 1), GRID_W * jj:GRID_W * (jj + 1)] = (
                    toep[int(dr_tab[a, jj])] if row_ok[a, jj] else neg)


def _bias_tables(rpb):
    n_dr, n_dc = 2 * WIN_H - 1, 2 * WIN_W - 1
    assert rpb.shape == (DEPTH, H_A, n_dr, n_dc)
    rows = jnp.concatenate([rpb[..., WIN_W - 1:], jnp.zeros((DEPTH, H_A, n_dr, LANES - n_dc), F32),
                            rpb[..., :WIN_W - 1]], axis=-1)
    rows = jnp.pad(rows, ((0, 0), (0, 0), (0, 16 - n_dr), (0, 0)))
    return pl.pallas_call(
        _bias_kernel,
        grid=(DEPTH, H_A),
        in_specs=[pl.BlockSpec((1, 1, 16, LANES), lambda l, h: (l, h, 0, 0))],
        out_specs=pl.BlockSpec((1, 3, 1, NA_Q, NA_K), lambda l, h: (l, 0, h, 0, 0)),
        out_shape=jax.ShapeDtypeStruct((DEPTH, 3, H_A, NA_Q, NA_K), F32),
        compiler_params=_params(("arbitrary", "arbitrary")),
        name="na_bias",
    )(rows)


def _mlstm_kernel(*refs, n_chunks, has_init, want_state, n_alias):
    dirs = (refs[0:4], refs[4:8])
    pos = 8
    if has_init:
        c0_ref, n0_ref, m0_ref = refs[pos:pos + 3]
        pos += 3
    pos += n_alias
    h_refs = refs[pos:pos + 2]
    pos += 2
    if want_state:
        cf_ref, nf_ref, mf_ref = refs[pos:pos + 3]
        pos += 3
    cs_ref, ms_ref = refs[pos:pos + 2]
    c = pl.program_id(1)

    def tail_rows(x):
        r = lax.broadcasted_iota(jnp.int32, (STATE_TAIL, x.shape[1]), 0)
        return jnp.where(r == 0, x, jnp.where(r == 1, x - x.astype(BF16).astype(F32), 0.0))

    @pl.when(c == 0)
    def _():
        if has_init:
            cs_ref[:, 0:HD_B, :] = c0_ref[0]
            for i in range(N_STATE):
                cs_ref[i, HD_B:HD_B + STATE_TAIL, :] = tail_rows(n0_ref[0, i:i + 1, :])
            ms_ref[...] = m0_ref[0]
        else:
            cs_ref[...] = jnp.zeros(cs_ref.shape, F32)
            ms_ref[...] = jnp.zeros(ms_ref.shape, F32)

    row = lax.broadcasted_iota(jnp.int32, (CHUNK, CHUNK), 0)
    col = lax.broadcasted_iota(jnp.int32, (CHUNK, CHUNK), 1)
    chains = []
    for d in range(2):
        q_ref, k_ref, v_ref, g_ref = dirs[d]
        vis = (row <= col) if d == 0 else (row >= col)
        cum_b = jnp.where((col <= row) if d == 0 else (col >= row), 1.0, 0.0).astype(BF16)
        g_t = g_ref[...].T[0:N_GATE, :]
        r = _dot_nt(jnp.concatenate(_split3(g_t), axis=0), cum_b)
        b_t = r[0:N_GATE] + r[N_GATE:2 * N_GATE] + r[2 * N_GATE:3 * N_GATE]
        i0 = 2 * H_B * d
        u_t = g_t[i0:i0 + H_B, :] - b_t[i0 + H_B:i0 + 2 * H_B, :]
        u_cols = jnp.concatenate([u_t, jnp.zeros((LANES - H_B, CHUNK), F32)], axis=0).T
        q = q_ref[...]
        k = k_ref[...]
        v32 = v_ref[...].astype(F32)
        for hh in range(H_B):
            sl = slice(HD_B * hh, HD_B * (hh + 1))
            chains.append(dict(
                d=d, i=d * H_B + hh, sl=sl, vis=vis, qh=q[:, sl], kh=k[:, sl], v32=v32,
                u_col=u_cols[:, hh:hh + 1],
                b_row=b_t[i0 + H_B + hh:i0 + H_B + hh + 1, :],
                i_row=g_t[i0 + hh:i0 + hh + 1, :]))

    for ch in chains:
        ch["state"] = cs_ref[ch["i"]]
        ch["r"] = _dot_nt(jnp.concatenate([ch["kh"], ch["state"].astype(BF16)], axis=0), ch["qh"])
    for ch in chains:
        b_row, m_prev = ch["b_row"], ms_ref[ch["i"]:ch["i"] + 1, 0:1]
        dm = jnp.where(ch["vis"], ch["u_col"] + b_row, -jnp.inf)
        inter = b_row + m_prev
        m_t = jnp.maximum(inter, jnp.max(dm, axis=0, keepdims=True))
        ch["a"] = jnp.exp(inter - m_t)
        ch["w"] = jnp.exp(dm - m_t)
        ch["floor"] = jnp.exp(-m_t)
        b_end = jnp.broadcast_to(b_row[:, CHUNK - 1:CHUNK] if ch["d"] == 0 else b_row[:, 0:1], (1, CHUNK))
        gl = b_end - b_row + ch["i_row"]
        m_new = jnp.maximum(b_end + m_prev, jnp.max(gl, axis=-1, keepdims=True))
        ch["a_s"] = jnp.exp(b_end + m_prev - m_new)[:, 0:HD_B]
        ch["w_s"] = jnp.exp(gl - m_new)
        ch["m_new"] = m_new
    for ch in chains:
        r, a = ch["r"], ch["a"]
        p = r[0:CHUNK] * ch["w"]
        qn = r[CHUNK + HD_B:CHUNK + HD_B + 1] + r[CHUNK + HD_B + 1:CHUNK + HD_B + 2]
        den = a * qn + jnp.sum(p, axis=0, keepdims=True)
        ch["v_t"] = ch["v32"][:, ch["sl"]].T
        num = a * r[CHUNK:CHUNK + HD_B] + _dot(ch["v_t"].astype(BF16), p.astype(BF16))
        h_t = num * (1.0 / jnp.maximum(jnp.abs(den), ch["floor"]))
        h_refs[ch["d"]][:, ch["sl"]] = h_t.T
    for ch in chains:
        i, a_s, state = ch["i"], ch["a_s"], ch["state"]
        lhs = jnp.concatenate([ch["v_t"] * ch["w_s"], tail_rows(ch["w_s"])], axis=0).astype(BF16)
        upd = _dot(lhs, ch["kh"])
        n_new = a_s * state[HD_B:HD_B + 1] + upd[HD_B:HD_B + 1] + upd[HD_B + 1:HD_B + 2]
        cs_ref[i, 0:HD_B, :] = a_s * state[0:HD_B] + upd[0:HD_B]
        cs_ref[i, HD_B:HD_B + STATE_TAIL, :] = tail_rows(n_new)
        ms_ref[i:i + 1, :] = ch["m_new"][:, 0:LANES]

    if want_state:
        @pl.when(c == n_chunks - 1)
        def _():
            cf_ref[0] = cs_ref[:, 0:HD_B, :]
            for i in range(N_STATE):
                nf_ref[0, i:i + 1, :] = cs_ref[i, HD_B:HD_B + 1, :]
            mf_ref[0] = ms_ref[...]


def _mlstm(qb, kb, vb, gt, row0, n_batch, seq_len, layer, init=None, want_state=False, c_prev=None):
    n_chunks = seq_len // CHUNK
    blk0 = row0 // CHUNK

    def fwd(w):
        return pl.BlockSpec((CHUNK, w), lambda b, c: (blk0 + b * n_chunks + c, 0))

    def bwd(w):
        return pl.BlockSpec((CHUNK, w), lambda b, c: (blk0 + b * n_chunks + n_chunks - 1 - c, 0))

    in_specs = [fwd(B_WIDTH), fwd(B_WIDTH), fwd(B_WIDTH), fwd(LANES),
                bwd(B_WIDTH), bwd(B_WIDTH), bwd(B_WIDTH), bwd(LANES)]
    args = [qb, kb, vb, gt, qb, kb, vb, gt]
    c_spec = pl.BlockSpec((1, None, N_STATE, HD_B, HD_B), lambda b, c: (b, layer, 0, 0, 0))
    if init is not None:
        in_specs += [c_spec,
                     pl.BlockSpec((1, None, N_STATE, LANES), lambda b, c: (b, layer, 0, 0)),
                     pl.BlockSpec((1, None, N_STATE, LANES), lambda b, c: (b, layer, 0, 0))]
        args += list(init)
    aliases = {}
    if c_prev is not None:
        aliases = {len(args): 2}
        in_specs += [pl.BlockSpec(memory_space=pl.ANY)]
        args += [c_prev]
    n_rows = n_batch * seq_len
    out_specs = [pl.BlockSpec((CHUNK, B_WIDTH), lambda b, c: (b * n_chunks + c, 0)),
                 pl.BlockSpec((CHUNK, B_WIDTH), lambda b, c: (b * n_chunks + n_chunks - 1 - c, 0))]
    out_shape = [jax.ShapeDtypeStruct((n_rows, B_WIDTH), F32)] * 2
    if want_state:
        out_specs += [c_spec,
                      pl.BlockSpec((1, N_STATE, LANES), lambda b, c: (b, 0, 0)),
                      pl.BlockSpec((1, N_STATE, LANES), lambda b, c: (b, 0, 0))]
        out_shape += [jax.ShapeDtypeStruct((n_batch, DEPTH, N_STATE, HD_B, HD_B), F32),
                      jax.ShapeDtypeStruct((n_batch, N_STATE, LANES), F32),
                      jax.ShapeDtypeStruct((n_batch, N_STATE, LANES), F32)]
    return pl.pallas_call(
        functools.partial(_mlstm_kernel, n_chunks=n_chunks, has_init=init is not None, want_state=want_state,
                          n_alias=len(aliases)),
        grid=(n_batch, n_chunks),
        input_output_aliases=aliases,
        in_specs=in_specs,
        out_specs=out_specs,
        out_shape=out_shape,
        scratch_shapes=[pltpu.VMEM((N_STATE, HD_B + STATE_TAIL, HD_B), F32),
                        pltpu.VMEM((N_STATE, LANES), F32)],
        compiler_params=_params(("parallel", "arbitrary")),
        name="mlstm",
    )(*args)


def _mixout_kernel(x_ref, mod_ref, attc_ref, attl_ref, hfc_ref, hbc_ref, hfl_ref, hbl_ref, ob_ref, br_ref,
                   ghn_ref, wpa_ref, wpb_ref, wo_ref, o_ref):
    is_ctx = pl.program_id(0) < N_CTX // TM
    att = jnp.where(is_ctx, attc_ref[...], attl_ref[...])
    hsum = jnp.where(is_ctx, hfc_ref[...] + hbc_ref[...], hfl_ref[...] + hbl_ref[...])
    parts = []
    for hh in range(H_B):
        seg = hsum[:, HD_B * hh:HD_B * (hh + 1)]
        parts.append(seg * lax.rsqrt(jnp.mean(seg * seg, axis=-1, keepdims=True) + EPS))
    hn = (jnp.concatenate(parts, axis=1) * ghn_ref[...]) * ob_ref[...].astype(F32)
    y = (br_ref[:, :D_MODEL].astype(F32) * _dot(att, wpa_ref[...])
         + br_ref[:, D_MODEL:].astype(F32) * _dot(hn.astype(BF16), wpb_ref[...]))
    o_ref[...] = x_ref[...] + _mod_row(mod_ref, 5) * _dot(y.astype(BF16), wo_ref[...])


def _mixout(x, mod, att_c, att_l, hf_c, hb_c, hf_l, hb_l, sob, sbr, g_hn, w_pa, w_pb, w_o, l):
    tok, ctx, lat = _tok_spec, _ctx_spec, _lat_spec
    return pl.pallas_call(
        _mixout_kernel,
        grid=(N_TOK // TM,),
        in_specs=[
            tok(D_MODEL),
            _layer((MOD_ROWS, N_MOD * D_MODEL), l),
            ctx(A_WIDTH), lat(A_WIDTH),
            ctx(B_WIDTH), ctx(B_WIDTH), lat(B_WIDTH), lat(B_WIDTH),
            tok(B_WIDTH), tok(2 * D_MODEL),
            _layer((1, B_WIDTH), l),
            _layer((A_WIDTH, D_MODEL), l),
            _layer((B_WIDTH, D_MODEL), l),
            _layer((D_MODEL, D_MODEL), l),
        ],
        out_specs=tok(D_MODEL),
        out_shape=jax.ShapeDtypeStruct((N_TOK, D_MODEL), F32),
        compiler_params=_params(("parallel",)),
        name="mixout",
    )(x, mod, att_c, att_l, hf_c, hb_c, hf_l, hb_l, sob, sbr, g_hn, w_pa, w_pb, w_o)


def _rope_tables():
    half = HD_B // 2
    nf = half // 2
    inv = ROPE_THETA ** (-jnp.arange(nf, dtype=F32) / nf)
    t = jnp.arange(LAT_LEN)
    rowp = (t // GRID_W).astype(F32)
    colp = (t % GRID_W).astype(F32)
    ang = jnp.concatenate([rowp[:, None] * inv, colp[:, None] * inv], axis=-1)
    cos, sin = jnp.cos(ang), jnp.sin(ang)
    cos_l = jnp.tile(jnp.concatenate([cos, cos], axis=-1), (N_LAT_B, 1))
    sin_l = jnp.tile(jnp.concatenate([-sin, sin], axis=-1), (N_LAT_B, 1))
    cosf = jnp.concatenate([jnp.ones((N_CTX, HD_B), F32), cos_l], axis=0)
    sinf = jnp.concatenate([jnp.zeros((N_CTX, HD_B), F32), sin_l], axis=0)
    return cosf, sinf


def kernel(x_prompt, x_sample, cache_k, cache_v, state_C, state_n, state_m, c, c_ctx, w_ada, b_ada, g_norm,
           w_up1, w_dn1, w_in, b_gates, g_qn, g_kn, rpb, g_hn, w_pa, w_pb, w_o, w_up2, w_dn2):
    assert x_prompt.shape == (N_CTX_B, CTX_LEN, D_MODEL) and x_sample.shape == (N_LAT_B, LAT_LEN, D_MODEL)
    past_len = cache_k.shape[2]

    w_a = w_in[..., :QKV_COLS].astype(BF16)
    w_br = w_in[..., QKV_COLS + N_GATE:].astype(BF16)
    w_gt = jnp.pad(w_in[..., QKV_COLS:QKV_COLS + N_GATE], ((0, 0), (0, 0), (0, LANES - N_GATE))).astype(BF16)
    b_gt = jnp.pad(b_gates, ((0, 0), (0, LANES - N_GATE))).reshape(DEPTH, 1, LANES)
    w_up1b, w_dn1b, w_up2b, w_dn2b = (w.astype(BF16) for w in (w_up1, w_dn1, w_up2, w_dn2))
    w_pab, w_pbb, w_ob = (w.astype(BF16) for w in (w_pa, w_pb, w_o))
    gq = jnp.tile(g_qn, (1, H_A)).reshape(DEPTH, 1, A_WIDTH)
    gk = jnp.tile(g_kn, (1, H_A)).reshape(DEPTH, 1, A_WIDTH)
    ghn = g_hn.reshape(DEPTH, 1, B_WIDTH)
    seg_np = (np.arange(A_WIDTH)[:, None] // HD_A == np.arange(LANES)[None, :]).astype(np.float32)
    seg = jnp.asarray(seg_np, BF16)
    segt = jnp.asarray(seg_np.T, BF16)
    cosf, sinf = _rope_tables()
    bias = _bias_tables(rpb)
    ck = cache_k.reshape(N_LAT_B, DEPTH, past_len, A_WIDTH)
    cv = cache_v.reshape(N_LAT_B, DEPTH, past_len, A_WIDTH)
    init = (state_C.reshape(N_LAT_B, DEPTH, N_STATE, HD_B, HD_B),
            state_n.reshape(N_LAT_B, DEPTH, N_STATE, HD_B),
            jnp.broadcast_to(state_m.reshape(N_LAT_B, DEPTH, N_STATE, 1), (N_LAT_B, DEPTH, N_STATE, LANES)))

    cond = jnp.concatenate([c_ctx[None], c, jnp.zeros((MOD_ROWS - 1 - N_LAT_B, D_MODEL), F32)], axis=0)
    mod = _ada(cond, w_ada, b_ada)

    x = (x_prompt.reshape(N_CTX, D_MODEL), x_sample.reshape(N_LAT, D_MODEL))
    caches, c_new, ns_out, ms_out = None, None, [], []
    for l in range(DEPTH):
        x = _ffn(x, mod, g_norm, w_up1b, w_dn1b, l, 0)
        qa, ka, va, k_new, v_new, qb, kb, vb, sob, gt, sbr = _mixin(
            x, mod, g_norm, w_a, w_br, w_gt, b_gt, gq, gk, seg, segt, cosf, sinf, l, caches)
        caches = (k_new, v_new)
        att_c = _ctx_attn(qa, ka, va)
        att_l = _na_attn(qa, ka, va, ck, cv, bias, l)
        hf_c, hb_c, c_new, n_fin, m_fin = _mlstm(qb, kb, vb, gt, 0, N_CTX_B, CTX_LEN, l,
                                                 want_state=True, c_prev=c_new)
        hf_l, hb_l = _mlstm(qb, kb, vb, gt, N_CTX, N_LAT_B, LAT_LEN, l, init=init)
        x = _mixout(x, mod, att_c, att_l, hf_c, hb_c, hf_l, hb_l, sob, sbr, ghn, w_pab, w_pbb, w_ob, l)
        x = _ffn(x, mod, g_norm, w_up2b, w_dn2b, l, 2, split_out=(l == DEPTH - 1))
        ns_out.append(n_fin.reshape(N_CTX_B, 2, H_B, HD_B))
        ms_out.append(m_fin[:, :, 0].reshape(N_CTX_B, 2, H_B))

    return (x[0].reshape(N_CTX_B, CTX_LEN, D_MODEL),
            x[1].reshape(N_LAT_B, LAT_LEN, D_MODEL),
            k_new.reshape(N_CTX_B, DEPTH, CTX_LEN, H_A, HD_A),
            v_new.reshape(N_CTX_B, DEPTH, CTX_LEN, H_A, HD_A),
            c_new.reshape(N_CTX_B, DEPTH, 2, H_B, HD_B, HD_B),
            jnp.stack(ns_out, axis=1), jnp.stack(ms_out, axis=1))
```

```python
import functools

import jax
import jax.numpy as jnp
import numpy as np
from jax import lax
from jax.experimental import pallas as pl
from jax.experimental.pallas import tpu as pltpu

F32 = jnp.float32
BF16 = jnp.bfloat16

D_MODEL = 1024
DEPTH = 4
N_CTX_B, CTX_LEN = 32, 256
N_LAT_B, LAT_LEN = 2, 2048
N_CTX = N_CTX_B * CTX_LEN
N_LAT = N_LAT_B * LAT_LEN
N_TOK = N_CTX + N_LAT
GRID_W, WIN_H, WIN_W = 64, 8, 16
LAT_ROWS = LAT_LEN // GRID_W
H_A, HD_A = 8, 64
H_B, HD_B = 4, 128
A_WIDTH = H_A * HD_A
B_WIDTH = H_B * HD_B
D_FF = 2816
N_MOD = 9
CHUNK = 256
ROPE_THETA = 10000.0
EPS = 1e-6
KSCALE = HD_B ** -0.5
ASCALE = HD_A ** -0.5
N_GATE = 4 * H_B
QKV_COLS = 3 * A_WIDTH + 4 * B_WIDTH
N_STATE = 2 * H_B
STATE_TAIL = 16
MOD_ROWS = 8

LANES = 128
TM = 512
TM_FFN = 1024
FF_CHUNK = 256
VMEM_LIMIT = 52 * 1024 * 1024

NA_RB = WIN_H // 2
NA_KR = NA_RB + WIN_H - 1
NA_Q = NA_RB * GRID_W
NA_K = NA_KR * GRID_W
NA_BLOCKS = LAT_ROWS // NA_RB
assert LAT_ROWS % NA_RB == 0 and NA_BLOCKS >= 3 and N_CTX % N_LAT == 0 and 1 + N_LAT_B <= MOD_ROWS


def _dot(a, b):
    return jnp.dot(a, b, preferred_element_type=F32)


def _dot_nt(a, b):
    return lax.dot_general(a, b, (((1,), (1,)), ((), ())), preferred_element_type=F32)


def _split2(x):
    hi = x.astype(BF16)
    lo = (x - hi.astype(F32)).astype(BF16)
    return hi, lo


def _split3(x):
    hi = x.astype(BF16)
    r1 = x - hi.astype(F32)
    mid = r1.astype(BF16)
    lo = (r1 - mid.astype(F32)).astype(BF16)
    return hi, mid, lo


def _norm_mod(x, g, shift, scale):
    ms = jnp.mean(x * x, axis=-1, keepdims=True)
    return (x * lax.rsqrt(ms + EPS) * g) * (1.0 + scale) + shift


def _params(sem):
    return pltpu.CompilerParams(dimension_semantics=sem, vmem_limit_bytes=VMEM_LIMIT)


def _resident(shape):
    nd = len(shape)
    return pl.BlockSpec(shape, lambda *_: (0,) * nd, pipeline_mode=pl.Buffered(1))


def _layer(shape, l):
    nd = len(shape)
    return pl.BlockSpec((None,) + tuple(shape), lambda *_: (l,) + (0,) * nd, pipeline_mode=pl.Buffered(1))


def _tok_spec(w, tm=TM):
    return pl.BlockSpec((tm, w), lambda i: (i, 0))


def _ctx_spec(w, tm=TM):
    return pl.BlockSpec((tm, w), lambda i: (jnp.minimum(i, N_CTX // tm - 1), 0))


def _lat_spec(w, tm=TM):
    return pl.BlockSpec((tm, w), lambda i: (jnp.maximum(i - N_CTX // tm, 0), 0))


def _mod_row(mod_ref, k, tm=TM):
    i = pl.program_id(0)
    n_ctx_tiles = N_CTX // tm
    mod_set = jnp.where(i < n_ctx_tiles, 0, 1 + (i - n_ctx_tiles) // (LAT_LEN // tm))
    return mod_ref[pl.ds(mod_set, 1), D_MODEL * k:D_MODEL * (k + 1)]


def _ada_kernel(c_ref, w_ref, b_ref, o_ref):
    c = c_ref[...]
    s = c * jax.nn.sigmoid(c)
    sh, sl = _split2(s)
    wh, wl = _split2(w_ref[0])
    o_ref[0] = _dot(sh, wh) + _dot(sl, wh) + _dot(sh, wl) + b_ref[0]


def _ada(cond, w_ada, b_ada):
    tn = 1024
    n_out = N_MOD * D_MODEL
    return pl.pallas_call(
        _ada_kernel,
        grid=(DEPTH, n_out // tn),
        in_specs=[
            pl.BlockSpec((MOD_ROWS, D_MODEL), lambda l, j: (0, 0)),
            pl.BlockSpec((1, D_MODEL, tn), lambda l, j: (l, 0, j)),
            pl.BlockSpec((1, 1, tn), lambda l, j: (l, 0, j)),
        ],
        out_specs=pl.BlockSpec((1, MOD_ROWS, tn), lambda l, j: (l, 0, j)),
        out_shape=jax.ShapeDtypeStruct((DEPTH, MOD_ROWS, n_out), F32),
        compiler_params=_params(("parallel", "parallel")),
        name="ada",
    )(cond, w_ada, b_ada.reshape(DEPTH, 1, n_out))


def _swiglu_sub(x, g, shift, scale, gate, wup_ref, wdn_ref):
    h = _norm_mod(x, g, shift, scale).astype(BF16)
    acc = jnp.zeros(x.shape, F32)
    for j in range(D_FF // FF_CHUNK):
        c0 = j * FF_CHUNK
        a = _dot(h, wup_ref[:, c0:c0 + FF_CHUNK])
        u = _dot(h, wup_ref[:, D_FF + c0:D_FF + c0 + FF_CHUNK])
        g_act = (a * jax.nn.sigmoid(a) * u).astype(BF16)
        acc = acc + _dot(g_act, wdn_ref[c0:c0 + FF_CHUNK, :])
    return x + 0.5 * gate * acc


def _write_stream(o_refs, y, is_ctx, split_out):
    if split_out:
        @pl.when(is_ctx)
        def _():
            o_refs[0][...] = y

        @pl.when(jnp.logical_not(is_ctx))
        def _():
            o_refs[1][...] = y
    else:
        o_refs[0][...] = y


def _ffn_kernel(*refs, sub, split_in, split_out):
    n_x = 2 if split_in else 1
    mod_ref, g_ref, wup_ref, wdn_ref = refs[n_x:n_x + 4]
    o_refs = refs[n_x + 4:]
    is_ctx = pl.program_id(0) < N_CTX // TM_FFN
    mod_row = functools.partial(_mod_row, mod_ref, tm=TM_FFN)
    x = jnp.where(is_ctx, refs[0][...], refs[1][...]) if split_in else refs[0][...]
    y = _swiglu_sub(x, g_ref[sub:sub + 1, :], mod_row(3 * sub), mod_row(3 * sub + 1), mod_row(3 * sub + 2),
                    wup_ref, wdn_ref)
    _write_stream(o_refs, y, is_ctx, split_out)


def _ffn(xs, mod, g_norm, w_up, w_dn, l, sub, split_out=False):
    split_in = isinstance(xs, tuple)
    xs = xs if split_in else (xs,)
    tm = TM_FFN
    x_specs = [_ctx_spec(D_MODEL, tm), _lat_spec(D_MODEL, tm)] if split_in else [_tok_spec(D_MODEL, tm)]
    if split_out:
        out_specs = [_ctx_spec(D_MODEL, tm), _lat_spec(D_MODEL, tm)]
        out_shape = [jax.ShapeDtypeStruct((N_CTX, D_MODEL), F32), jax.ShapeDtypeStruct((N_LAT, D_MODEL), F32)]
    else:
        out_specs = _tok_spec(D_MODEL, tm)
        out_shape = jax.ShapeDtypeStruct((N_TOK, D_MODEL), F32)
    return pl.pallas_call(
        functools.partial(_ffn_kernel, sub=sub, split_in=split_in, split_out=split_out),
        grid=(N_TOK // tm,),
        in_specs=x_specs + [
            _layer((MOD_ROWS, N_MOD * D_MODEL), l),
            _layer((3, D_MODEL), l),
            _layer((D_MODEL, 2 * D_FF), l),
            _layer((D_FF, D_MODEL), l),
        ],
        out_specs=out_specs,
        out_shape=out_shape,
        compiler_params=_params(("arbitrary",)),
        name="ffn",
    )(*xs, mod, g_norm, w_up, w_dn)


N_MIXIN_IN = 13


def _mixin_kernel(*refs, n_alias):
    (x_ref, mod_ref, g_ref, wa_ref, wbr_ref, wg_ref, bg_ref, gq_ref, gk_ref, seg_ref, segt_ref,
     cos_ref, sin_ref) = refs[:N_MIXIN_IN]
    (qa_ref, ka_ref, va_ref, kc_ref, vc_ref, qb_ref, kb_ref, vb_ref, ob_ref, gt_ref,
     br_ref) = refs[N_MIXIN_IN + n_alias:]
    is_ctx = pl.program_id(0) < N_CTX // TM
    x = x_ref[...]
    h = _norm_mod(x, g_ref[1:2, :], _mod_row(mod_ref, 3), _mod_row(mod_ref, 4)).astype(BF16)

    def proj(c0, c1):
        return _dot(h, wa_ref[:, c0:c1])

    def head_sums(p):
        hi, lo = _split2(p * p)
        return _dot(hi, seg_ref[...]) + _dot(lo, seg_ref[...])

    def head_scale(ssum):
        rh, rl = _split2(lax.rsqrt(ssum * (1.0 / HD_A) + EPS))
        return _dot(rh, segt_ref[...]) + _dot(rl, segt_ref[...])

    cosf = cos_ref[...]
    sinf = sin_ref[...]

    def rope(p):
        parts = []
        for hh in range(H_B):
            ph = p[:, HD_B * hh:HD_B * (hh + 1)]
            parts.append(ph * cosf + pltpu.roll(ph, HD_B // 2, 1) * sinf)
        return jnp.concatenate(parts, axis=1)

    cols = [0]
    for w in (A_WIDTH,) * 3 + (B_WIDTH,) * 4:
        cols.append(cols[-1] + w)
    p_qa = proj(cols[0], cols[1])
    p_ka = proj(cols[1], cols[2])
    s_qa = head_sums(p_qa)
    s_ka = head_sums(p_ka)
    va = proj(cols[2], cols[3])
    p_qb = proj(cols[3], cols[4])
    r_qa = head_scale(s_qa)
    r_ka = head_scale(s_ka)
    p_kb = proj(cols[4], cols[5])
    p_vb = proj(cols[5], cols[6])
    p_ob = proj(cols[6], cols[7])

    qa_ref[...] = (p_qa * r_qa * gq_ref[...] * ASCALE).astype(BF16)
    ka = p_ka * r_ka * gk_ref[...]
    ka_ref[...] = ka.astype(BF16)
    va_ref[...] = va.astype(BF16)

    @pl.when(is_ctx)
    def _():
        kc_ref[...] = ka.reshape(kc_ref.shape)
        vc_ref[...] = va.reshape(vc_ref.shape)

    qb_ref[...] = rope(p_qb).astype(BF16)
    kb_ref[...] = (rope(p_kb) * KSCALE).astype(BF16)
    vb_ref[...] = p_vb.astype(BF16)
    ob_ref[...] = jax.nn.sigmoid(p_ob).astype(BF16)
    for j in range(2 * D_MODEL // 512):
        sl = slice(512 * j, 512 * (j + 1))
        br_ref[:, sl] = jax.nn.sigmoid(_dot(h, wbr_ref[:, sl])).astype(BF16)

    pg = _dot(h, wg_ref[...]) + bg_ref[...]
    ls = jnp.minimum(pg, 0.0) - jnp.log1p(jnp.exp(-jnp.abs(pg)))
    lane = lax.broadcasted_iota(jnp.int32, pg.shape, 1)
    is_f = ((lane >= H_B) & (lane < 2 * H_B)) | ((lane >= 3 * H_B) & (lane < 4 * H_B))
    gt_ref[...] = jnp.where(is_f, ls, pg)


def _mixin(x, mod, g_norm, w_a, w_br, w_gt, b_gt, gq, gk, seg, segt, cosf, sinf, l, caches=None):
    tok = _tok_spec
    out = lambda w, dt: jax.ShapeDtypeStruct((N_TOK, w), dt)
    seqs = TM // CTX_LEN
    ctx_out = jax.ShapeDtypeStruct((N_CTX_B, DEPTH, CTX_LEN, A_WIDTH), F32)
    cache_spec = pl.BlockSpec((seqs, None, CTX_LEN, A_WIDTH),
                              lambda i: (jnp.minimum(i, N_CTX // TM - 1), l, 0, 0))
    caches = () if caches is None else tuple(caches)
    return pl.pallas_call(
        functools.partial(_mixin_kernel, n_alias=len(caches)),
        grid=(N_TOK // TM,),
        input_output_aliases={N_MIXIN_IN + j: 3 + j for j in range(len(caches))},
        in_specs=[
            tok(D_MODEL),
            _layer((MOD_ROWS, N_MOD * D_MODEL), l),
            _layer((3, D_MODEL), l),
            _layer((D_MODEL, QKV_COLS), l),
            _layer((D_MODEL, 2 * D_MODEL), l),
            _layer((D_MODEL, LANES), l),
            _layer((1, LANES), l),
            _layer((1, A_WIDTH), l),
            _layer((1, A_WIDTH), l),
            _resident((A_WIDTH, LANES)),
            _resident((LANES, A_WIDTH)),
            tok(HD_B),
            tok(HD_B),
        ] + [pl.BlockSpec(memory_space=pl.ANY)] * len(caches),
        out_specs=[tok(A_WIDTH), tok(A_WIDTH), tok(A_WIDTH), cache_spec, cache_spec,
                   tok(B_WIDTH), tok(B_WIDTH), tok(B_WIDTH), tok(B_WIDTH), tok(LANES), tok(2 * D_MODEL)],
        out_shape=[out(A_WIDTH, BF16), out(A_WIDTH, BF16), out(A_WIDTH, BF16), ctx_out, ctx_out,
                   out(B_WIDTH, BF16), out(B_WIDTH, BF16), out(B_WIDTH, BF16),
                   out(B_WIDTH, BF16), out(LANES, F32), out(2 * D_MODEL, BF16)],
        compiler_params=_params(("arbitrary",)),
        name="mixin",
    )(x, mod, g_norm, w_a, w_br, w_gt, b_gt, gq, gk, seg, segt, cosf, sinf, *caches)


CTX_SEQS = 2


def _ctx_attn_kernel(q_ref, k_ref, v_ref, o_ref):
    q = q_ref[...]
    k = k_ref[...]
    v = v_ref[...]
    pairs = [(slice(CTX_LEN * sq, CTX_LEN * (sq + 1)), slice(HD_A * hh, HD_A * (hh + 1)))
             for sq in range(CTX_SEQS) for hh in range(H_A)]
    scores = [_dot_nt(q[rows, sl], k[rows, sl]) for rows, sl in pairs]
    probs = [jnp.exp(s - jnp.max(s, axis=-1, keepdims=True)) for s in scores]
    outs = [_dot(p.astype(BF16), v[rows, sl]) / jnp.sum(p, axis=-1, keepdims=True)
            for p, (rows, sl) in zip(probs, pairs)]
    for sq in range(CTX_SEQS):
        o_ref[CTX_LEN * sq:CTX_LEN * (sq + 1), :] = jnp.concatenate(
            outs[H_A * sq:H_A * (sq + 1)], axis=1).astype(BF16)


def _ctx_attn(qa, ka, va):
    blk = pl.BlockSpec((CTX_SEQS * CTX_LEN, A_WIDTH), lambda b: (b, 0))
    return pl.pallas_call(
        _ctx_attn_kernel,
        grid=(N_CTX_B // CTX_SEQS,),
        in_specs=[blk, blk, blk],
        out_specs=blk,
        out_shape=jax.ShapeDtypeStruct((N_CTX, A_WIDTH), BF16),
        compiler_params=_params(("parallel",)),
        name="ctx_attn",
    )(qa, ka, va)


def _na_key_row0(j):
    return jnp.clip(NA_RB * j - WIN_H // 2, 0, LAT_ROWS - NA_KR)


def _na_kernel(q_ref, k_ref, v_ref, ck_ref, cv_ref, bias_ref, o_ref):
    j = pl.program_id(0)
    b = pl.program_id(1)
    base = pl.multiple_of(b * LAT_LEN + _na_key_row0(j) * GRID_W, GRID_W)
    kw = k_ref[pl.ds(base, NA_K), :]
    vw = v_ref[pl.ds(base, NA_K), :]
    ck = ck_ref[b].astype(BF16)
    cv = cv_ref[b].astype(BF16)
    q = q_ref[...]
    heads = [slice(HD_A * hh, HD_A * (hh + 1)) for hh in range(H_A)]
    s_win = [_dot_nt(q[:, sl], kw[:, sl]) for sl in heads]
    s_ctx = [_dot_nt(q[:, sl], ck[:, sl]) for sl in heads]
    probs = []
    for hh in range(H_A):
        sw = s_win[hh] + bias_ref[hh]
        sx = s_ctx[hh]
        m = jnp.maximum(jnp.max(sw, axis=-1, keepdims=True), jnp.max(sx, axis=-1, keepdims=True))
        pw = jnp.exp(sw - m)
        px = jnp.exp(sx - m)
        l = jnp.sum(pw, axis=-1, keepdims=True) + jnp.sum(px, axis=-1, keepdims=True)
        probs.append((pw.astype(BF16), px.astype(BF16), l))
    outs = [(_dot(pw, vw[:, sl]) + _dot(px, cv[:, sl])) / l for (pw, px, l), sl in zip(probs, heads)]
    o_ref[...] = jnp.concatenate(outs, axis=1).astype(BF16)


def _na_attn(qa, ka, va, ck, cv, bias, l):
    past = ck.shape[2]
    lat_kv = pl.BlockSpec((N_LAT, A_WIDTH), lambda j, b: (N_CTX // N_LAT, 0), pipeline_mode=pl.Buffered(1))
    cache = pl.BlockSpec((N_LAT_B, None, past, A_WIDTH), lambda j, b: (0, l, 0, 0), pipeline_mode=pl.Buffered(1))
    bias_class = lambda j: jnp.where(j == 0, 0, jnp.where(j == NA_BLOCKS - 1, 2, 1))
    return pl.pallas_call(
        _na_kernel,
        grid=(NA_BLOCKS, N_LAT_B),
        in_specs=[
            pl.BlockSpec((NA_Q, A_WIDTH), lambda j, b: (N_CTX // NA_Q + b * NA_BLOCKS + j, 0)),
            lat_kv, lat_kv, cache, cache,
            pl.BlockSpec((None, None, H_A, NA_Q, NA_K), lambda j, b: (l, bias_class(j), 0, 0, 0)),
        ],
        out_specs=pl.BlockSpec((NA_Q, A_WIDTH), lambda j, b: (b * NA_BLOCKS + j, 0)),
        out_shape=jax.ShapeDtypeStruct((N_LAT, A_WIDTH), BF16),
        compiler_params=_params(("arbitrary", "arbitrary")),
        name="na_attn",
    )(qa, ka, va, ck, cv, bias)


def _na_class_geometry():
    out = []
    for j in (0, 1, NA_BLOCKS - 1):
        r = NA_RB * j + np.arange(NA_RB)
        kr = int(np.clip(NA_RB * j - WIN_H // 2, 0, LAT_ROWS - NA_KR)) + np.arange(NA_KR)
        start_r = np.clip(r - WIN_H // 2, 0, LAT_ROWS - WIN_H)
        row_ok = (kr[None, :] >= start_r[:, None]) & (kr[None, :] < start_r[:, None] + WIN_H)
        dr = kr[None, :] - r[:, None] + (WIN_H - 1)
        out.append((dr, row_ok))
    return out


def _bias_kernel(r_ref, o_ref):
    qc = lax.broadcasted_iota(jnp.int32, (GRID_W, GRID_W), 0)
    kc = lax.broadcasted_iota(jnp.int32, (GRID_W, GRID_W), 1)
    c0 = jnp.clip(qc - WIN_W // 2, 0, GRID_W - WIN_W)
    col_ok = (kc >= c0) & (kc < c0 + WIN_W)
    neg = jnp.full((GRID_W, GRID_W), -1e30, F32)
    toep = []
    for dr in range(2 * WIN_H - 1):
        rows = jnp.broadcast_to(r_ref[0, 0, dr:dr + 1, :], (GRID_W, LANES))
        t = pltpu.roll(rows, 0, 1, stride=1, stride_axis=0)[:, :GRID_W]
        toep.append(jnp.where(col_ok, t, neg))
    for cls, (dr_tab, row_ok) in enumerate(_na_class_geometry()):
        for a in range(NA_RB):
            for jj in range(NA_KR):
                o_ref[0, cls, 0, GRID_W * a:GRID_W * (a + 1), GRID_W * jj:GRID_W * (jj + 1)] = (
                    toep[int(dr_tab[a, jj])] if row_ok[a, jj] else neg)


def _bias_tables(rpb):
    n_dr, n_dc = 2 * WIN_H - 1, 2 * WIN_W - 1
    assert rpb.shape == (DEPTH, H_A, n_dr, n_dc)
    rows = jnp.concatenate([rpb[..., WIN_W - 1:], jnp.zeros((DEPTH, H_A, n_dr, LANES - n_dc), F32),
                            rpb[..., :WIN_W - 1]], axis=-1)
    rows = jnp.pad(rows, ((0, 0), (0, 0), (0, 16 - n_dr), (0, 0)))
    return pl.pallas_call(
        _bias_kernel,
        grid=(DEPTH, H_A),
        in_specs=[pl.BlockSpec((1, 1, 16, LANES), lambda l, h: (l, h, 0, 0))],
        out_specs=pl.BlockSpec((1, 3, 1, NA_Q, NA_K), lambda l, h: (l, 0, h, 0, 0)),
        out_shape=jax.ShapeDtypeStruct((DEPTH, 3, H_A, NA_Q, NA_K), F32),
        compiler_params=_params(("arbitrary", "arbitrary")),
        name="na_bias",
    )(rows)


def _mlstm_kernel(*refs, n_chunks, has_init, want_state, n_alias):
    dirs = (refs[0:4], refs[4:8])
    pos = 8
    if has_init:
        c0_ref, n0_ref, m0_ref = refs[pos:pos + 3]
        pos += 3
    pos += n_alias
    n_h = 1 if n_chunks == 1 else 2
    h_refs = refs[pos:pos + n_h]
    pos += n_h
    if want_state:
        cf_ref, nf_ref, mf_ref = refs[pos:pos + 3]
        pos += 3
    cs_ref, ms_ref = refs[pos:pos + 2]
    c = pl.program_id(1)

    def tail_rows(x):
        r = lax.broadcasted_iota(jnp.int32, (STATE_TAIL, x.shape[1]), 0)
        return jnp.where(r == 0, x, jnp.where(r == 1, x - x.astype(BF16).astype(F32), 0.0))

    @pl.when(c == 0)
    def _():
        if has_init:
            cs_ref[:, 0:HD_B, :] = c0_ref[0]
            for i in range(N_STATE):
                cs_ref[i, HD_B:HD_B + STATE_TAIL, :] = tail_rows(n0_ref[0, i:i + 1, :])
            ms_ref[...] = m0_ref[0]
        else:
            cs_ref[...] = jnp.zeros(cs_ref.shape, F32)
            ms_ref[...] = jnp.zeros(ms_ref.shape, F32)

    row = lax.broadcasted_iota(jnp.int32, (CHUNK, CHUNK), 0)
    col = lax.broadcasted_iota(jnp.int32, (CHUNK, CHUNK), 1)
    chains = []
    for d in range(2):
        q_ref, k_ref, v_ref, g_ref = dirs[d]
        vis = (row <= col) if d == 0 else (row >= col)
        cum_b = jnp.where((col <= row) if d == 0 else (col >= row), 1.0, 0.0).astype(BF16)
        g_t = g_ref[...].T[0:N_GATE, :]
        r = _dot_nt(jnp.concatenate(_split3(g_t), axis=0), cum_b)
        b_t = r[0:N_GATE] + r[N_GATE:2 * N_GATE] + r[2 * N_GATE:3 * N_GATE]
        i0 = 2 * H_B * d
        u_t = g_t[i0:i0 + H_B, :] - b_t[i0 + H_B:i0 + 2 * H_B, :]
        u_cols = jnp.concatenate([u_t, jnp.zeros((LANES - H_B, CHUNK), F32)], axis=0).T
        q = q_ref[...]
        k = k_ref[...]
        v32 = v_ref[...].astype(F32)
        for hh in range(H_B):
            sl = slice(HD_B * hh, HD_B * (hh + 1))
            chains.append(dict(
                d=d, i=d * H_B + hh, sl=sl, vis=vis, qh=q[:, sl], kh=k[:, sl], v32=v32,
                u_col=u_cols[:, hh:hh + 1],
                b_row=b_t[i0 + H_B + hh:i0 + H_B + hh + 1, :],
                i_row=g_t[i0 + hh:i0 + hh + 1, :]))

    for ch in chains:
        ch["state"] = cs_ref[ch["i"]]
        ch["r"] = _dot_nt(jnp.concatenate([ch["kh"], ch["state"].astype(BF16)], axis=0), ch["qh"])
    for ch in chains:
        b_row, m_prev = ch["b_row"], ms_ref[ch["i"]:ch["i"] + 1, 0:1]
        dm = jnp.where(ch["vis"], ch["u_col"] + b_row, -jnp.inf)
        inter = b_row + m_prev
        m_t = jnp.maximum(inter, jnp.max(dm, axis=0, keepdims=True))
        ch["a"] = jnp.exp(inter - m_t)
        ch["w"] = jnp.exp(dm - m_t)
        ch["floor"] = jnp.exp(-m_t)
        b_end = jnp.broadcast_to(b_row[:, CHUNK - 1:CHUNK] if ch["d"] == 0 else b_row[:, 0:1], (1, CHUNK))
        gl = b_end - b_row + ch["i_row"]
        m_new = jnp.maximum(b_end + m_prev, jnp.max(gl, axis=-1, keepdims=True))
        ch["a_s"] = jnp.exp(b_end + m_prev - m_new)[:, 0:HD_B]
        ch["w_s"] = jnp.exp(gl - m_new)
        ch["m_new"] = m_new
    for ch in chains:
        r, a = ch["r"], ch["a"]
        p = r[0:CHUNK] * ch["w"]
        qn = r[CHUNK + HD_B:CHUNK + HD_B + 1] + r[CHUNK + HD_B + 1:CHUNK + HD_B + 2]
        den = a * qn + jnp.sum(p, axis=0, keepdims=True)
        ch["v_t"] = ch["v32"][:, ch["sl"]].T
        num = a * r[CHUNK:CHUNK + HD_B] + _dot(ch["v_t"].astype(BF16), p.astype(BF16))
        h_t = num * (1.0 / jnp.maximum(jnp.abs(den), ch["floor"]))
        if n_h == 2:
            h_refs[ch["d"]][:, ch["sl"]] = h_t.T
        else:
            ch["h_t"] = h_t
    if n_h == 1:
        for fwd_ch, bwd_ch in zip(chains[:H_B], chains[H_B:]):
            h_refs[0][:, fwd_ch["sl"]] = (fwd_ch["h_t"] + bwd_ch["h_t"]).T
    for ch in chains:
        i, a_s, state = ch["i"], ch["a_s"], ch["state"]
        lhs = jnp.concatenate([ch["v_t"] * ch["w_s"], tail_rows(ch["w_s"])], axis=0).astype(BF16)
        upd = _dot(lhs, ch["kh"])
        n_new = a_s * state[HD_B:HD_B + 1] + upd[HD_B:HD_B + 1] + upd[HD_B + 1:HD_B + 2]
        cs_ref[i, 0:HD_B, :] = a_s * state[0:HD_B] + upd[0:HD_B]
        cs_ref[i, HD_B:HD_B + STATE_TAIL, :] = tail_rows(n_new)
        ms_ref[i:i + 1, :] = ch["m_new"][:, 0:LANES]

    if want_state:
        @pl.when(c == n_chunks - 1)
        def _():
            cf_ref[0] = cs_ref[:, 0:HD_B, :]
            for i in range(N_STATE):
                nf_ref[0, i:i + 1, :] = cs_ref[i, HD_B:HD_B + 1, :]
            mf_ref[0] = ms_ref[...]


def _mlstm(qb, kb, vb, gt, row0, n_batch, seq_len, layer, init=None, want_state=False, c_prev=None):
    n_chunks = seq_len // CHUNK
    blk0 = row0 // CHUNK

    def fwd(w):
        return pl.BlockSpec((CHUNK, w), lambda b, c: (blk0 + b * n_chunks + c, 0))

    def bwd(w):
        return pl.BlockSpec((CHUNK, w), lambda b, c: (blk0 + b * n_chunks + n_chunks - 1 - c, 0))

    in_specs = [fwd(B_WIDTH), fwd(B_WIDTH), fwd(B_WIDTH), fwd(LANES),
                bwd(B_WIDTH), bwd(B_WIDTH), bwd(B_WIDTH), bwd(LANES)]
    args = [qb, kb, vb, gt, qb, kb, vb, gt]
    c_spec = pl.BlockSpec((1, None, N_STATE, HD_B, HD_B), lambda b, c: (b, layer, 0, 0, 0))
    if init is not None:
        in_specs += [c_spec,
                     pl.BlockSpec((1, None, N_STATE, LANES), lambda b, c: (b, layer, 0, 0)),
                     pl.BlockSpec((1, None, N_STATE, LANES), lambda b, c: (b, layer, 0, 0))]
        args += list(init)
    n_rows = n_batch * seq_len
    out_specs = [pl.BlockSpec((CHUNK, B_WIDTH), lambda b, c: (b * n_chunks + c, 0)),
                 pl.BlockSpec((CHUNK, B_WIDTH), lambda b, c: (b * n_chunks + n_chunks - 1 - c, 0))]
    if n_chunks == 1:
        out_specs = out_specs[:1]
    out_shape = [jax.ShapeDtypeStruct((n_rows, B_WIDTH), F32)] * len(out_specs)
    aliases = {}
    if c_prev is not None:
        aliases = {len(args): len(out_specs)}
        in_specs += [pl.BlockSpec(memory_space=pl.ANY)]
        args += [c_prev]
    if want_state:
        out_specs += [c_spec,
                      pl.BlockSpec((1, N_STATE, LANES), lambda b, c: (b, 0, 0)),
                      pl.BlockSpec((1, N_STATE, LANES), lambda b, c: (b, 0, 0))]
        out_shape += [jax.ShapeDtypeStruct((n_batch, DEPTH, N_STATE, HD_B, HD_B), F32),
                      jax.ShapeDtypeStruct((n_batch, N_STATE, LANES), F32),
                      jax.ShapeDtypeStruct((n_batch, N_STATE, LANES), F32)]
    return pl.pallas_call(
        functools.partial(_mlstm_kernel, n_chunks=n_chunks, has_init=init is not None, want_state=want_state,
                          n_alias=len(aliases)),
        grid=(n_batch, n_chunks),
        input_output_aliases=aliases,
        in_specs=in_specs,
        out_specs=out_specs,
        out_shape=out_shape,
        scratch_shapes=[pltpu.VMEM((N_STATE, HD_B + STATE_TAIL, HD_B), F32),
                        pltpu.VMEM((N_STATE, LANES), F32)],
        compiler_params=_params(("parallel", "arbitrary")),
        name="mlstm",
    )(*args)


def _mixout_ffn_kernel(*refs, split_out, n_hc, n_hl):
    x_ref, mod_ref, attc_ref, attl_ref = refs[:4]
    hc_refs = refs[4:4 + n_hc]
    hl_refs = refs[4 + n_hc:4 + n_hc + n_hl]
    (ob_ref, br_ref, g_ref, ghn_ref, wpa_ref, wpb_ref, wo_ref, wup_ref,
     wdn_ref) = refs[4 + n_hc + n_hl:13 + n_hc + n_hl]
    o_refs = refs[13 + n_hc + n_hl:]
    is_ctx = pl.program_id(0) < N_CTX // TM
    att = jnp.where(is_ctx, attc_ref[...], attl_ref[...])
    hsum = jnp.where(is_ctx, sum(r[...] for r in hc_refs), sum(r[...] for r in hl_refs))
    parts = []
    for hh in range(H_B):
        seg = hsum[:, HD_B * hh:HD_B * (hh + 1)]
        parts.append(seg * lax.rsqrt(jnp.mean(seg * seg, axis=-1, keepdims=True) + EPS))
    hn = (jnp.concatenate(parts, axis=1) * ghn_ref[...]) * ob_ref[...].astype(F32)
    y = (br_ref[:, :D_MODEL].astype(F32) * _dot(att, wpa_ref[...])
         + br_ref[:, D_MODEL:].astype(F32) * _dot(hn.astype(BF16), wpb_ref[...]))
    x = x_ref[...] + _mod_row(mod_ref, 5) * _dot(y.astype(BF16), wo_ref[...])
    out = _swiglu_sub(x, g_ref[2:3, :], _mod_row(mod_ref, 6), _mod_row(mod_ref, 7), _mod_row(mod_ref, 8),
                      wup_ref, wdn_ref)
    _write_stream(o_refs, out, is_ctx, split_out)


def _mixout_ffn(x, mod, att_c, att_l, h_c, h_l, sob, sbr, g_norm, g_hn, w_pa, w_pb, w_o,
                w_up, w_dn, l, split_out=False):
    tok, ctx, lat = _tok_spec, _ctx_spec, _lat_spec
    if split_out:
        out_specs = [ctx(D_MODEL), lat(D_MODEL)]
        out_shape = [jax.ShapeDtypeStruct((N_CTX, D_MODEL), F32), jax.ShapeDtypeStruct((N_LAT, D_MODEL), F32)]
    else:
        out_specs = tok(D_MODEL)
        out_shape = jax.ShapeDtypeStruct((N_TOK, D_MODEL), F32)
    return pl.pallas_call(
        functools.partial(_mixout_ffn_kernel, split_out=split_out, n_hc=len(h_c), n_hl=len(h_l)),
        grid=(N_TOK // TM,),
        in_specs=[
            tok(D_MODEL),
            _layer((MOD_ROWS, N_MOD * D_MODEL), l),
            ctx(A_WIDTH), lat(A_WIDTH),
        ] + [ctx(B_WIDTH)] * len(h_c) + [lat(B_WIDTH)] * len(h_l) + [
            tok(B_WIDTH), tok(2 * D_MODEL),
            _layer((3, D_MODEL), l),
            _layer((1, B_WIDTH), l),
            _layer((A_WIDTH, D_MODEL), l),
            _layer((B_WIDTH, D_MODEL), l),
            _layer((D_MODEL, D_MODEL), l),
            _layer((D_MODEL, 2 * D_FF), l),
            _layer((D_FF, D_MODEL), l),
        ],
        out_specs=out_specs,
        out_shape=out_shape,
        compiler_params=_params(("arbitrary",)),
        name="mixout_ffn",
    )(x, mod, att_c, att_l, *h_c, *h_l, sob, sbr, g_norm, g_hn, w_pa, w_pb, w_o, w_up, w_dn)


def _rope_tables():
    half = HD_B // 2
    nf = half // 2
    inv = ROPE_THETA ** (-jnp.arange(nf, dtype=F32) / nf)
    t = jnp.arange(LAT_LEN)
    rowp = (t // GRID_W).astype(F32)
    colp = (t % GRID_W).astype(F32)
    ang = jnp.concatenate([rowp[:, None] * inv, colp[:, None] * inv], axis=-1)
    cos, sin = jnp.cos(ang), jnp.sin(ang)
    cos_l = jnp.tile(jnp.concatenate([cos, cos], axis=-1), (N_LAT_B, 1))
    sin_l = jnp.tile(jnp.concatenate([-sin, sin], axis=-1), (N_LAT_B, 1))
    cosf = jnp.concatenate([jnp.ones((N_CTX, HD_B), F32), cos_l], axis=0)
    sinf = jnp.concatenate([jnp.zeros((N_CTX, HD_B), F32), sin_l], axis=0)
    return cosf, sinf


def kernel(x_prompt, x_sample, cache_k, cache_v, state_C, state_n, state_m, c, c_ctx, w_ada, b_ada, g_norm,
           w_up1, w_dn1, w_in, b_gates, g_qn, g_kn, rpb, g_hn, w_pa, w_pb, w_o, w_up2, w_dn2):
    assert x_prompt.shape == (N_CTX_B, CTX_LEN, D_MODEL) and x_sample.shape == (N_LAT_B, LAT_LEN, D_MODEL)
    past_len = cache_k.shape[2]

    w_a = w_in[..., :QKV_COLS].astype(BF16)
    w_br = w_in[..., QKV_COLS + N_GATE:].astype(BF16)
    w_gt = jnp.pad(w_in[..., QKV_COLS:QKV_COLS + N_GATE], ((0, 0), (0, 0), (0, LANES - N_GATE))).astype(BF16)
    b_gt = jnp.pad(b_gates, ((0, 0), (0, LANES - N_GATE))).reshape(DEPTH, 1, LANES)
    w_up1b, w_dn1b, w_up2b, w_dn2b = (w.astype(BF16) for w in (w_up1, w_dn1, w_up2, w_dn2))
    w_pab, w_pbb, w_ob = (w.astype(BF16) for w in (w_pa, w_pb, w_o))
    gq = jnp.tile(g_qn, (1, H_A)).reshape(DEPTH, 1, A_WIDTH)
    gk = jnp.tile(g_kn, (1, H_A)).reshape(DEPTH, 1, A_WIDTH)
    ghn = g_hn.reshape(DEPTH, 1, B_WIDTH)
    seg_np = (np.arange(A_WIDTH)[:, None] // HD_A == np.arange(LANES)[None, :]).astype(np.float32)
    seg = jnp.asarray(seg_np, BF16)
    segt = jnp.asarray(seg_np.T, BF16)
    cosf, sinf = _rope_tables()
    bias = _bias_tables(rpb)
    ck = cache_k.reshape(N_LAT_B, DEPTH, past_len, A_WIDTH)
    cv = cache_v.reshape(N_LAT_B, DEPTH, past_len, A_WIDTH)
    init = (state_C.reshape(N_LAT_B, DEPTH, N_STATE, HD_B, HD_B),
            state_n.reshape(N_LAT_B, DEPTH, N_STATE, HD_B),
            jnp.broadcast_to(state_m.reshape(N_LAT_B, DEPTH, N_STATE, 1), (N_LAT_B, DEPTH, N_STATE, LANES)))

    cond = jnp.concatenate([c_ctx[None], c, jnp.zeros((MOD_ROWS - 1 - N_LAT_B, D_MODEL), F32)], axis=0)
    mod = _ada(cond, w_ada, b_ada)

    x = (x_prompt.reshape(N_CTX, D_MODEL), x_sample.reshape(N_LAT, D_MODEL))
    caches, c_new, ns_out, ms_out = None, None, [], []
    for l in range(DEPTH):
        x = _ffn(x, mod, g_norm, w_up1b, w_dn1b, l, 0)
        qa, ka, va, k_new, v_new, qb, kb, vb, sob, gt, sbr = _mixin(
            x, mod, g_norm, w_a, w_br, w_gt, b_gt, gq, gk, seg, segt, cosf, sinf, l, caches)
        caches = (k_new, v_new)
        att_c = _ctx_attn(qa, ka, va)
        att_l = _na_attn(qa, ka, va, ck, cv, bias, l)
        *h_c, c_new, n_fin, m_fin = _mlstm(qb, kb, vb, gt, 0, N_CTX_B, CTX_LEN, l, want_state=True, c_prev=c_new)
        h_l = _mlstm(qb, kb, vb, gt, N_CTX, N_LAT_B, LAT_LEN, l, init=init)
        x = _mixout_ffn(x, mod, att_c, att_l, h_c, h_l, sob, sbr, g_norm, ghn, w_pab, w_pbb, w_ob,
                        w_up2b, w_dn2b, l, split_out=(l == DEPTH - 1))
        ns_out.append(n_fin.reshape(N_CTX_B, 2, H_B, HD_B))
        ms_out.append(m_fin[:, :, 0].reshape(N_CTX_B, 2, H_B))

    return (x[0].reshape(N_CTX_B, CTX_LEN, D_MODEL),
            x[1].reshape(N_LAT_B, LAT_LEN, D_MODEL),
            k_new.reshape(N_CTX_B, DEPTH, CTX_LEN, H_A, HD_A),
            v_new.reshape(N_CTX_B, DEPTH, CTX_LEN, H_A, HD_A),
            c_new.reshape(N_CTX_B, DEPTH, 2, H_B, HD_B, HD_B),
            jnp.stack(ns_out, axis=1), jnp.stack(ms_out, axis=1))
```

```python
import functools

import jax
import jax.numpy as jnp
import numpy as np
from jax import lax
from jax.experimental import pallas as pl
from jax.experimental.pallas import tpu as pltpu

F32 = jnp.float32
BF16 = jnp.bfloat16

D_MODEL = 1024
DEPTH = 4
N_CTX_B, CTX_LEN = 32, 256
N_LAT_B, LAT_LEN = 2, 2048
N_CTX = N_CTX_B * CTX_LEN
N_LAT = N_LAT_B * LAT_LEN
N_TOK = N_CTX + N_LAT
GRID_W, WIN_H, WIN_W = 64, 8, 16
LAT_ROWS = LAT_LEN // GRID_W
H_A, HD_A = 8, 64
H_B, HD_B = 4, 128
A_WIDTH = H_A * HD_A
B_WIDTH = H_B * HD_B
D_FF = 2816
N_MOD = 9
CHUNK = 256
ROPE_THETA = 10000.0
EPS = 1e-6
KSCALE = HD_B ** -0.5
ASCALE = HD_A ** -0.5
N_GATE = 4 * H_B
QKV_COLS = 3 * A_WIDTH + 4 * B_WIDTH
N_STATE = 2 * H_B
STATE_TAIL = 16
MOD_ROWS = 8

LANES = 128
TM = 512
TM_FFN = 1024
FF_CHUNK = 256
VMEM_LIMIT = 52 * 1024 * 1024

NA_RB = WIN_H // 2
NA_KR = NA_RB + WIN_H - 1
NA_Q = NA_RB * GRID_W
NA_K = NA_KR * GRID_W
NA_BLOCKS = LAT_ROWS // NA_RB
assert LAT_ROWS % NA_RB == 0 and NA_BLOCKS >= 3 and N_CTX % N_LAT == 0 and 1 + N_LAT_B <= MOD_ROWS


def _dot(a, b):
    return jnp.dot(a, b, preferred_element_type=F32)


def _dot_nt(a, b):
    return lax.dot_general(a, b, (((1,), (1,)), ((), ())), preferred_element_type=F32)


def _split2(x):
    hi = x.astype(BF16)
    lo = (x - hi.astype(F32)).astype(BF16)
    return hi, lo


def _split3(x):
    hi = x.astype(BF16)
    r1 = x - hi.astype(F32)
    mid = r1.astype(BF16)
    lo = (r1 - mid.astype(F32)).astype(BF16)
    return hi, mid, lo


def _norm_mod(x, g, shift, scale):
    ms = jnp.mean(x * x, axis=-1, keepdims=True)
    return (x * lax.rsqrt(ms + EPS) * g) * (1.0 + scale) + shift


def _params(sem):
    return pltpu.CompilerParams(dimension_semantics=sem, vmem_limit_bytes=VMEM_LIMIT)


def _resident(shape):
    nd = len(shape)
    return pl.BlockSpec(shape, lambda *_: (0,) * nd, pipeline_mode=pl.Buffered(1))


def _layer(shape, l):
    nd = len(shape)
    return pl.BlockSpec((None,) + tuple(shape), lambda *_: (l,) + (0,) * nd, pipeline_mode=pl.Buffered(1))


def _tok_spec(w, tm=TM):
    return pl.BlockSpec((tm, w), lambda i: (i, 0))


def _ctx_spec(w, tm=TM):
    return pl.BlockSpec((tm, w), lambda i: (jnp.minimum(i, N_CTX // tm - 1), 0))


def _lat_spec(w, tm=TM):
    return pl.BlockSpec((tm, w), lambda i: (jnp.maximum(i - N_CTX // tm, 0), 0))


def _mod_row(mod_ref, k, tm=TM):
    i = pl.program_id(0)
    n_ctx_tiles = N_CTX // tm
    mod_set = jnp.where(i < n_ctx_tiles, 0, 1 + (i - n_ctx_tiles) // (LAT_LEN // tm))
    return mod_ref[pl.ds(mod_set, 1), D_MODEL * k:D_MODEL * (k + 1)]


def _ada_kernel(c_ref, w_ref, b_ref, o_ref):
    c = c_ref[...]
    s = c * jax.nn.sigmoid(c)
    sh, sl = _split2(s)
    wh, wl = _split2(w_ref[0])
    o_ref[0] = _dot(sh, wh) + _dot(sl, wh) + _dot(sh, wl) + b_ref[0]


def _ada(cond, w_ada, b_ada):
    tn = 1024
    n_out = N_MOD * D_MODEL
    return pl.pallas_call(
        _ada_kernel,
        grid=(DEPTH, n_out // tn),
        in_specs=[
            pl.BlockSpec((MOD_ROWS, D_MODEL), lambda l, j: (0, 0)),
            pl.BlockSpec((1, D_MODEL, tn), lambda l, j: (l, 0, j)),
            pl.BlockSpec((1, 1, tn), lambda l, j: (l, 0, j)),
        ],
        out_specs=pl.BlockSpec((1, MOD_ROWS, tn), lambda l, j: (l, 0, j)),
        out_shape=jax.ShapeDtypeStruct((DEPTH, MOD_ROWS, n_out), F32),
        compiler_params=_params(("parallel", "parallel")),
        name="ada",
    )(cond, w_ada, b_ada.reshape(DEPTH, 1, n_out))


def _swiglu_sub(x, g, shift, scale, gate, wup_ref, wdn_ref):
    h = _norm_mod(x, g, shift, scale).astype(BF16)
    acc = jnp.zeros(x.shape, F32)
    for j in range(D_FF // FF_CHUNK):
        c0 = j * FF_CHUNK
        a = _dot(h, wup_ref[:, c0:c0 + FF_CHUNK])
        u = _dot(h, wup_ref[:, D_FF + c0:D_FF + c0 + FF_CHUNK])
        g_act = (a * jax.nn.sigmoid(a) * u).astype(BF16)
        acc = acc + _dot(g_act, wdn_ref[c0:c0 + FF_CHUNK, :])
    return x + 0.5 * gate * acc


def _write_stream(o_refs, y, is_ctx, split_out):
    if split_out:
        @pl.when(is_ctx)
        def _():
            o_refs[0][...] = y

        @pl.when(jnp.logical_not(is_ctx))
        def _():
            o_refs[1][...] = y
    else:
        o_refs[0][...] = y


def _ffn_kernel(*refs, sub, split_in, split_out):
    n_x = 2 if split_in else 1
    mod_ref, g_ref, wup_ref, wdn_ref = refs[n_x:n_x + 4]
    o_refs = refs[n_x + 4:]
    is_ctx = pl.program_id(0) < N_CTX // TM_FFN
    mod_row = functools.partial(_mod_row, mod_ref, tm=TM_FFN)
    x = jnp.where(is_ctx, refs[0][...], refs[1][...]) if split_in else refs[0][...]
    y = _swiglu_sub(x, g_ref[sub:sub + 1, :], mod_row(3 * sub), mod_row(3 * sub + 1), mod_row(3 * sub + 2),
                    wup_ref, wdn_ref)
    _write_stream(o_refs, y, is_ctx, split_out)


def _ffn(xs, mod, g_norm, w_up, w_dn, l, sub, split_out=False):
    split_in = isinstance(xs, tuple)
    xs = xs if split_in else (xs,)
    tm = TM_FFN
    x_specs = [_ctx_spec(D_MODEL, tm), _lat_spec(D_MODEL, tm)] if split_in else [_tok_spec(D_MODEL, tm)]
    if split_out:
        out_specs = [_ctx_spec(D_MODEL, tm), _lat_spec(D_MODEL, tm)]
        out_shape = [jax.ShapeDtypeStruct((N_CTX, D_MODEL), F32), jax.ShapeDtypeStruct((N_LAT, D_MODEL), F32)]
    else:
        out_specs = _tok_spec(D_MODEL, tm)
        out_shape = jax.ShapeDtypeStruct((N_TOK, D_MODEL), F32)
    return pl.pallas_call(
        functools.partial(_ffn_kernel, sub=sub, split_in=split_in, split_out=split_out),
        grid=(N_TOK // tm,),
        in_specs=x_specs + [
            _layer((MOD_ROWS, N_MOD * D_MODEL), l),
            _layer((3, D_MODEL), l),
            _layer((D_MODEL, 2 * D_FF), l),
            _layer((D_FF, D_MODEL), l),
        ],
        out_specs=out_specs,
        out_shape=out_shape,
        compiler_params=_params(("arbitrary",)),
        name="ffn",
    )(*xs, mod, g_norm, w_up, w_dn)


N_MIXIN_IN = 13


def _mixin_kernel(*refs, n_alias):
    (x_ref, mod_ref, g_ref, wa_ref, wbr_ref, wg_ref, bg_ref, gq_ref, gk_ref, seg_ref, segt_ref,
     cos_ref, sin_ref) = refs[:N_MIXIN_IN]
    (qa_ref, ka_ref, va_ref, kc_ref, vc_ref, qb_ref, kb_ref, vb_ref, ob_ref, gt_ref,
     br_ref) = refs[N_MIXIN_IN + n_alias:]
    is_ctx = pl.program_id(0) < N_CTX // TM
    x = x_ref[...]
    h = _norm_mod(x, g_ref[1:2, :], _mod_row(mod_ref, 3), _mod_row(mod_ref, 4)).astype(BF16)

    def proj(c0, c1):
        return _dot(h, wa_ref[:, c0:c1])

    def head_sums(p):
        hi, lo = _split2(p * p)
        return _dot(hi, seg_ref[...]) + _dot(lo, seg_ref[...])

    def head_scale(ssum):
        rh, rl = _split2(lax.rsqrt(ssum * (1.0 / HD_A) + EPS))
        return _dot(rh, segt_ref[...]) + _dot(rl, segt_ref[...])

    cosf = cos_ref[...]
    sinf = sin_ref[...]

    def rope(p):
        parts = []
        for hh in range(H_B):
            ph = p[:, HD_B * hh:HD_B * (hh + 1)]
            parts.append(ph * cosf + pltpu.roll(ph, HD_B // 2, 1) * sinf)
        return jnp.concatenate(parts, axis=1)

    cols = [0]
    for w in (A_WIDTH,) * 3 + (B_WIDTH,) * 4:
        cols.append(cols[-1] + w)
    p_qa = proj(cols[0], cols[1])
    p_ka = proj(cols[1], cols[2])
    s_qa = head_sums(p_qa)
    s_ka = head_sums(p_ka)
    va = proj(cols[2], cols[3])
    p_qb = proj(cols[3], cols[4])
    r_qa = head_scale(s_qa)
    r_ka = head_scale(s_ka)
    p_kb = proj(cols[4], cols[5])
    p_vb = proj(cols[5], cols[6])
    p_ob = proj(cols[6], cols[7])

    qa_ref[...] = (p_qa * r_qa * gq_ref[...] * ASCALE).astype(BF16)
    ka = p_ka * r_ka * gk_ref[...]
    ka_ref[...] = ka.astype(BF16)
    va_ref[...] = va.astype(BF16)

    @pl.when(is_ctx)
    def _():
        kc_ref[...] = ka.reshape(kc_ref.shape)
        vc_ref[...] = va.reshape(vc_ref.shape)

    qb_ref[...] = rope(p_qb).astype(BF16)
    kb_ref[...] = (rope(p_kb) * KSCALE).astype(BF16)
    vb_ref[...] = p_vb.astype(BF16)
    ob_ref[...] = jax.nn.sigmoid(p_ob).astype(BF16)
    for j in range(2 * D_MODEL // 512):
        sl = slice(512 * j, 512 * (j + 1))
        br_ref[:, sl] = jax.nn.sigmoid(_dot(h, wbr_ref[:, sl])).astype(BF16)

    pg = _dot(h, wg_ref[...]) + bg_ref[...]
    ls = jnp.minimum(pg, 0.0) - jnp.log1p(jnp.exp(-jnp.abs(pg)))
    lane = lax.broadcasted_iota(jnp.int32, pg.shape, 1)
    is_f = ((lane >= H_B) & (lane < 2 * H_B)) | ((lane >= 3 * H_B) & (lane < 4 * H_B))
    gt_ref[...] = jnp.where(is_f, ls, pg)


def _mixin(x, mod, g_norm, w_a, w_br, w_gt, b_gt, gq, gk, seg, segt, cosf, sinf, l, caches=None):
    tok = _tok_spec
    out = lambda w, dt: jax.ShapeDtypeStruct((N_TOK, w), dt)
    seqs = TM // CTX_LEN
    ctx_out = jax.ShapeDtypeStruct((N_CTX_B, DEPTH, CTX_LEN, A_WIDTH), F32)
    cache_spec = pl.BlockSpec((seqs, None, CTX_LEN, A_WIDTH),
                              lambda i: (jnp.minimum(i, N_CTX // TM - 1), l, 0, 0))
    caches = () if caches is None else tuple(caches)
    return pl.pallas_call(
        functools.partial(_mixin_kernel, n_alias=len(caches)),
        grid=(N_TOK // TM,),
        input_output_aliases={N_MIXIN_IN + j: 3 + j for j in range(len(caches))},
        in_specs=[
            tok(D_MODEL),
            _layer((MOD_ROWS, N_MOD * D_MODEL), l),
            _layer((3, D_MODEL), l),
            _layer((D_MODEL, QKV_COLS), l),
            _layer((D_MODEL, 2 * D_MODEL), l),
            _layer((D_MODEL, LANES), l),
            _layer((1, LANES), l),
            _layer((1, A_WIDTH), l),
            _layer((1, A_WIDTH), l),
            _resident((A_WIDTH, LANES)),
            _resident((LANES, A_WIDTH)),
            tok(HD_B),
            tok(HD_B),
        ] + [pl.BlockSpec(memory_space=pl.ANY)] * len(caches),
        out_specs=[tok(A_WIDTH), tok(A_WIDTH), tok(A_WIDTH), cache_spec, cache_spec,
                   tok(B_WIDTH), tok(B_WIDTH), tok(B_WIDTH), tok(B_WIDTH), tok(LANES), tok(2 * D_MODEL)],
        out_shape=[out(A_WIDTH, BF16), out(A_WIDTH, BF16), out(A_WIDTH, BF16), ctx_out, ctx_out,
                   out(B_WIDTH, BF16), out(B_WIDTH, BF16), out(B_WIDTH, BF16),
                   out(B_WIDTH, BF16), out(LANES, F32), out(2 * D_MODEL, BF16)],
        compiler_params=_params(("arbitrary",)),
        name="mixin",
    )(x, mod, g_norm, w_a, w_br, w_gt, b_gt, gq, gk, seg, segt, cosf, sinf, *caches)


CTX_SEQS = 2


def _head_pair_masks():
    lane = lax.broadcasted_iota(jnp.int32, (1, LANES), 1)
    first = lane < HD_A
    return first, [jnp.where(first, 1.0, 0.0).astype(BF16), jnp.where(first, 0.0, 1.0).astype(BF16)]


def _ctx_attn_kernel(q_ref, k_ref, v_ref, o_ref):
    first, keep = _head_pair_masks()
    q = q_ref[...]
    k = k_ref[...]
    v = v_ref[...]
    items = [(slice(CTX_LEN * sq, CTX_LEN * (sq + 1)), slice(LANES * t, LANES * (t + 1)), half)
             for sq in range(CTX_SEQS) for t in range(A_WIDTH // LANES) for half in range(2)]
    scores = [_dot_nt(q[rows, tile] * keep[half], k[rows, tile]) for rows, tile, half in items]
    probs = [jnp.exp(s - jnp.max(s, axis=-1, keepdims=True)) for s in scores]
    outs = [_dot(p.astype(BF16), v[rows, tile]) / jnp.sum(p, axis=-1, keepdims=True)
            for p, (rows, tile, half) in zip(probs, items)]
    for (rows, tile, _), o_first, o_second in zip(items[0::2], outs[0::2], outs[1::2]):
        o_ref[rows, tile] = jnp.where(first, o_first, o_second).astype(BF16)


def _ctx_attn(qa, ka, va):
    blk = pl.BlockSpec((CTX_SEQS * CTX_LEN, A_WIDTH), lambda b: (b, 0))
    return pl.pallas_call(
        _ctx_attn_kernel,
        grid=(N_CTX_B // CTX_SEQS,),
        in_specs=[blk, blk, blk],
        out_specs=blk,
        out_shape=jax.ShapeDtypeStruct((N_CTX, A_WIDTH), BF16),
        compiler_params=_params(("parallel",)),
        name="ctx_attn",
    )(qa, ka, va)


def _na_key_row0(j):
    return jnp.clip(NA_RB * j - WIN_H // 2, 0, LAT_ROWS - NA_KR)


def _na_kernel(q_ref, k_ref, v_ref, ck_ref, cv_ref, bias_ref, o_ref):
    j = pl.program_id(0)
    b = pl.program_id(1)
    base = pl.multiple_of(b * LAT_LEN + _na_key_row0(j) * GRID_W, GRID_W)
    kw = k_ref[pl.ds(base, NA_K), :]
    vw = v_ref[pl.ds(base, NA_K), :]
    ck = ck_ref[b].astype(BF16)
    cv = cv_ref[b].astype(BF16)
    q = q_ref[...]
    heads = [slice(HD_A * hh, HD_A * (hh + 1)) for hh in range(H_A)]
    s_win = [_dot_nt(q[:, sl], kw[:, sl]) for sl in heads]
    s_ctx = [_dot_nt(q[:, sl], ck[:, sl]) for sl in heads]
    probs = []
    for hh in range(H_A):
        sw = s_win[hh] + bias_ref[hh]
        sx = s_ctx[hh]
        m = jnp.maximum(jnp.max(sw, axis=-1, keepdims=True), jnp.max(sx, axis=-1, keepdims=True))
        pw = jnp.exp(sw - m)
        px = jnp.exp(sx - m)
        l = jnp.sum(pw, axis=-1, keepdims=True) + jnp.sum(px, axis=-1, keepdims=True)
        probs.append((pw.astype(BF16), px.astype(BF16), l))
    outs = [(_dot(pw, vw[:, sl]) + _dot(px, cv[:, sl])) / l for (pw, px, l), sl in zip(probs, heads)]
    o_ref[...] = jnp.concatenate(outs, axis=1).astype(BF16)


def _na_attn(qa, ka, va, ck, cv, bias, l):
    past = ck.shape[2]
    lat_kv = pl.BlockSpec((N_LAT, A_WIDTH), lambda j, b: (N_CTX // N_LAT, 0), pipeline_mode=pl.Buffered(1))
    cache = pl.BlockSpec((N_LAT_B, None, past, A_WIDTH), lambda j, b: (0, l, 0, 0), pipeline_mode=pl.Buffered(1))
    bias_class = lambda j: jnp.where(j == 0, 0, jnp.where(j == NA_BLOCKS - 1, 2, 1))
    return pl.pallas_call(
        _na_kernel,
        grid=(NA_BLOCKS, N_LAT_B),
        in_specs=[
            pl.BlockSpec((NA_Q, A_WIDTH), lambda j, b: (N_CTX // NA_Q + b * NA_BLOCKS + j, 0)),
            lat_kv, lat_kv, cache, cache,
            pl.BlockSpec((None, None, H_A, NA_Q, NA_K), lambda j, b: (l, bias_class(j), 0, 0, 0)),
        ],
        out_specs=pl.BlockSpec((NA_Q, A_WIDTH), lambda j, b: (b * NA_BLOCKS + j, 0)),
        out_shape=jax.ShapeDtypeStruct((N_LAT, A_WIDTH), BF16),
        compiler_params=_params(("arbitrary", "arbitrary")),
        name="na_attn",
    )(qa, ka, va, ck, cv, bias)


def _na_class_geometry():
    out = []
    for j in (0, 1, NA_BLOCKS - 1):
        r = NA_RB * j + np.arange(NA_RB)
        kr = int(np.clip(NA_RB * j - WIN_H // 2, 0, LAT_ROWS - NA_KR)) + np.arange(NA_KR)
        start_r = np.clip(r - WIN_H // 2, 0, LAT_ROWS - WIN_H)
        row_ok = (kr[None, :] >= start_r[:, None]) & (kr[None, :] < start_r[:, None] + WIN_H)
        dr = kr[None, :] - r[:, None] + (WIN_H - 1)
        out.append((dr, row_ok))
    return out


def _bias_kernel(r_ref, o_ref):
    qc = lax.broadcasted_iota(jnp.int32, (GRID_W, GRID_W), 0)
    kc = lax.broadcasted_iota(jnp.int32, (GRID_W, GRID_W), 1)
    c0 = jnp.clip(qc - WIN_W // 2, 0, GRID_W - WIN_W)
    col_ok = (kc >= c0) & (kc < c0 + WIN_W)
    neg = jnp.full((GRID_W, GRID_W), -1e30, F32)
    toep = []
    for dr in range(2 * WIN_H - 1):
        rows = jnp.broadcast_to(r_ref[0, 0, dr:dr + 1, :], (GRID_W, LANES))
        t = pltpu.roll(rows, 0, 1, stride=1, stride_axis=0)[:, :GRID_W]
        toep.append(jnp.where(col_ok, t, neg))
    for cls, (dr_tab, row_ok) in enumerate(_na_class_geometry()):
        for a in range(NA_RB):
            for jj in range(NA_KR):
                o_ref[0, cls, 0, GRID_W * a:GRID_W * (a + 1), GRID_W * jj:GRID_W * (jj + 1)] = (
                    toep[int(dr_tab[a, jj])] if row_ok[a, jj] else neg)


def _bias_tables(rpb):
    n_dr, n_dc = 2 * WIN_H - 1, 2 * WIN_W - 1
    assert rpb.shape == (DEPTH, H_A, n_dr, n_dc)
    rows = jnp.concatenate([rpb[..., WIN_W - 1:], jnp.zeros((DEPTH, H_A, n_dr, LANES - n_dc), F32),
                            rpb[..., :WIN_W - 1]], axis=-1)
    rows = jnp.pad(rows, ((0, 0), (0, 0), (0, 16 - n_dr), (0, 0)))
    return pl.pallas_call(
        _bias_kernel,
        grid=(DEPTH, H_A),
        in_specs=[pl.BlockSpec((1, 1, 16, LANES), lambda l, h: (l, h, 0, 0))],
        out_specs=pl.BlockSpec((1, 3, 1, NA_Q, NA_K), lambda l, h: (l, 0, h, 0, 0)),
        out_shape=jax.ShapeDtypeStruct((DEPTH, 3, H_A, NA_Q, NA_K), F32),
        compiler_params=_params(("arbitrary", "arbitrary")),
        name="na_bias",
    )(rows)


def _mlstm_kernel(*refs, n_chunks, has_init, want_state, n_alias):
    dirs = (refs[0:4], refs[4:8])
    pos = 8
    if has_init:
        c0_ref, n0_ref, m0_ref = refs[pos:pos + 3]
        pos += 3
    pos += n_alias
    n_h = 1 if n_chunks == 1 else 2
    h_refs = refs[pos:pos + n_h]
    pos += n_h
    if want_state:
        cf_ref, nf_ref, mf_ref = refs[pos:pos + 3]
        pos += 3
    cs_ref, ms_ref = refs[pos:pos + 2]
    c = pl.program_id(1)

    def tail_rows(x):
        r = lax.broadcasted_iota(jnp.int32, (STATE_TAIL, x.shape[1]), 0)
        return jnp.where(r == 0, x, jnp.where(r == 1, x - x.astype(BF16).astype(F32), 0.0))

    @pl.when(c == 0)
    def _():
        if has_init:
            cs_ref[:, 0:HD_B, :] = c0_ref[0]
            for i in range(N_STATE):
                cs_ref[i, HD_B:HD_B + STATE_TAIL, :] = tail_rows(n0_ref[0, i:i + 1, :])
            ms_ref[...] = m0_ref[0]
        else:
            cs_ref[...] = jnp.zeros(cs_ref.shape, F32)
            ms_ref[...] = jnp.zeros(ms_ref.shape, F32)

    row = lax.broadcasted_iota(jnp.int32, (CHUNK, CHUNK), 0)
    col = lax.broadcasted_iota(jnp.int32, (CHUNK, CHUNK), 1)
    chains = []
    for d in range(2):
        q_ref, k_ref, v_ref, g_ref = dirs[d]
        vis = (row <= col) if d == 0 else (row >= col)
        cum_b = jnp.where((col <= row) if d == 0 else (col >= row), 1.0, 0.0).astype(BF16)
        g_t = g_ref[...].T[0:N_GATE, :]
        r = _dot_nt(jnp.concatenate(_split3(g_t), axis=0), cum_b)
        b_t = r[0:N_GATE] + r[N_GATE:2 * N_GATE] + r[2 * N_GATE:3 * N_GATE]
        i0 = 2 * H_B * d
        u_t = g_t[i0:i0 + H_B, :] - b_t[i0 + H_B:i0 + 2 * H_B, :]
        u_cols = jnp.concatenate([u_t, jnp.zeros((LANES - H_B, CHUNK), F32)], axis=0).T
        q = q_ref[...]
        k = k_ref[...]
        v32 = v_ref[...].astype(F32)
        for hh in range(H_B):
            sl = slice(HD_B * hh, HD_B * (hh + 1))
            chains.append(dict(
                d=d, i=d * H_B + hh, sl=sl, vis=vis, qh=q[:, sl], kh=k[:, sl], v32=v32,
                u_col=u_cols[:, hh:hh + 1],
                b_row=b_t[i0 + H_B + hh:i0 + H_B + hh + 1, :],
                i_row=g_t[i0 + hh:i0 + hh + 1, :]))

    for ch in chains:
        ch["state"] = cs_ref[ch["i"]]
        ch["r"] = _dot_nt(jnp.concatenate([ch["kh"], ch["state"].astype(BF16)], axis=0), ch["qh"])
    for ch in chains:
        b_row, m_prev = ch["b_row"], ms_ref[ch["i"]:ch["i"] + 1, 0:1]
        dm = jnp.where(ch["vis"], ch["u_col"] + b_row, -jnp.inf)
        inter = b_row + m_prev
        m_t = jnp.maximum(inter, jnp.max(dm, axis=0, keepdims=True))
        ch["a"] = jnp.exp(inter - m_t)
        ch["w"] = jnp.exp(dm - m_t)
        ch["floor"] = jnp.exp(-m_t)
        b_end = jnp.broadcast_to(b_row[:, CHUNK - 1:CHUNK] if ch["d"] == 0 else b_row[:, 0:1], (1, CHUNK))
        gl = b_end - b_row + ch["i_row"]
        m_new = jnp.maximum(b_end + m_prev, jnp.max(gl, axis=-1, keepdims=True))
        ch["a_s"] = jnp.exp(b_end + m_prev - m_new)[:, 0:HD_B]
        ch["w_s"] = jnp.exp(gl - m_new)
        ch["m_new"] = m_new
    for ch in chains:
        r, a = ch["r"], ch["a"]
        p = r[0:CHUNK] * ch["w"]
        qn = r[CHUNK + HD_B:CHUNK + HD_B + 1] + r[CHUNK + HD_B + 1:CHUNK + HD_B + 2]
        den = a * qn + jnp.sum(p, axis=0, keepdims=True)
        ch["v_t"] = ch["v32"][:, ch["sl"]].T
        num = a * r[CHUNK:CHUNK + HD_B] + _dot(ch["v_t"].astype(BF16), p.astype(BF16))
        h_t = num * (1.0 / jnp.maximum(jnp.abs(den), ch["floor"]))
        if n_h == 2:
            h_refs[ch["d"]][:, ch["sl"]] = h_t.T
        else:
            ch["h_t"] = h_t
    if n_h == 1:
        for fwd_ch, bwd_ch in zip(chains[:H_B], chains[H_B:]):
            h_refs[0][:, fwd_ch["sl"]] = (fwd_ch["h_t"] + bwd_ch["h_t"]).T
    for ch in chains:
        i, a_s, state = ch["i"], ch["a_s"], ch["state"]
        lhs = jnp.concatenate([ch["v_t"] * ch["w_s"], tail_rows(ch["w_s"])], axis=0).astype(BF16)
        upd = _dot(lhs, ch["kh"])
        n_new = a_s * state[HD_B:HD_B + 1] + upd[HD_B:HD_B + 1] + upd[HD_B + 1:HD_B + 2]
        cs_ref[i, 0:HD_B, :] = a_s * state[0:HD_B] + upd[0:HD_B]
        cs_ref[i, HD_B:HD_B + STATE_TAIL, :] = tail_rows(n_new)
        ms_ref[i:i + 1, :] = ch["m_new"][:, 0:LANES]

    if want_state:
        @pl.when(c == n_chunks - 1)
        def _():
            cf_ref[0] = cs_ref[:, 0:HD_B, :]
            for i in range(N_STATE):
                nf_ref[0, i:i + 1, :] = cs_ref[i, HD_B:HD_B + 1, :]
            mf_ref[0] = ms_ref[...]


def _mlstm(qb, kb, vb, gt, row0, n_batch, seq_len, layer, init=None, want_state=False, c_prev=None):
    n_chunks = seq_len // CHUNK
    blk0 = row0 // CHUNK

    def fwd(w):
        return pl.BlockSpec((CHUNK, w), lambda b, c: (blk0 + b * n_chunks + c, 0))

    def bwd(w):
        return pl.BlockSpec((CHUNK, w), lambda b, c: (blk0 + b * n_chunks + n_chunks - 1 - c, 0))

    in_specs = [fwd(B_WIDTH), fwd(B_WIDTH), fwd(B_WIDTH), fwd(LANES),
                bwd(B_WIDTH), bwd(B_WIDTH), bwd(B_WIDTH), bwd(LANES)]
    args = [qb, kb, vb, gt, qb, kb, vb, gt]
    c_spec = pl.BlockSpec((1, None, N_STATE, HD_B, HD_B), lambda b, c: (b, layer, 0, 0, 0))
    if init is not None:
        in_specs += [c_spec,
                     pl.BlockSpec((1, None, N_STATE, LANES), lambda b, c: (b, layer, 0, 0)),
                     pl.BlockSpec((1, None, N_STATE, LANES), lambda b, c: (b, layer, 0, 0))]
        args += list(init)
    n_rows = n_batch * seq_len
    out_specs = [pl.BlockSpec((CHUNK, B_WIDTH), lambda b, c: (b * n_chunks + c, 0)),
                 pl.BlockSpec((CHUNK, B_WIDTH), lambda b, c: (b * n_chunks + n_chunks - 1 - c, 0))]
    if n_chunks == 1:
        out_specs = out_specs[:1]
    out_shape = [jax.ShapeDtypeStruct((n_rows, B_WIDTH), F32)] * len(out_specs)
    aliases = {}
    if c_prev is not None:
        aliases = {len(args): len(out_specs)}
        in_specs += [pl.BlockSpec(memory_space=pl.ANY)]
        args += [c_prev]
    if want_state:
        out_specs += [c_spec,
                      pl.BlockSpec((1, N_STATE, LANES), lambda b, c: (b, 0, 0)),
                      pl.BlockSpec((1, N_STATE, LANES), lambda b, c: (b, 0, 0))]
        out_shape += [jax.ShapeDtypeStruct((n_batch, DEPTH, N_STATE, HD_B, HD_B), F32),
                      jax.ShapeDtypeStruct((n_batch, N_STATE, LANES), F32),
                      jax.ShapeDtypeStruct((n_batch, N_STATE, LANES), F32)]
    return pl.pallas_call(
        functools.partial(_mlstm_kernel, n_chunks=n_chunks, has_init=init is not None, want_state=want_state,
                          n_alias=len(aliases)),
        grid=(n_batch, n_chunks),
        input_output_aliases=aliases,
        in_specs=in_specs,
        out_specs=out_specs,
        out_shape=out_shape,
        scratch_shapes=[pltpu.VMEM((N_STATE, HD_B + STATE_TAIL, HD_B), F32),
                        pltpu.VMEM((N_STATE, LANES), F32)],
        compiler_params=_params(("parallel", "arbitrary")),
        name="mlstm",
    )(*args)


def _mixout_ffn_kernel(*refs, split_out, n_hc, n_hl):
    x_ref, mod_ref, attc_ref, attl_ref = refs[:4]
    hc_refs = refs[4:4 + n_hc]
    hl_refs = refs[4 + n_hc:4 + n_hc + n_hl]
    (ob_ref, br_ref, g_ref, ghn_ref, wpa_ref, wpb_ref, wo_ref, wup_ref,
     wdn_ref) = refs[4 + n_hc + n_hl:13 + n_hc + n_hl]
    o_refs = refs[13 + n_hc + n_hl:]
    is_ctx = pl.program_id(0) < N_CTX // TM
    att = jnp.where(is_ctx, attc_ref[...], attl_ref[...])
    hsum = jnp.where(is_ctx, sum(r[...] for r in hc_refs), sum(r[...] for r in hl_refs))
    parts = []
    for hh in range(H_B):
        seg = hsum[:, HD_B * hh:HD_B * (hh + 1)]
        parts.append(seg * lax.rsqrt(jnp.mean(seg * seg, axis=-1, keepdims=True) + EPS))
    hn = (jnp.concatenate(parts, axis=1) * ghn_ref[...]) * ob_ref[...].astype(F32)
    y = (br_ref[:, :D_MODEL].astype(F32) * _dot(att, wpa_ref[...])
         + br_ref[:, D_MODEL:].astype(F32) * _dot(hn.astype(BF16), wpb_ref[...]))
    x = x_ref[...] + _mod_row(mod_ref, 5) * _dot(y.astype(BF16), wo_ref[...])
    out = _swiglu_sub(x, g_ref[2:3, :], _mod_row(mod_ref, 6), _mod_row(mod_ref, 7), _mod_row(mod_ref, 8),
                      wup_ref, wdn_ref)
    _write_stream(o_refs, out, is_ctx, split_out)


def _mixout_ffn(x, mod, att_c, att_l, h_c, h_l, sob, sbr, g_norm, g_hn, w_pa, w_pb, w_o,
                w_up, w_dn, l, split_out=False):
    tok, ctx, lat = _tok_spec, _ctx_spec, _lat_spec
    if split_out:
        out_specs = [ctx(D_MODEL), lat(D_MODEL)]
        out_shape = [jax.ShapeDtypeStruct((N_CTX, D_MODEL), F32), jax.ShapeDtypeStruct((N_LAT, D_MODEL), F32)]
    else:
        out_specs = tok(D_MODEL)
        out_shape = jax.ShapeDtypeStruct((N_TOK, D_MODEL), F32)
    return pl.pallas_call(
        functools.partial(_mixout_ffn_kernel, split_out=split_out, n_hc=len(h_c), n_hl=len(h_l)),
        grid=(N_TOK // TM,),
        in_specs=[
            tok(D_MODEL),
            _layer((MOD_ROWS, N_MOD * D_MODEL), l),
            ctx(A_WIDTH), lat(A_WIDTH),
        ] + [ctx(B_WIDTH)] * len(h_c) + [lat(B_WIDTH)] * len(h_l) + [
            tok(B_WIDTH), tok(2 * D_MODEL),
            _layer((3, D_MODEL), l),
            _layer((1, B_WIDTH), l),
            _layer((A_WIDTH, D_MODEL), l),
            _layer((B_WIDTH, D_MODEL), l),
            _layer((D_MODEL, D_MODEL), l),
            _layer((D_MODEL, 2 * D_FF), l),
            _layer((D_FF, D_MODEL), l),
        ],
        out_specs=out_specs,
        out_shape=out_shape,
        compiler_params=_params(("arbitrary",)),
        name="mixout_ffn",
    )(x, mod, att_c, att_l, *h_c, *h_l, sob, sbr, g_norm, g_hn, w_pa, w_pb, w_o, w_up, w_dn)


def _rope_tables():
    half = HD_B // 2
    nf = half // 2
    inv = ROPE_THETA ** (-jnp.arange(nf, dtype=F32) / nf)
    t = jnp.arange(LAT_LEN)
    rowp = (t // GRID_W).astype(F32)
    colp = (t % GRID_W).astype(F32)
    ang = jnp.concatenate([rowp[:, None] * inv, colp[:, None] * inv], axis=-1)
    cos, sin = jnp.cos(ang), jnp.sin(ang)
    cos_l = jnp.tile(jnp.concatenate([cos, cos], axis=-1), (N_LAT_B, 1))
    sin_l = jnp.tile(jnp.concatenate([-sin, sin], axis=-1), (N_LAT_B, 1))
    cosf = jnp.concatenate([jnp.ones((N_CTX, HD_B), F32), cos_l], axis=0)
    sinf = jnp.concatenate([jnp.zeros((N_CTX, HD_B), F32), sin_l], axis=0)
    return cosf, sinf


def kernel(x_prompt, x_sample, cache_k, cache_v, state_C, state_n, state_m, c, c_ctx, w_ada, b_ada, g_norm,
           w_up1, w_dn1, w_in, b_gates, g_qn, g_kn, rpb, g_hn, w_pa, w_pb, w_o, w_up2, w_dn2):
    assert x_prompt.shape == (N_CTX_B, CTX_LEN, D_MODEL) and x_sample.shape == (N_LAT_B, LAT_LEN, D_MODEL)
    past_len = cache_k.shape[2]

    w_a = w_in[..., :QKV_COLS].astype(BF16)
    w_br = w_in[..., QKV_COLS + N_GATE:].astype(BF16)
    w_gt = jnp.pad(w_in[..., QKV_COLS:QKV_COLS + N_GATE], ((0, 0), (0, 0), (0, LANES - N_GATE))).astype(BF16)
    b_gt = jnp.pad(b_gates, ((0, 0), (0, LANES - N_GATE))).reshape(DEPTH, 1, LANES)
    w_up1b, w_dn1b, w_up2b, w_dn2b = (w.astype(BF16) for w in (w_up1, w_dn1, w_up2, w_dn2))
    w_pab, w_pbb, w_ob = (w.astype(BF16) for w in (w_pa, w_pb, w_o))
    gq = jnp.tile(g_qn, (1, H_A)).reshape(DEPTH, 1, A_WIDTH)
    gk = jnp.tile(g_kn, (1, H_A)).reshape(DEPTH, 1, A_WIDTH)
    ghn = g_hn.reshape(DEPTH, 1, B_WIDTH)
    seg_np = (np.arange(A_WIDTH)[:, None] // HD_A == np.arange(LANES)[None, :]).astype(np.float32)
    seg = jnp.asarray(seg_np, BF16)
    segt = jnp.asarray(seg_np.T, BF16)
    cosf, sinf = _rope_tables()
    bias = _bias_tables(rpb)
    ck = cache_k.reshape(N_LAT_B, DEPTH, past_len, A_WIDTH)
    cv = cache_v.reshape(N_LAT_B, DEPTH, past_len, A_WIDTH)
    init = (state_C.reshape(N_LAT_B, DEPTH, N_STATE, HD_B, HD_B),
            state_n.reshape(N_LAT_B, DEPTH, N_STATE, HD_B),
            jnp.broadcast_to(state_m.reshape(N_LAT_B, DEPTH, N_STATE, 1), (N_LAT_B, DEPTH, N_STATE, LANES)))

    cond = jnp.concatenate([c_ctx[None], c, jnp.zeros((MOD_ROWS - 1 - N_LAT_B, D_MODEL), F32)], axis=0)
    mod = _ada(cond, w_ada, b_ada)

    x = (x_prompt.reshape(N_CTX, D_MODEL), x_sample.reshape(N_LAT, D_MODEL))
    caches = tuple(jnp.zeros((N_CTX_B, DEPTH, CTX_LEN, A_WIDTH), F32) for _ in range(2))
    c_new = jnp.zeros((N_CTX_B, DEPTH, N_STATE, HD_B, HD_B), F32)
    ns_out, ms_out = [], []
    for l in range(DEPTH):
        x = _ffn(x, mod, g_norm, w_up1b, w_dn1b, l, 0)
        qa, ka, va, k_new, v_new, qb, kb, vb, sob, gt, sbr = _mixin(
            x, mod, g_norm, w_a, w_br, w_gt, b_gt, gq, gk, seg, segt, cosf, sinf, l, caches)
        caches = (k_new, v_new)
        att_c = _ctx_attn(qa, ka, va)
        att_l = _na_attn(qa, ka, va, ck, cv, bias, l)
        *h_c, c_new, n_fin, m_fin = _mlstm(qb, kb, vb, gt, 0, N_CTX_B, CTX_LEN, l, want_state=True, c_prev=c_new)
        h_l = _mlstm(qb, kb, vb, gt, N_CTX, N_LAT_B, LAT_LEN, l, init=init)
        x = _mixout_ffn(x, mod, att_c, att_l, h_c, h_l, sob, sbr, g_norm, ghn, w_pab, w_pbb, w_ob,
                        w_up2b, w_dn2b, l, split_out=(l == DEPTH - 1))
        ns_out.append(n_fin.reshape(N_CTX_B, 2, H_B, HD_B))
        ms_out.append(m_fin[:, :, 0].reshape(N_CTX_B, 2, H_B))

    return (x[0].reshape(N_CTX_B, CTX_LEN, D_MODEL),
            x[1].reshape(N_LAT_B, LAT_LEN, D_MODEL),
            k_new.reshape(N_CTX_B, DEPTH, CTX_LEN, H_A, HD_A),
            v_new.reshape(N_CTX_B, DEPTH, CTX_LEN, H_A, HD_A),
            c_new.reshape(N_CTX_B, DEPTH, 2, H_B, HD_B, HD_B),
            jnp.stack(ns_out, axis=1), jnp.stack(ms_out, axis=1))
```

```python
import functools

import jax
import jax.numpy as jnp
import numpy as np
from jax import lax
from jax.experimental import pallas as pl
from jax.experimental.pallas import tpu as pltpu

F32 = jnp.float32
BF16 = jnp.bfloat16

D_MODEL = 1024
DEPTH = 4
N_CTX_B, CTX_LEN = 32, 256
N_LAT_B, LAT_LEN = 2, 2048
N_CTX = N_CTX_B * CTX_LEN
N_LAT = N_LAT_B * LAT_LEN
N_TOK = N_CTX + N_LAT
GRID_W, WIN_H, WIN_W = 64, 8, 16
LAT_ROWS = LAT_LEN // GRID_W
H_A, HD_A = 8, 64
H_B, HD_B = 4, 128
A_WIDTH = H_A * HD_A
B_WIDTH = H_B * HD_B
D_FF = 2816
N_MOD = 9
CHUNK = 256
ROPE_THETA = 10000.0
EPS = 1e-6
KSCALE = HD_B ** -0.5
ASCALE = HD_A ** -0.5
N_GATE = 4 * H_B
QKV_COLS = 3 * A_WIDTH + 4 * B_WIDTH
N_STATE = 2 * H_B
STATE_TAIL = 16
MOD_ROWS = 8

LANES = 128
TM = 512
TM_FFN = 512
FF_CHUNK = 256
VMEM_LIMIT = 52 * 1024 * 1024

NA_RB = WIN_H // 2
NA_KR = NA_RB + WIN_H - 1
NA_Q = NA_RB * GRID_W
NA_K = NA_KR * GRID_W
NA_BLOCKS = LAT_ROWS // NA_RB
assert LAT_ROWS % NA_RB == 0 and NA_BLOCKS >= 3 and N_CTX % N_LAT == 0 and 1 + N_LAT_B <= MOD_ROWS


def _dot(a, b):
    return jnp.dot(a, b, preferred_element_type=F32)


def _dot_nt(a, b):
    return lax.dot_general(a, b, (((1,), (1,)), ((), ())), preferred_element_type=F32)


def _split2(x):
    hi = x.astype(BF16)
    lo = (x - hi.astype(F32)).astype(BF16)
    return hi, lo


def _split3(x):
    hi = x.astype(BF16)
    r1 = x - hi.astype(F32)
    mid = r1.astype(BF16)
    lo = (r1 - mid.astype(F32)).astype(BF16)
    return hi, mid, lo


def _norm_mod(x, g, shift, scale):
    ms = jnp.mean(x * x, axis=-1, keepdims=True)
    return (x * lax.rsqrt(ms + EPS) * g) * (1.0 + scale) + shift


def _params(sem):
    return pltpu.CompilerParams(dimension_semantics=sem, vmem_limit_bytes=VMEM_LIMIT)


def _resident(shape):
    nd = len(shape)
    return pl.BlockSpec(shape, lambda *_: (0,) * nd, pipeline_mode=pl.Buffered(1))


def _layer(shape, l):
    nd = len(shape)
    return pl.BlockSpec((None,) + tuple(shape), lambda *_: (l,) + (0,) * nd, pipeline_mode=pl.Buffered(1))


def _tok_spec(w, tm=TM):
    return pl.BlockSpec((tm, w), lambda i: (i, 0))


def _ctx_spec(w, tm=TM):
    return pl.BlockSpec((tm, w), lambda i: (jnp.minimum(i, N_CTX // tm - 1), 0))


def _lat_spec(w, tm=TM):
    return pl.BlockSpec((tm, w), lambda i: (jnp.maximum(i - N_CTX // tm, 0), 0))


def _mod_row(mod_ref, k, tm=TM):
    i = pl.program_id(0)
    n_ctx_tiles = N_CTX // tm
    mod_set = jnp.where(i < n_ctx_tiles, 0, 1 + (i - n_ctx_tiles) // (LAT_LEN // tm))
    return mod_ref[pl.ds(mod_set, 1), D_MODEL * k:D_MODEL * (k + 1)]


def _ada_kernel(c_ref, w_ref, b_ref, o_ref):
    c = c_ref[...]
    s = c * jax.nn.sigmoid(c)
    sh, sl = _split2(s)
    wh, wl = _split2(w_ref[0])
    o_ref[0] = _dot(sh, wh) + _dot(sl, wh) + _dot(sh, wl) + b_ref[0]


def _ada(cond, w_ada, b_ada):
    tn = 1024
    n_out = N_MOD * D_MODEL
    return pl.pallas_call(
        _ada_kernel,
        grid=(DEPTH, n_out // tn),
        in_specs=[
            pl.BlockSpec((MOD_ROWS, D_MODEL), lambda l, j: (0, 0)),
            pl.BlockSpec((1, D_MODEL, tn), lambda l, j: (l, 0, j)),
            pl.BlockSpec((1, 1, tn), lambda l, j: (l, 0, j)),
        ],
        out_specs=pl.BlockSpec((1, MOD_ROWS, tn), lambda l, j: (l, 0, j)),
        out_shape=jax.ShapeDtypeStruct((DEPTH, MOD_ROWS, n_out), F32),
        compiler_params=_params(("parallel", "parallel")),
        name="ada",
    )(cond, w_ada, b_ada.reshape(DEPTH, 1, n_out))


N_FF_CHUNKS = D_FF // FF_CHUNK


def _ffn_weight_scratch():
    return [pltpu.VMEM((D_MODEL, 2 * D_FF), BF16),
            pltpu.VMEM((D_FF, D_MODEL), BF16),
            pltpu.VMEM((D_MODEL, FF_CHUNK), F32),
            pltpu.VMEM((D_MODEL, FF_CHUNK), F32),
            pltpu.VMEM((FF_CHUNK, D_MODEL), F32),
            pltpu.SemaphoreType.DMA((3,))]


def _ffn_weight_feed(wup_hbm, wdn_hbm, wup_ref, wdn_ref, st_a, st_u, st_d, sems):
    def copies(j):
        c0 = j * FF_CHUNK
        return (pltpu.make_async_copy(wup_hbm.at[:, pl.ds(c0, FF_CHUNK)], st_a, sems.at[0]),
                pltpu.make_async_copy(wup_hbm.at[:, pl.ds(D_FF + c0, FF_CHUNK)], st_u, sems.at[1]),
                pltpu.make_async_copy(wdn_hbm.at[pl.ds(c0, FF_CHUNK), :], st_d, sems.at[2]))

    def before_chunk(j):
        if j == 0:
            for cp in copies(0):
                cp.start()
        for cp in copies(j):
            cp.wait()
        c0 = j * FF_CHUNK
        wup_ref[:, c0:c0 + FF_CHUNK] = st_a[...].astype(BF16)
        wup_ref[:, D_FF + c0:D_FF + c0 + FF_CHUNK] = st_u[...].astype(BF16)
        wdn_ref[c0:c0 + FF_CHUNK, :] = st_d[...].astype(BF16)
        if j + 1 < N_FF_CHUNKS:
            for cp in copies(j + 1):
                cp.start()

    return before_chunk


def _swiglu_sub(x, g, shift, scale, gate, wup_ref, wdn_ref, before_chunk=None):
    h = _norm_mod(x, g, shift, scale).astype(BF16)
    acc = jnp.zeros(x.shape, F32)
    for j in range(N_FF_CHUNKS):
        c0 = j * FF_CHUNK
        if before_chunk is not None:
            before_chunk(j)
        a = _dot(h, wup_ref[:, c0:c0 + FF_CHUNK])
        u = _dot(h, wup_ref[:, D_FF + c0:D_FF + c0 + FF_CHUNK])
        g_act = (a * jax.nn.sigmoid(a) * u).astype(BF16)
        acc = acc + _dot(g_act, wdn_ref[c0:c0 + FF_CHUNK, :])
    return x + 0.5 * gate * acc


def _write_stream(o_refs, y, is_ctx, split_out):
    if split_out:
        @pl.when(is_ctx)
        def _():
            o_refs[0][...] = y

        @pl.when(jnp.logical_not(is_ctx))
        def _():
            o_refs[1][...] = y
    else:
        o_refs[0][...] = y


def _ffn_kernel(*refs, layer, sub, split_in, split_out):
    n_x = 2 if split_in else 1
    mod_ref, g_ref, wup_hbm, wdn_hbm = refs[n_x:n_x + 4]
    n_out = 2 if split_out else 1
    o_refs = refs[n_x + 4:n_x + 4 + n_out]
    wup_ref, wdn_ref, *stage = refs[n_x + 4 + n_out:]
    is_ctx = pl.program_id(0) < N_CTX // TM_FFN
    mod_row = functools.partial(_mod_row, mod_ref, tm=TM_FFN)

    def run(before_chunk):
        x = jnp.where(is_ctx, refs[0][...], refs[1][...]) if split_in else refs[0][...]
        y = _swiglu_sub(x, g_ref[sub:sub + 1, :], mod_row(3 * sub), mod_row(3 * sub + 1), mod_row(3 * sub + 2),
                        wup_ref, wdn_ref, before_chunk)
        _write_stream(o_refs, y, is_ctx, split_out)

    @pl.when(pl.program_id(0) == 0)
    def _():
        run(_ffn_weight_feed(wup_hbm.at[layer], wdn_hbm.at[layer], wup_ref, wdn_ref, *stage))

    @pl.when(pl.program_id(0) != 0)
    def _():
        run(None)


def _ffn(xs, mod, g_norm, w_up, w_dn, l, sub, split_out=False):
    split_in = isinstance(xs, tuple)
    xs = xs if split_in else (xs,)
    tm = TM_FFN
    x_specs = [_ctx_spec(D_MODEL, tm), _lat_spec(D_MODEL, tm)] if split_in else [_tok_spec(D_MODEL, tm)]
    if split_out:
        out_specs = [_ctx_spec(D_MODEL, tm), _lat_spec(D_MODEL, tm)]
        out_shape = [jax.ShapeDtypeStruct((N_CTX, D_MODEL), F32), jax.ShapeDtypeStruct((N_LAT, D_MODEL), F32)]
    else:
        out_specs = _tok_spec(D_MODEL, tm)
        out_shape = jax.ShapeDtypeStruct((N_TOK, D_MODEL), F32)
    return pl.pallas_call(
        functools.partial(_ffn_kernel, layer=l, sub=sub, split_in=split_in, split_out=split_out),
        grid=(N_TOK // tm,),
        in_specs=x_specs + [
            _layer((MOD_ROWS, N_MOD * D_MODEL), l),
            _layer((3, D_MODEL), l),
            pl.BlockSpec(memory_space=pl.ANY),
            pl.BlockSpec(memory_space=pl.ANY),
        ],
        out_specs=out_specs,
        out_shape=out_shape,
        scratch_shapes=_ffn_weight_scratch(),
        compiler_params=_params(("arbitrary",)),
        name="ffn",
    )(*xs, mod, g_norm, w_up, w_dn)


N_MIXIN_IN = 13


def _mixin_kernel(*refs, n_alias):
    (x_ref, mod_ref, g_ref, wa_ref, wbr_ref, wg_ref, bg_ref, gq_ref, gk_ref, seg_ref, segt_ref,
     cos_ref, sin_ref) = refs[:N_MIXIN_IN]
    (qa_ref, ka_ref, va_ref, kc_ref, vc_ref, qb_ref, kb_ref, vb_ref, ob_ref, gt_ref,
     br_ref) = refs[N_MIXIN_IN + n_alias:]
    is_ctx = pl.program_id(0) < N_CTX // TM
    x = x_ref[...]
    h = _norm_mod(x, g_ref[1:2, :], _mod_row(mod_ref, 3), _mod_row(mod_ref, 4)).astype(BF16)

    def proj(c0, c1):
        return _dot(h, wa_ref[:, c0:c1])

    def head_sums(p):
        hi, lo = _split2(p * p)
        return _dot(hi, seg_ref[...]) + _dot(lo, seg_ref[...])

    def head_scale(ssum):
        rh, rl = _split2(lax.rsqrt(ssum * (1.0 / HD_A) + EPS))
        return _dot(rh, segt_ref[...]) + _dot(rl, segt_ref[...])

    cosf = cos_ref[...]
    sinf = sin_ref[...]

    def rope(p):
        parts = []
        for hh in range(H_B):
            ph = p[:, HD_B * hh:HD_B * (hh + 1)]
            parts.append(ph * cosf + pltpu.roll(ph, HD_B // 2, 1) * sinf)
        return jnp.concatenate(parts, axis=1)

    cols = [0]
    for w in (A_WIDTH,) * 3 + (B_WIDTH,) * 4:
        cols.append(cols[-1] + w)
    p_qa = proj(cols[0], cols[1])
    p_ka = proj(cols[1], cols[2])
    s_qa = head_sums(p_qa)
    s_ka = head_sums(p_ka)
    va = proj(cols[2], cols[3])
    p_qb = proj(cols[3], cols[4])
    r_qa = head_scale(s_qa)
    r_ka = head_scale(s_ka)
    p_kb = proj(cols[4], cols[5])
    p_vb = proj(cols[5], cols[6])
    p_ob = proj(cols[6], cols[7])

    qa_ref[...] = (p_qa * r_qa * gq_ref[...] * ASCALE).astype(BF16)
    ka = p_ka * r_ka * gk_ref[...]
    ka_ref[...] = ka.astype(BF16)
    va_ref[...] = va.astype(BF16)

    @pl.when(is_ctx)
    def _():
        kc_ref[...] = ka.reshape(kc_ref.shape)
        vc_ref[...] = va.reshape(vc_ref.shape)

    qb_ref[...] = rope(p_qb).astype(BF16)
    kb_ref[...] = (rope(p_kb) * KSCALE).astype(BF16)
    vb_ref[...] = p_vb.astype(BF16)
    ob_ref[...] = jax.nn.sigmoid(p_ob).astype(BF16)
    for j in range(2 * D_MODEL // 512):
        sl = slice(512 * j, 512 * (j + 1))
        br_ref[:, sl] = jax.nn.sigmoid(_dot(h, wbr_ref[:, sl])).astype(BF16)

    pg = _dot(h, wg_ref[...]) + bg_ref[...]
    ls = jnp.minimum(pg, 0.0) - jnp.log1p(jnp.exp(-jnp.abs(pg)))
    lane = lax.broadcasted_iota(jnp.int32, pg.shape, 1)
    is_f = ((lane >= H_B) & (lane < 2 * H_B)) | ((lane >= 3 * H_B) & (lane < 4 * H_B))
    gt_ref[...] = jnp.where(is_f, ls, pg)


def _mixin(x, mod, g_norm, w_a, w_br, w_gt, b_gt, gq, gk, seg, segt, cosf, sinf, l, caches=None):
    tok = _tok_spec
    out = lambda w, dt: jax.ShapeDtypeStruct((N_TOK, w), dt)
    seqs = TM // CTX_LEN
    ctx_out = jax.ShapeDtypeStruct((N_CTX_B, DEPTH, CTX_LEN, A_WIDTH), F32)
    cache_spec = pl.BlockSpec((seqs, None, CTX_LEN, A_WIDTH),
                              lambda i: (jnp.minimum(i, N_CTX // TM - 1), l, 0, 0))
    caches = () if caches is None else tuple(caches)
    return pl.pallas_call(
        functools.partial(_mixin_kernel, n_alias=len(caches)),
        grid=(N_TOK // TM,),
        input_output_aliases={N_MIXIN_IN + j: 3 + j for j in range(len(caches))},
        in_specs=[
            tok(D_MODEL),
            _layer((MOD_ROWS, N_MOD * D_MODEL), l),
            _layer((3, D_MODEL), l),
            _layer((D_MODEL, QKV_COLS), l),
            _layer((D_MODEL, 2 * D_MODEL), l),
            _layer((D_MODEL, LANES), l),
            _layer((1, LANES), l),
            _layer((1, A_WIDTH), l),
            _layer((1, A_WIDTH), l),
            _resident((A_WIDTH, LANES)),
            _resident((LANES, A_WIDTH)),
            tok(HD_B),
            tok(HD_B),
        ] + [pl.BlockSpec(memory_space=pl.ANY)] * len(caches),
        out_specs=[tok(A_WIDTH), tok(A_WIDTH), tok(A_WIDTH), cache_spec, cache_spec,
                   tok(B_WIDTH), tok(B_WIDTH), tok(B_WIDTH), tok(B_WIDTH), tok(LANES), tok(2 * D_MODEL)],
        out_shape=[out(A_WIDTH, BF16), out(A_WIDTH, BF16), out(A_WIDTH, BF16), ctx_out, ctx_out,
                   out(B_WIDTH, BF16), out(B_WIDTH, BF16), out(B_WIDTH, BF16),
                   out(B_WIDTH, BF16), out(LANES, F32), out(2 * D_MODEL, BF16)],
        compiler_params=_params(("arbitrary",)),
        name="mixin",
    )(x, mod, g_norm, w_a, w_br, w_gt, b_gt, gq, gk, seg, segt, cosf, sinf, *caches)


CTX_SEQS = 2


def _head_pair_masks():
    lane = lax.broadcasted_iota(jnp.int32, (1, LANES), 1)
    first = lane < HD_A
    return first, [jnp.where(first, 1.0, 0.0).astype(BF16), jnp.where(first, 0.0, 1.0).astype(BF16)]


def _ctx_attn_kernel(q_ref, k_ref, v_ref, o_ref):
    first, keep = _head_pair_masks()
    q = q_ref[...]
    k = k_ref[...]
    v = v_ref[...]
    items = [(slice(CTX_LEN * sq, CTX_LEN * (sq + 1)), slice(LANES * t, LANES * (t + 1)), half)
             for sq in range(CTX_SEQS) for t in range(A_WIDTH // LANES) for half in range(2)]
    scores = [_dot_nt(q[rows, tile] * keep[half], k[rows, tile]) for rows, tile, half in items]
    probs = [jnp.exp(s - jnp.max(s, axis=-1, keepdims=True)) for s in scores]
    outs = [_dot(p.astype(BF16), v[rows, tile]) / jnp.sum(p, axis=-1, keepdims=True)
            for p, (rows, tile, half) in zip(probs, items)]
    for (rows, tile, _), o_first, o_second in zip(items[0::2], outs[0::2], outs[1::2]):
        o_ref[rows, tile] = jnp.where(first, o_first, o_second).astype(BF16)


def _ctx_attn(qa, ka, va):
    blk = pl.BlockSpec((CTX_SEQS * CTX_LEN, A_WIDTH), lambda b: (b, 0))
    return pl.pallas_call(
        _ctx_attn_kernel,
        grid=(N_CTX_B // CTX_SEQS,),
        in_specs=[blk, blk, blk],
        out_specs=blk,
        out_shape=jax.ShapeDtypeStruct((N_CTX, A_WIDTH), BF16),
        compiler_params=_params(("parallel",)),
        name="ctx_attn",
    )(qa, ka, va)


def _na_key_row0(j):
    return jnp.clip(NA_RB * j - WIN_H // 2, 0, LAT_ROWS - NA_KR)


def _na_kernel(q_ref, k_ref, v_ref, ck_ref, cv_ref, bias_ref, o_ref):
    j = pl.program_id(0)
    b = pl.program_id(1)
    base = pl.multiple_of(b * LAT_LEN + _na_key_row0(j) * GRID_W, GRID_W)
    kw = k_ref[pl.ds(base, NA_K), :]
    vw = v_ref[pl.ds(base, NA_K), :]
    ck = ck_ref[b].astype(BF16)
    cv = cv_ref[b].astype(BF16)
    q = q_ref[...]
    heads = [slice(HD_A * hh, HD_A * (hh + 1)) for hh in range(H_A)]
    s_win = [_dot_nt(q[:, sl], kw[:, sl]) for sl in heads]
    s_ctx = [_dot_nt(q[:, sl], ck[:, sl]) for sl in heads]
    probs = []
    for hh in range(H_A):
        sw = s_win[hh] + bias_ref[hh]
        sx = s_ctx[hh]
        m = jnp.maximum(jnp.max(sw, axis=-1, keepdims=True), jnp.max(sx, axis=-1, keepdims=True))
        pw = jnp.exp(sw - m)
        px = jnp.exp(sx - m)
        l = jnp.sum(pw, axis=-1, keepdims=True) + jnp.sum(px, axis=-1, keepdims=True)
        probs.append((pw.astype(BF16), px.astype(BF16), l))
    outs = [(_dot(pw, vw[:, sl]) + _dot(px, cv[:, sl])) / l for (pw, px, l), sl in zip(probs, heads)]
    o_ref[...] = jnp.concatenate(outs, axis=1).astype(BF16)


def _na_attn(qa, ka, va, ck, cv, bias, l):
    past = ck.shape[2]
    lat_kv = pl.BlockSpec((N_LAT, A_WIDTH), lambda j, b: (N_CTX // N_LAT, 0), pipeline_mode=pl.Buffered(1))
    cache = pl.BlockSpec((N_LAT_B, None, past, A_WIDTH), lambda j, b: (0, l, 0, 0), pipeline_mode=pl.Buffered(1))
    bias_class = lambda j: jnp.where(j == 0, 0, jnp.where(j == NA_BLOCKS - 1, 2, 1))
    return pl.pallas_call(
        _na_kernel,
        grid=(NA_BLOCKS, N_LAT_B),
        in_specs=[
            pl.BlockSpec((NA_Q, A_WIDTH), lambda j, b: (N_CTX // NA_Q + b * NA_BLOCKS + j, 0)),
            lat_kv, lat_kv, cache, cache,
            pl.BlockSpec((None, None, H_A, NA_Q, NA_K), lambda j, b: (l, bias_class(j), 0, 0, 0)),
        ],
        out_specs=pl.BlockSpec((NA_Q, A_WIDTH), lambda j, b: (b * NA_BLOCKS + j, 0)),
        out_shape=jax.ShapeDtypeStruct((N_LAT, A_WIDTH), BF16),
        compiler_params=_params(("arbitrary", "arbitrary")),
        name="na_attn",
    )(qa, ka, va, ck, cv, bias)


def _na_class_geometry():
    out = []
    for j in (0, 1, NA_BLOCKS - 1):
        r = NA_RB * j + np.arange(NA_RB)
        kr = int(np.clip(NA_RB * j - WIN_H // 2, 0, LAT_ROWS - NA_KR)) + np.arange(NA_KR)
        start_r = np.clip(r - WIN_H // 2, 0, LAT_ROWS - WIN_H)
        row_ok = (kr[None, :] >= start_r[:, None]) & (kr[None, :] < start_r[:, None] + WIN_H)
        dr = kr[None, :] - r[:, None] + (WIN_H - 1)
        out.append((dr, row_ok))
    return out


def _bias_kernel(r_ref, o_ref):
    qc = lax.broadcasted_iota(jnp.int32, (GRID_W, GRID_W), 0)
    kc = lax.broadcasted_iota(jnp.int32, (GRID_W, GRID_W), 1)
    c0 = jnp.clip(qc - WIN_W // 2, 0, GRID_W - WIN_W)
    col_ok = (kc >= c0) & (kc < c0 + WIN_W)
    neg = jnp.full((GRID_W, GRID_W), -1e30, F32)
    toep = []
    for dr in range(2 * WIN_H - 1):
        rows = jnp.broadcast_to(r_ref[0, 0, dr:dr + 1, :], (GRID_W, LANES))
        t = pltpu.roll(rows, 0, 1, stride=1, stride_axis=0)[:, :GRID_W]
        toep.append(jnp.where(col_ok, t, neg))
    for cls, (dr_tab, row_ok) in enumerate(_na_class_geometry()):
        for a in range(NA_RB):
            for jj in range(NA_KR):
                o_ref[0, cls, 0, GRID_W * a:GRID_W * (a + 1), GRID_W * jj:GRID_W * (jj + 1)] = (
                    toep[int(dr_tab[a, jj])] if row_ok[a, jj] else neg)


def _bias_tables(rpb):
    n_dr, n_dc = 2 * WIN_H - 1, 2 * WIN_W - 1
    assert rpb.shape == (DEPTH, H_A, n_dr, n_dc)
    rows = jnp.concatenate([rpb[..., WIN_W - 1:], jnp.zeros((DEPTH, H_A, n_dr, LANES - n_dc), F32),
                            rpb[..., :WIN_W - 1]], axis=-1)
    rows = jnp.pad(rows, ((0, 0), (0, 0), (0, 16 - n_dr), (0, 0)))
    return pl.pallas_call(
        _bias_kernel,
        grid=(DEPTH, H_A),
        in_specs=[pl.BlockSpec((1, 1, 16, LANES), lambda l, h: (l, h, 0, 0))],
        out_specs=pl.BlockSpec((1, 3, 1, NA_Q, NA_K), lambda l, h: (l, 0, h, 0, 0)),
        out_shape=jax.ShapeDtypeStruct((DEPTH, 3, H_A, NA_Q, NA_K), F32),
        compiler_params=_params(("arbitrary", "arbitrary")),
        name="na_bias",
    )(rows)


def _mlstm_kernel(*refs, n_chunks, has_init, want_state, n_alias):
    dirs = (refs[0:4], refs[4:8])
    pos = 8
    if has_init:
        c0_ref, n0_ref, m0_ref = refs[pos:pos + 3]
        pos += 3
    pos += n_alias
    n_h = 1 if n_chunks == 1 else 2
    h_refs = refs[pos:pos + n_h]
    pos += n_h
    if want_state:
        cf_ref, nf_ref, mf_ref = refs[pos:pos + 3]
        pos += 3
    cs_ref, ms_ref = refs[pos:pos + 2]
    c = pl.program_id(1)

    def tail_rows(x):
        r = lax.broadcasted_iota(jnp.int32, (STATE_TAIL, x.shape[1]), 0)
        return jnp.where(r == 0, x, jnp.where(r == 1, x - x.astype(BF16).astype(F32), 0.0))

    @pl.when(c == 0)
    def _():
        if has_init:
            cs_ref[:, 0:HD_B, :] = c0_ref[0]
            for i in range(N_STATE):
                cs_ref[i, HD_B:HD_B + STATE_TAIL, :] = tail_rows(n0_ref[0, i:i + 1, :])
            ms_ref[...] = m0_ref[0]
        else:
            cs_ref[...] = jnp.zeros(cs_ref.shape, F32)
            ms_ref[...] = jnp.zeros(ms_ref.shape, F32)

    row = lax.broadcasted_iota(jnp.int32, (CHUNK, CHUNK), 0)
    col = lax.broadcasted_iota(jnp.int32, (CHUNK, CHUNK), 1)
    chains = []
    for d in range(2):
        q_ref, k_ref, v_ref, g_ref = dirs[d]
        vis = (row <= col) if d == 0 else (row >= col)
        cum_b = jnp.where((col <= row) if d == 0 else (col >= row), 1.0, 0.0).astype(BF16)
        g_t = g_ref[...].T[0:N_GATE, :]
        r = _dot_nt(jnp.concatenate(_split3(g_t), axis=0), cum_b)
        b_t = r[0:N_GATE] + r[N_GATE:2 * N_GATE] + r[2 * N_GATE:3 * N_GATE]
        i0 = 2 * H_B * d
        u_t = g_t[i0:i0 + H_B, :] - b_t[i0 + H_B:i0 + 2 * H_B, :]
        u_cols = jnp.concatenate([u_t, jnp.zeros((LANES - H_B, CHUNK), F32)], axis=0).T
        q = q_ref[...]
        k = k_ref[...]
        v32 = v_ref[...].astype(F32)
        for hh in range(H_B):
            sl = slice(HD_B * hh, HD_B * (hh + 1))
            chains.append(dict(
                d=d, i=d * H_B + hh, sl=sl, vis=vis, qh=q[:, sl], kh=k[:, sl], v32=v32,
                u_col=u_cols[:, hh:hh + 1],
                b_row=b_t[i0 + H_B + hh:i0 + H_B + hh + 1, :],
                i_row=g_t[i0 + hh:i0 + hh + 1, :]))

    for ch in chains:
        ch["state"] = cs_ref[ch["i"]]
        ch["r"] = _dot_nt(jnp.concatenate([ch["kh"], ch["state"].astype(BF16)], axis=0), ch["qh"])
    for ch in chains:
        b_row, m_prev = ch["b_row"], ms_ref[ch["i"]:ch["i"] + 1, 0:1]
        dm = jnp.where(ch["vis"], ch["u_col"] + b_row, -jnp.inf)
        inter = b_row + m_prev
        m_t = jnp.maximum(inter, jnp.max(dm, axis=0, keepdims=True))
        ch["a"] = jnp.exp(inter - m_t)
        ch["w"] = jnp.exp(dm - m_t)
        ch["floor"] = jnp.exp(-m_t)
        b_end = jnp.broadcast_to(b_row[:, CHUNK - 1:CHUNK] if ch["d"] == 0 else b_row[:, 0:1], (1, CHUNK))
        gl = b_end - b_row + ch["i_row"]
        m_new = jnp.maximum(b_end + m_prev, jnp.max(gl, axis=-1, keepdims=True))
        ch["a_s"] = jnp.exp(b_end + m_prev - m_new)[:, 0:HD_B]
        ch["w_s"] = jnp.exp(gl - m_new)
        ch["m_new"] = m_new
    for ch in chains:
        r, a = ch["r"], ch["a"]
        p = r[0:CHUNK] * ch["w"]
        qn = r[CHUNK + HD_B:CHUNK + HD_B + 1] + r[CHUNK + HD_B + 1:CHUNK + HD_B + 2]
        den = a * qn + jnp.sum(p, axis=0, keepdims=True)
        ch["v_t"] = ch["v32"][:, ch["sl"]].T
        num = a * r[CHUNK:CHUNK + HD_B] + _dot(ch["v_t"].astype(BF16), p.astype(BF16))
        h_t = num * (1.0 / jnp.maximum(jnp.abs(den), ch["floor"]))
        if n_h == 2:
            h_refs[ch["d"]][:, ch["sl"]] = h_t.T
        else:
            ch["h_t"] = h_t
    if n_h == 1:
        for fwd_ch, bwd_ch in zip(chains[:H_B], chains[H_B:]):
            h_refs[0][:, fwd_ch["sl"]] = (fwd_ch["h_t"] + bwd_ch["h_t"]).T
    for ch in chains:
        i, a_s, state = ch["i"], ch["a_s"], ch["state"]
        lhs = jnp.concatenate([ch["v_t"] * ch["w_s"], tail_rows(ch["w_s"])], axis=0).astype(BF16)
        upd = _dot(lhs, ch["kh"])
        n_new = a_s * state[HD_B:HD_B + 1] + upd[HD_B:HD_B + 1] + upd[HD_B + 1:HD_B + 2]
        cs_ref[i, 0:HD_B, :] = a_s * state[0:HD_B] + upd[0:HD_B]
        cs_ref[i, HD_B:HD_B + STATE_TAIL, :] = tail_rows(n_new)
        ms_ref[i:i + 1, :] = ch["m_new"][:, 0:LANES]

    if want_state:
        @pl.when(c == n_chunks - 1)
        def _():
            cf_ref[0] = cs_ref[:, 0:HD_B, :]
            for i in range(N_STATE):
                nf_ref[0, i:i + 1, :] = cs_ref[i, HD_B:HD_B + 1, :]
            mf_ref[0] = ms_ref[...]


def _mlstm(qb, kb, vb, gt, row0, n_batch, seq_len, layer, init=None, want_state=False, c_prev=None):
    n_chunks = seq_len // CHUNK
    blk0 = row0 // CHUNK

    def fwd(w):
        return pl.BlockSpec((CHUNK, w), lambda b, c: (blk0 + b * n_chunks + c, 0))

    def bwd(w):
        return pl.BlockSpec((CHUNK, w), lambda b, c: (blk0 + b * n_chunks + n_chunks - 1 - c, 0))

    in_specs = [fwd(B_WIDTH), fwd(B_WIDTH), fwd(B_WIDTH), fwd(LANES),
                bwd(B_WIDTH), bwd(B_WIDTH), bwd(B_WIDTH), bwd(LANES)]
    args = [qb, kb, vb, gt, qb, kb, vb, gt]
    c_spec = pl.BlockSpec((1, None, N_STATE, HD_B, HD_B), lambda b, c: (b, layer, 0, 0, 0))
    if init is not None:
        in_specs += [c_spec,
                     pl.BlockSpec((1, None, N_STATE, LANES), lambda b, c: (b, layer, 0, 0)),
                     pl.BlockSpec((1, None, N_STATE, LANES), lambda b, c: (b, layer, 0, 0))]
        args += list(init)
    n_rows = n_batch * seq_len
    out_specs = [pl.BlockSpec((CHUNK, B_WIDTH), lambda b, c: (b * n_chunks + c, 0)),
                 pl.BlockSpec((CHUNK, B_WIDTH), lambda b, c: (b * n_chunks + n_chunks - 1 - c, 0))]
    if n_chunks == 1:
        out_specs = out_specs[:1]
    out_shape = [jax.ShapeDtypeStruct((n_rows, B_WIDTH), F32)] * len(out_specs)
    aliases = {}
    if c_prev is not None:
        aliases = {len(args): len(out_specs)}
        in_specs += [pl.BlockSpec(memory_space=pl.ANY)]
        args += [c_prev]
    if want_state:
        out_specs += [c_spec,
                      pl.BlockSpec((1, N_STATE, LANES), lambda b, c: (b, 0, 0)),
                      pl.BlockSpec((1, N_STATE, LANES), lambda b, c: (b, 0, 0))]
        out_shape += [jax.ShapeDtypeStruct((n_batch, DEPTH, N_STATE, HD_B, HD_B), F32),
                      jax.ShapeDtypeStruct((n_batch, N_STATE, LANES), F32),
                      jax.ShapeDtypeStruct((n_batch, N_STATE, LANES), F32)]
    return pl.pallas_call(
        functools.partial(_mlstm_kernel, n_chunks=n_chunks, has_init=init is not None, want_state=want_state,
                          n_alias=len(aliases)),
        grid=(n_batch, n_chunks),
        input_output_aliases=aliases,
        in_specs=in_specs,
        out_specs=out_specs,
        out_shape=out_shape,
        scratch_shapes=[pltpu.VMEM((N_STATE, HD_B + STATE_TAIL, HD_B), F32),
                        pltpu.VMEM((N_STATE, LANES), F32)],
        compiler_params=_params(("parallel", "arbitrary")),
        name="mlstm",
    )(*args)


def _mixout_ffn_kernel(*refs, split_out, n_hc, n_hl):
    x_ref, mod_ref, attc_ref, attl_ref = refs[:4]
    hc_refs = refs[4:4 + n_hc]
    hl_refs = refs[4 + n_hc:4 + n_hc + n_hl]
    n_in = 13 + n_hc + n_hl
    (ob_ref, br_ref, g_ref, ghn_ref, wpa_ref, wpb_ref, wo_ref, wup_ref,
     wdn_ref) = refs[4 + n_hc + n_hl:n_in]
    o_refs = refs[n_in:]
    is_ctx = pl.program_id(0) < N_CTX // TM
    att = jnp.where(is_ctx, attc_ref[...], attl_ref[...])
    hsum = jnp.where(is_ctx, sum(r[...] for r in hc_refs), sum(r[...] for r in hl_refs))
    parts = []
    for hh in range(H_B):
        seg = hsum[:, HD_B * hh:HD_B * (hh + 1)]
        parts.append(seg * lax.rsqrt(jnp.mean(seg * seg, axis=-1, keepdims=True) + EPS))
    hn = (jnp.concatenate(parts, axis=1) * ghn_ref[...]) * ob_ref[...].astype(F32)
    y = (br_ref[:, :D_MODEL].astype(F32) * _dot(att, wpa_ref[...])
         + br_ref[:, D_MODEL:].astype(F32) * _dot(hn.astype(BF16), wpb_ref[...]))
    x = x_ref[...] + _mod_row(mod_ref, 5) * _dot(y.astype(BF16), wo_ref[...])
    out = _swiglu_sub(x, g_ref[2:3, :], _mod_row(mod_ref, 6), _mod_row(mod_ref, 7), _mod_row(mod_ref, 8),
                      wup_ref, wdn_ref)
    _write_stream(o_refs, out, is_ctx, split_out)


def _mixout_ffn(x, mod, att_c, att_l, h_c, h_l, sob, sbr, g_norm, g_hn, w_pa, w_pb, w_o,
                w_up, w_dn, l, split_out=False):
    tok, ctx, lat = _tok_spec, _ctx_spec, _lat_spec
    if split_out:
        out_specs = [ctx(D_MODEL), lat(D_MODEL)]
        out_shape = [jax.ShapeDtypeStruct((N_CTX, D_MODEL), F32), jax.ShapeDtypeStruct((N_LAT, D_MODEL), F32)]
    else:
        out_specs = tok(D_MODEL)
        out_shape = jax.ShapeDtypeStruct((N_TOK, D_MODEL), F32)
    return pl.pallas_call(
        functools.partial(_mixout_ffn_kernel, split_out=split_out, n_hc=len(h_c), n_hl=len(h_l)),
        grid=(N_TOK // TM,),
        in_specs=[
            tok(D_MODEL),
            _layer((MOD_ROWS, N_MOD * D_MODEL), l),
            ctx(A_WIDTH), lat(A_WIDTH),
        ] + [ctx(B_WIDTH)] * len(h_c) + [lat(B_WIDTH)] * len(h_l) + [
            tok(B_WIDTH), tok(2 * D_MODEL),
            _layer((3, D_MODEL), l),
            _layer((1, B_WIDTH), l),
            _layer((A_WIDTH, D_MODEL), l),
            _layer((B_WIDTH, D_MODEL), l),
            _layer((D_MODEL, D_MODEL), l),
            _layer((D_MODEL, 2 * D_FF), l),
            _layer((D_FF, D_MODEL), l),
        ],
        out_specs=out_specs,
        out_shape=out_shape,
        compiler_params=_params(("arbitrary",)),
        name="mixout_ffn",
    )(x, mod, att_c, att_l, *h_c, *h_l, sob, sbr, g_norm, g_hn, w_pa, w_pb, w_o, w_up, w_dn)


def _rope_tables():
    half = HD_B // 2
    nf = half // 2
    inv = ROPE_THETA ** (-jnp.arange(nf, dtype=F32) / nf)
    t = jnp.arange(LAT_LEN)
    rowp = (t // GRID_W).astype(F32)
    colp = (t % GRID_W).astype(F32)
    ang = jnp.concatenate([rowp[:, None] * inv, colp[:, None] * inv], axis=-1)
    cos, sin = jnp.cos(ang), jnp.sin(ang)
    cos_l = jnp.tile(jnp.concatenate([cos, cos], axis=-1), (N_LAT_B, 1))
    sin_l = jnp.tile(jnp.concatenate([-sin, sin], axis=-1), (N_LAT_B, 1))
    cosf = jnp.concatenate([jnp.ones((N_CTX, HD_B), F32), cos_l], axis=0)
    sinf = jnp.concatenate([jnp.zeros((N_CTX, HD_B), F32), sin_l], axis=0)
    return cosf, sinf


def kernel(x_prompt, x_sample, cache_k, cache_v, state_C, state_n, state_m, c, c_ctx, w_ada, b_ada, g_norm,
           w_up1, w_dn1, w_in, b_gates, g_qn, g_kn, rpb, g_hn, w_pa, w_pb, w_o, w_up2, w_dn2):
    assert x_prompt.shape == (N_CTX_B, CTX_LEN, D_MODEL) and x_sample.shape == (N_LAT_B, LAT_LEN, D_MODEL)
    past_len = cache_k.shape[2]

    w_a = w_in[..., :QKV_COLS].astype(BF16)
    w_br = w_in[..., QKV_COLS + N_GATE:].astype(BF16)
    w_gt = jnp.pad(w_in[..., QKV_COLS:QKV_COLS + N_GATE], ((0, 0), (0, 0), (0, LANES - N_GATE))).astype(BF16)
    b_gt = jnp.pad(b_gates, ((0, 0), (0, LANES - N_GATE))).reshape(DEPTH, 1, LANES)
    w_pab, w_pbb, w_ob, w_up2b, w_dn2b = (w.astype(BF16) for w in (w_pa, w_pb, w_o, w_up2, w_dn2))
    gq = jnp.tile(g_qn, (1, H_A)).reshape(DEPTH, 1, A_WIDTH)
    gk = jnp.tile(g_kn, (1, H_A)).reshape(DEPTH, 1, A_WIDTH)
    ghn = g_hn.reshape(DEPTH, 1, B_WIDTH)
    seg_np = (np.arange(A_WIDTH)[:, None] // HD_A == np.arange(LANES)[None, :]).astype(np.float32)
    seg = jnp.asarray(seg_np, BF16)
    segt = jnp.asarray(seg_np.T, BF16)
    cosf, sinf = _rope_tables()
    bias = _bias_tables(rpb)
    ck = cache_k.reshape(N_LAT_B, DEPTH, past_len, A_WIDTH)
    cv = cache_v.reshape(N_LAT_B, DEPTH, past_len, A_WIDTH)
    init = (state_C.reshape(N_LAT_B, DEPTH, N_STATE, HD_B, HD_B),
            state_n.reshape(N_LAT_B, DEPTH, N_STATE, HD_B),
            jnp.broadcast_to(state_m.reshape(N_LAT_B, DEPTH, N_STATE, 1), (N_LAT_B, DEPTH, N_STATE, LANES)))

    cond = jnp.concatenate([c_ctx[None], c, jnp.zeros((MOD_ROWS - 1 - N_LAT_B, D_MODEL), F32)], axis=0)
    mod = _ada(cond, w_ada, b_ada)

    x = (x_prompt.reshape(N_CTX, D_MODEL), x_sample.reshape(N_LAT, D_MODEL))
    caches = tuple(jnp.zeros((N_CTX_B, DEPTH, CTX_LEN, A_WIDTH), F32) for _ in range(2))
    c_new = jnp.zeros((N_CTX_B, DEPTH, N_STATE, HD_B, HD_B), F32)
    ns_out, ms_out = [], []
    for l in range(DEPTH):
        x = _ffn(x, mod, g_norm, w_up1, w_dn1, l, 0)
        qa, ka, va, k_new, v_new, qb, kb, vb, sob, gt, sbr = _mixin(
            x, mod, g_norm, w_a, w_br, w_gt, b_gt, gq, gk, seg, segt, cosf, sinf, l, caches)
        caches = (k_new, v_new)
        att_c = _ctx_attn(qa, ka, va)
        att_l = _na_attn(qa, ka, va, ck, cv, bias, l)
        *h_c, c_new, n_fin, m_fin = _mlstm(qb, kb, vb, gt, 0, N_CTX_B, CTX_LEN, l, want_state=True, c_prev=c_new)
        h_l = _mlstm(qb, kb, vb, gt, N_CTX, N_LAT_B, LAT_LEN, l, init=init)
        x = _mixout_ffn(x, mod, att_c, att_l, h_c, h_l, sob, sbr, g_norm, ghn, w_pab, w_pbb, w_ob,
                        w_up2b, w_dn2b, l, split_out=(l == DEPTH - 1))
        ns_out.append(n_fin.reshape(N_CTX_B, 2, H_B, HD_B))
        ms_out.append(m_fin[:, :, 0].reshape(N_CTX_B, 2, H_B))

    return (x[0].reshape(N_CTX_B, CTX_LEN, D_MODEL),
            x[1].reshape(N_LAT_B, LAT_LEN, D_MODEL),
            k_new.reshape(N_CTX_B, DEPTH, CTX_LEN, H_A, HD_A),
            v_new.reshape(N_CTX_B, DEPTH, CTX_LEN, H_A, HD_A),
            c_new.reshape(N_CTX_B, DEPTH, 2, H_B, HD_B, HD_B),
            jnp.stack(ns_out, axis=1), jnp.stack(ms_out, axis=1))
```

```python
import functools

import jax
import jax.numpy as jnp
import numpy as np
from jax import lax
from jax.experimental import pallas as pl
from jax.experimental.pallas import tpu as pltpu

F32 = jnp.float32
BF16 = jnp.bfloat16

D_MODEL = 1024
DEPTH = 4
N_CTX_B, CTX_LEN = 32, 256
N_LAT_B, LAT_LEN = 2, 2048
N_CTX = N_CTX_B * CTX_LEN
N_LAT = N_LAT_B * LAT_LEN
N_TOK = N_CTX + N_LAT
GRID_W, WIN_H, WIN_W = 64, 8, 16
LAT_ROWS = LAT_LEN // GRID_W
H_A, HD_A = 8, 64
H_B, HD_B = 4, 128
A_WIDTH = H_A * HD_A
B_WIDTH = H_B * HD_B
D_FF = 2816
N_MOD = 9
CHUNK = 256
ROPE_THETA = 10000.0
EPS = 1e-6
KSCALE = HD_B ** -0.5
ASCALE = HD_A ** -0.5
N_GATE = 4 * H_B
QKV_COLS = 3 * A_WIDTH + 4 * B_WIDTH
N_STATE = 2 * H_B
STATE_TAIL = 16
MOD_ROWS = 8

LANES = 128
TM = 512
TM_FFN = 512
FF_CHUNK = 256
VMEM_LIMIT = 52 * 1024 * 1024

NA_RB = WIN_H // 2
NA_KR = NA_RB + WIN_H - 1
NA_Q = NA_RB * GRID_W
NA_K = NA_KR * GRID_W
NA_BLOCKS = LAT_ROWS // NA_RB
assert LAT_ROWS % NA_RB == 0 and NA_BLOCKS >= 3 and N_CTX % N_LAT == 0 and 1 + N_LAT_B <= MOD_ROWS


def _dot(a, b):
    return jnp.dot(a, b, preferred_element_type=F32)


def _dot_nt(a, b):
    return lax.dot_general(a, b, (((1,), (1,)), ((), ())), preferred_element_type=F32)


def _split2(x):
    hi = x.astype(BF16)
    lo = (x - hi.astype(F32)).astype(BF16)
    return hi, lo


def _split3(x):
    hi = x.astype(BF16)
    r1 = x - hi.astype(F32)
    mid = r1.astype(BF16)
    lo = (r1 - mid.astype(F32)).astype(BF16)
    return hi, mid, lo


def _norm_mod(x, g, shift, scale):
    ms = jnp.mean(x * x, axis=-1, keepdims=True)
    return (x * lax.rsqrt(ms + EPS) * g) * (1.0 + scale) + shift


def _params(sem):
    return pltpu.CompilerParams(dimension_semantics=sem, vmem_limit_bytes=VMEM_LIMIT)


def _resident(shape):
    nd = len(shape)
    return pl.BlockSpec(shape, lambda *_: (0,) * nd, pipeline_mode=pl.Buffered(1))


def _layer(shape, l):
    nd = len(shape)
    return pl.BlockSpec((None,) + tuple(shape), lambda *_: (l,) + (0,) * nd, pipeline_mode=pl.Buffered(1))


def _tok_spec(w, tm=TM):
    return pl.BlockSpec((tm, w), lambda i: (i, 0))


def _ctx_spec(w, tm=TM):
    return pl.BlockSpec((tm, w), lambda i: (jnp.minimum(i, N_CTX // tm - 1), 0))


def _lat_spec(w, tm=TM):
    return pl.BlockSpec((tm, w), lambda i: (jnp.maximum(i - N_CTX // tm, 0), 0))


def _mod_row(mod_ref, k, tm=TM):
    i = pl.program_id(0)
    n_ctx_tiles = N_CTX // tm
    mod_set = jnp.where(i < n_ctx_tiles, 0, 1 + (i - n_ctx_tiles) // (LAT_LEN // tm))
    return mod_ref[pl.ds(mod_set, 1), D_MODEL * k:D_MODEL * (k + 1)]


def _ada_kernel(c_ref, w_ref, b_ref, o_ref):
    c = c_ref[...]
    s = c * jax.nn.sigmoid(c)
    sh, sl = _split2(s)
    wh, wl = _split2(w_ref[0])
    o_ref[0] = _dot(sh, wh) + _dot(sl, wh) + _dot(sh, wl) + b_ref[0]


def _ada(cond, w_ada, b_ada):
    tn = 1024
    n_out = N_MOD * D_MODEL
    return pl.pallas_call(
        _ada_kernel,
        grid=(DEPTH, n_out // tn),
        in_specs=[
            pl.BlockSpec((MOD_ROWS, D_MODEL), lambda l, j: (0, 0)),
            pl.BlockSpec((1, D_MODEL, tn), lambda l, j: (l, 0, j)),
            pl.BlockSpec((1, 1, tn), lambda l, j: (l, 0, j)),
        ],
        out_specs=pl.BlockSpec((1, MOD_ROWS, tn), lambda l, j: (l, 0, j)),
        out_shape=jax.ShapeDtypeStruct((DEPTH, MOD_ROWS, n_out), F32),
        compiler_params=_params(("parallel", "parallel")),
        name="ada",
    )(cond, w_ada, b_ada.reshape(DEPTH, 1, n_out))


N_FF_CHUNKS = D_FF // FF_CHUNK


FEED_SLOTS = 2


def _ffn_weight_scratch():
    return [pltpu.VMEM((D_MODEL, 2 * D_FF), BF16),
            pltpu.VMEM((D_FF, D_MODEL), BF16),
            pltpu.VMEM((FEED_SLOTS, D_MODEL, FF_CHUNK), F32),
            pltpu.VMEM((FEED_SLOTS, D_MODEL, FF_CHUNK), F32),
            pltpu.VMEM((FEED_SLOTS, FF_CHUNK, D_MODEL), F32),
            pltpu.SemaphoreType.DMA((3, FEED_SLOTS))]


def _ffn_weight_feed(wup_hbm, wdn_hbm, wup_ref, wdn_ref, st_a, st_u, st_d, sems):
    def copies(j):
        slot, c0 = j % FEED_SLOTS, j * FF_CHUNK
        return (pltpu.make_async_copy(wup_hbm.at[:, pl.ds(c0, FF_CHUNK)], st_a.at[slot], sems.at[0, slot]),
                pltpu.make_async_copy(wup_hbm.at[:, pl.ds(D_FF + c0, FF_CHUNK)], st_u.at[slot], sems.at[1, slot]),
                pltpu.make_async_copy(wdn_hbm.at[pl.ds(c0, FF_CHUNK), :], st_d.at[slot], sems.at[2, slot]))

    def before_chunk(j):
        if j == 0:
            for jj in range(min(FEED_SLOTS, N_FF_CHUNKS)):
                for cp in copies(jj):
                    cp.start()
        for cp in copies(j):
            cp.wait()
        slot, c0 = j % FEED_SLOTS, j * FF_CHUNK
        wup_ref[:, c0:c0 + FF_CHUNK] = st_a[slot].astype(BF16)
        wup_ref[:, D_FF + c0:D_FF + c0 + FF_CHUNK] = st_u[slot].astype(BF16)
        wdn_ref[c0:c0 + FF_CHUNK, :] = st_d[slot].astype(BF16)
        if j + FEED_SLOTS < N_FF_CHUNKS:
            for cp in copies(j + FEED_SLOTS):
                cp.start()

    return before_chunk


def _swiglu_sub(x, g, shift, scale, gate, wup_ref, wdn_ref, before_chunk=None):
    h = _norm_mod(x, g, shift, scale).astype(BF16)
    acc = jnp.zeros(x.shape, F32)
    for j in range(N_FF_CHUNKS):
        c0 = j * FF_CHUNK
        if before_chunk is not None:
            before_chunk(j)
        a = _dot(h, wup_ref[:, c0:c0 + FF_CHUNK])
        u = _dot(h, wup_ref[:, D_FF + c0:D_FF + c0 + FF_CHUNK])
        g_act = (a * jax.nn.sigmoid(a) * u).astype(BF16)
        acc = acc + _dot(g_act, wdn_ref[c0:c0 + FF_CHUNK, :])
    return x + 0.5 * gate * acc


def _write_stream(o_refs, y, is_ctx, split_out):
    if split_out:
        @pl.when(is_ctx)
        def _():
            o_refs[0][...] = y

        @pl.when(jnp.logical_not(is_ctx))
        def _():
            o_refs[1][...] = y
    else:
        o_refs[0][...] = y


def _ffn_kernel(*refs, layer, sub, split_in, split_out):
    n_x = 2 if split_in else 1
    mod_ref, g_ref, wup_hbm, wdn_hbm = refs[n_x:n_x + 4]
    n_out = 2 if split_out else 1
    o_refs = refs[n_x + 4:n_x + 4 + n_out]
    wup_ref, wdn_ref, *stage = refs[n_x + 4 + n_out:]
    is_ctx = pl.program_id(0) < N_CTX // TM_FFN
    mod_row = functools.partial(_mod_row, mod_ref, tm=TM_FFN)

    def run(before_chunk):
        x = jnp.where(is_ctx, refs[0][...], refs[1][...]) if split_in else refs[0][...]
        y = _swiglu_sub(x, g_ref[sub:sub + 1, :], mod_row(3 * sub), mod_row(3 * sub + 1), mod_row(3 * sub + 2),
                        wup_ref, wdn_ref, before_chunk)
        _write_stream(o_refs, y, is_ctx, split_out)

    @pl.when(pl.program_id(0) == 0)
    def _():
        run(_ffn_weight_feed(wup_hbm.at[layer], wdn_hbm.at[layer], wup_ref, wdn_ref, *stage))

    @pl.when(pl.program_id(0) != 0)
    def _():
        run(None)


def _ffn(xs, mod, g_norm, w_up, w_dn, l, sub, split_out=False):
    split_in = isinstance(xs, tuple)
    xs = xs if split_in else (xs,)
    tm = TM_FFN
    x_specs = [_ctx_spec(D_MODEL, tm), _lat_spec(D_MODEL, tm)] if split_in else [_tok_spec(D_MODEL, tm)]
    if split_out:
        out_specs = [_ctx_spec(D_MODEL, tm), _lat_spec(D_MODEL, tm)]
        out_shape = [jax.ShapeDtypeStruct((N_CTX, D_MODEL), F32), jax.ShapeDtypeStruct((N_LAT, D_MODEL), F32)]
    else:
        out_specs = _tok_spec(D_MODEL, tm)
        out_shape = jax.ShapeDtypeStruct((N_TOK, D_MODEL), F32)
    return pl.pallas_call(
        functools.partial(_ffn_kernel, layer=l, sub=sub, split_in=split_in, split_out=split_out),
        grid=(N_TOK // tm,),
        in_specs=x_specs + [
            _layer((MOD_ROWS, N_MOD * D_MODEL), l),
            _layer((3, D_MODEL), l),
            pl.BlockSpec(memory_space=pl.ANY),
            pl.BlockSpec(memory_space=pl.ANY),
        ],
        out_specs=out_specs,
        out_shape=out_shape,
        scratch_shapes=_ffn_weight_scratch(),
        compiler_params=_params(("arbitrary",)),
        name="ffn",
    )(*xs, mod, g_norm, w_up, w_dn)


N_MIXIN_IN = 12
N_MIXIN_OUT = 11


def _mixin_kernel(*refs, n_alias):
    (x_ref, mod_ref, g_ref, wa_ref, wtail_ref, bg_ref, gq_ref, gk_ref, seg_ref, segt_ref,
     cos_ref, sin_ref) = refs[:N_MIXIN_IN]
    (qa_ref, ka_ref, va_ref, kc_ref, vc_ref, qb_ref, kb_ref, vb_ref, ob_ref, gt_ref,
     br_ref) = refs[N_MIXIN_IN + n_alias:N_MIXIN_IN + n_alias + N_MIXIN_OUT]
    wbr_ref = refs[N_MIXIN_IN + n_alias + N_MIXIN_OUT]
    is_ctx = pl.program_id(0) < N_CTX // TM

    @pl.when(pl.program_id(0) == 0)
    def _():
        wbr_ref[...] = wtail_ref[:, N_GATE:N_GATE + 2 * D_MODEL]

    x = x_ref[...]
    h = _norm_mod(x, g_ref[1:2, :], _mod_row(mod_ref, 3), _mod_row(mod_ref, 4)).astype(BF16)

    def proj(c0, c1):
        return _dot(h, wa_ref[:, c0:c1])

    def head_sums(p):
        hi, lo = _split2(p * p)
        return _dot(hi, seg_ref[...]) + _dot(lo, seg_ref[...])

    def head_scale(ssum):
        rh, rl = _split2(lax.rsqrt(ssum * (1.0 / HD_A) + EPS))
        return _dot(rh, segt_ref[...]) + _dot(rl, segt_ref[...])

    cosf = cos_ref[...]
    sinf = sin_ref[...]

    def rope(p):
        parts = []
        for hh in range(H_B):
            ph = p[:, HD_B * hh:HD_B * (hh + 1)]
            parts.append(ph * cosf + pltpu.roll(ph, HD_B // 2, 1) * sinf)
        return jnp.concatenate(parts, axis=1)

    cols = [0]
    for w in (A_WIDTH,) * 3 + (B_WIDTH,) * 4:
        cols.append(cols[-1] + w)
    p_qa = proj(cols[0], cols[1])
    p_ka = proj(cols[1], cols[2])
    s_qa = head_sums(p_qa)
    s_ka = head_sums(p_ka)
    va = proj(cols[2], cols[3])
    p_qb = proj(cols[3], cols[4])
    r_qa = head_scale(s_qa)
    r_ka = head_scale(s_ka)
    p_kb = proj(cols[4], cols[5])
    p_vb = proj(cols[5], cols[6])
    p_ob = proj(cols[6], cols[7])

    qa_ref[...] = (p_qa * r_qa * gq_ref[...] * ASCALE).astype(BF16)
    ka = p_ka * r_ka * gk_ref[...]
    ka_ref[...] = ka.astype(BF16)
    va_ref[...] = va.astype(BF16)

    @pl.when(is_ctx)
    def _():
        kc_ref[...] = ka.reshape(kc_ref.shape)
        vc_ref[...] = va.reshape(vc_ref.shape)

    qb_ref[...] = rope(p_qb).astype(BF16)
    kb_ref[...] = (rope(p_kb) * KSCALE).astype(BF16)
    vb_ref[...] = p_vb.astype(BF16)
    ob_ref[...] = jax.nn.sigmoid(p_ob).astype(BF16)
    for j in range(2 * D_MODEL // 512):
        sl = slice(512 * j, 512 * (j + 1))
        br_ref[:, sl] = jax.nn.sigmoid(_dot(h, wbr_ref[:, sl])).astype(BF16)

    pg = _dot(h, wtail_ref[:, 0:LANES]) + bg_ref[...]
    ls = jnp.minimum(pg, 0.0) - jnp.log1p(jnp.exp(-jnp.abs(pg)))
    lane = lax.broadcasted_iota(jnp.int32, pg.shape, 1)
    is_f = ((lane >= H_B) & (lane < 2 * H_B)) | ((lane >= 3 * H_B) & (lane < 4 * H_B))
    gt_ref[...] = jnp.where(is_f, ls, pg)


def _mixin(x, mod, g_norm, w_a, w_tail, b_gt, gq, gk, seg, segt, cosf, sinf, l, caches=None):
    tail_cols = w_tail.shape[-1]
    tok = _tok_spec
    out = lambda w, dt: jax.ShapeDtypeStruct((N_TOK, w), dt)
    seqs = TM // CTX_LEN
    ctx_out = jax.ShapeDtypeStruct((N_CTX_B, DEPTH, CTX_LEN, A_WIDTH), F32)
    cache_spec = pl.BlockSpec((seqs, None, CTX_LEN, A_WIDTH),
                              lambda i: (jnp.minimum(i, N_CTX // TM - 1), l, 0, 0))
    caches = () if caches is None else tuple(caches)
    return pl.pallas_call(
        functools.partial(_mixin_kernel, n_alias=len(caches)),
        grid=(N_TOK // TM,),
        input_output_aliases={N_MIXIN_IN + j: 3 + j for j in range(len(caches))},
        in_specs=[
            tok(D_MODEL),
            _layer((MOD_ROWS, N_MOD * D_MODEL), l),
            _layer((3, D_MODEL), l),
            _layer((D_MODEL, QKV_COLS), l),
            _layer((D_MODEL, tail_cols), l),
            _layer((1, LANES), l),
            _layer((1, A_WIDTH), l),
            _layer((1, A_WIDTH), l),
            _resident((A_WIDTH, LANES)),
            _resident((LANES, A_WIDTH)),
            tok(HD_B),
            tok(HD_B),
        ] + [pl.BlockSpec(memory_space=pl.ANY)] * len(caches),
        out_specs=[tok(A_WIDTH), tok(A_WIDTH), tok(A_WIDTH), cache_spec, cache_spec,
                   tok(B_WIDTH), tok(B_WIDTH), tok(B_WIDTH), tok(B_WIDTH), tok(LANES), tok(2 * D_MODEL)],
        out_shape=[out(A_WIDTH, BF16), out(A_WIDTH, BF16), out(A_WIDTH, BF16), ctx_out, ctx_out,
                   out(B_WIDTH, BF16), out(B_WIDTH, BF16), out(B_WIDTH, BF16),
                   out(B_WIDTH, BF16), out(LANES, F32), out(2 * D_MODEL, BF16)],
        scratch_shapes=[pltpu.VMEM((D_MODEL, 2 * D_MODEL), BF16)],
        compiler_params=_params(("arbitrary",)),
        name="mixin",
    )(x, mod, g_norm, w_a, w_tail, b_gt, gq, gk, seg, segt, cosf, sinf, *caches)


CTX_SEQS = 2


def _head_pair_masks():
    lane = lax.broadcasted_iota(jnp.int32, (1, LANES), 1)
    first = lane < HD_A
    return first, [jnp.where(first, 1.0, 0.0).astype(BF16), jnp.where(first, 0.0, 1.0).astype(BF16)]


def _ctx_attn_kernel(q_ref, k_ref, v_ref, o_ref):
    first, keep = _head_pair_masks()
    q = q_ref[...]
    k = k_ref[...]
    v = v_ref[...]
    items = [(slice(CTX_LEN * sq, CTX_LEN * (sq + 1)), slice(LANES * t, LANES * (t + 1)), half)
             for sq in range(CTX_SEQS) for t in range(A_WIDTH // LANES) for half in range(2)]
    scores = [_dot_nt(q[rows, tile] * keep[half], k[rows, tile]) for rows, tile, half in items]
    probs = [jnp.exp(s - jnp.max(s, axis=-1, keepdims=True)) for s in scores]
    outs = [_dot(p.astype(BF16), v[rows, tile]) / jnp.sum(p, axis=-1, keepdims=True)
            for p, (rows, tile, half) in zip(probs, items)]
    for (rows, tile, _), o_first, o_second in zip(items[0::2], outs[0::2], outs[1::2]):
        o_ref[rows, tile] = jnp.where(first, o_first, o_second).astype(BF16)


def _ctx_attn(qa, ka, va):
    blk = pl.BlockSpec((CTX_SEQS * CTX_LEN, A_WIDTH), lambda b: (b, 0))
    return pl.pallas_call(
        _ctx_attn_kernel,
        grid=(N_CTX_B // CTX_SEQS,),
        in_specs=[blk, blk, blk],
        out_specs=blk,
        out_shape=jax.ShapeDtypeStruct((N_CTX, A_WIDTH), BF16),
        compiler_params=_params(("parallel",)),
        name="ctx_attn",
    )(qa, ka, va)


def _na_key_row0(j):
    return jnp.clip(NA_RB * j - WIN_H // 2, 0, LAT_ROWS - NA_KR)


def _na_kernel(q_ref, k_ref, v_ref, ck_ref, cv_ref, bias_ref, o_ref):
    j = pl.program_id(0)
    b = pl.program_id(1)
    base = pl.multiple_of(b * LAT_LEN + _na_key_row0(j) * GRID_W, GRID_W)
    kw = k_ref[pl.ds(base, NA_K), :]
    vw = v_ref[pl.ds(base, NA_K), :]
    ck = ck_ref[b].astype(BF16)
    cv = cv_ref[b].astype(BF16)
    q = q_ref[...]
    heads = [slice(HD_A * hh, HD_A * (hh + 1)) for hh in range(H_A)]
    s_win = [_dot_nt(q[:, sl], kw[:, sl]) for sl in heads]
    s_ctx = [_dot_nt(q[:, sl], ck[:, sl]) for sl in heads]
    probs = []
    for hh in range(H_A):
        sw = s_win[hh] + bias_ref[hh]
        sx = s_ctx[hh]
        m = jnp.maximum(jnp.max(sw, axis=-1, keepdims=True), jnp.max(sx, axis=-1, keepdims=True))
        pw = jnp.exp(sw - m)
        px = jnp.exp(sx - m)
        l = jnp.sum(pw, axis=-1, keepdims=True) + jnp.sum(px, axis=-1, keepdims=True)
        probs.append((pw.astype(BF16), px.astype(BF16), l))
    outs = [(_dot(pw, vw[:, sl]) + _dot(px, cv[:, sl])) / l for (pw, px, l), sl in zip(probs, heads)]
    o_ref[...] = jnp.concatenate(outs, axis=1).astype(BF16)


def _na_attn(qa, ka, va, ck, cv, bias, l):
    past = ck.shape[2]
    lat_kv = pl.BlockSpec((N_LAT, A_WIDTH), lambda j, b: (N_CTX // N_LAT, 0), pipeline_mode=pl.Buffered(1))
    cache = pl.BlockSpec((N_LAT_B, None, past, A_WIDTH), lambda j, b: (0, l, 0, 0), pipeline_mode=pl.Buffered(1))
    bias_class = lambda j: jnp.where(j == 0, 0, jnp.where(j == NA_BLOCKS - 1, 2, 1))
    return pl.pallas_call(
        _na_kernel,
        grid=(NA_BLOCKS, N_LAT_B),
        in_specs=[
            pl.BlockSpec((NA_Q, A_WIDTH), lambda j, b: (N_CTX // NA_Q + b * NA_BLOCKS + j, 0)),
            lat_kv, lat_kv, cache, cache,
            pl.BlockSpec((None, None, H_A, NA_Q, NA_K), lambda j, b: (l, bias_class(j), 0, 0, 0)),
        ],
        out_specs=pl.BlockSpec((NA_Q, A_WIDTH), lambda j, b: (b * NA_BLOCKS + j, 0)),
        out_shape=jax.ShapeDtypeStruct((N_LAT, A_WIDTH), BF16),
        compiler_params=_params(("arbitrary", "arbitrary")),
        name="na_attn",
    )(qa, ka, va, ck, cv, bias)


def _na_class_geometry():
    out = []
    for j in (0, 1, NA_BLOCKS - 1):
        r = NA_RB * j + np.arange(NA_RB)
        kr = int(np.clip(NA_RB * j - WIN_H // 2, 0, LAT_ROWS - NA_KR)) + np.arange(NA_KR)
        start_r = np.clip(r - WIN_H // 2, 0, LAT_ROWS - WIN_H)
        row_ok = (kr[None, :] >= start_r[:, None]) & (kr[None, :] < start_r[:, None] + WIN_H)
        dr = kr[None, :] - r[:, None] + (WIN_H - 1)
        out.append((dr, row_ok))
    return out


def _bias_kernel(r_ref, o_ref):
    qc = lax.broadcasted_iota(jnp.int32, (GRID_W, GRID_W), 0)
    kc = lax.broadcasted_iota(jnp.int32, (GRID_W, GRID_W), 1)
    c0 = jnp.clip(qc - WIN_W // 2, 0, GRID_W - WIN_W)
    col_ok = (kc >= c0) & (kc < c0 + WIN_W)
    neg = jnp.full((GRID_W, GRID_W), -1e30, F32)
    toep = []
    for dr in range(2 * WIN_H - 1):
        rows = jnp.broadcast_to(r_ref[0, 0, dr:dr + 1, :], (GRID_W, LANES))
        t = pltpu.roll(rows, 0, 1, stride=1, stride_axis=0)[:, :GRID_W]
        toep.append(jnp.where(col_ok, t, neg))
    for cls, (dr_tab, row_ok) in enumerate(_na_class_geometry()):
        for a in range(NA_RB):
            for jj in range(NA_KR):
                o_ref[0, cls, 0, GRID_W * a:GRID_W * (a + 1), GRID_W * jj:GRID_W * (jj + 1)] = (
                    toep[int(dr_tab[a, jj])] if row_ok[a, jj] else neg)


def _bias_tables(rpb):
    n_dr, n_dc = 2 * WIN_H - 1, 2 * WIN_W - 1
    assert rpb.shape == (DEPTH, H_A, n_dr, n_dc)
    rows = jnp.concatenate([rpb[..., WIN_W - 1:], jnp.zeros((DEPTH, H_A, n_dr, LANES - n_dc), F32),
                            rpb[..., :WIN_W - 1]], axis=-1)
    rows = jnp.pad(rows, ((0, 0), (0, 0), (0, 16 - n_dr), (0, 0)))
    return pl.pallas_call(
        _bias_kernel,
        grid=(DEPTH, H_A),
        in_specs=[pl.BlockSpec((1, 1, 16, LANES), lambda l, h: (l, h, 0, 0))],
        out_specs=pl.BlockSpec((1, 3, 1, NA_Q, NA_K), lambda l, h: (l, 0, h, 0, 0)),
        out_shape=jax.ShapeDtypeStruct((DEPTH, 3, H_A, NA_Q, NA_K), F32),
        compiler_params=_params(("arbitrary", "arbitrary")),
        name="na_bias",
    )(rows)


def _mlstm_kernel(*refs, n_chunks, has_init, want_state, n_alias):
    dirs = (refs[0:4], refs[4:8])
    pos = 8
    if has_init:
        c0_ref, n0_ref, m0_ref = refs[pos:pos + 3]
        pos += 3
    pos += n_alias
    n_h = 1 if n_chunks == 1 else 2
    h_refs = refs[pos:pos + n_h]
    pos += n_h
    if want_state:
        cf_ref, nf_ref, mf_ref = refs[pos:pos + 3]
        pos += 3
    cs_ref, ms_ref = refs[pos:pos + 2]
    c = pl.program_id(1)

    def tail_rows(x):
        r = lax.broadcasted_iota(jnp.int32, (STATE_TAIL, x.shape[1]), 0)
        return jnp.where(r == 0, x, jnp.where(r == 1, x - x.astype(BF16).astype(F32), 0.0))

    @pl.when(c == 0)
    def _():
        if has_init:
            cs_ref[:, 0:HD_B, :] = c0_ref[0]
            for i in range(N_STATE):
                cs_ref[i, HD_B:HD_B + STATE_TAIL, :] = tail_rows(n0_ref[0, i:i + 1, :])
            ms_ref[...] = m0_ref[0]
        else:
            cs_ref[...] = jnp.zeros(cs_ref.shape, F32)
            ms_ref[...] = jnp.zeros(ms_ref.shape, F32)

    row = lax.broadcasted_iota(jnp.int32, (CHUNK, CHUNK), 0)
    col = lax.broadcasted_iota(jnp.int32, (CHUNK, CHUNK), 1)
    chains = []
    for d in range(2):
        q_ref, k_ref, v_ref, g_ref = dirs[d]
        vis = (row <= col) if d == 0 else (row >= col)
        cum_b = jnp.where((col <= row) if d == 0 else (col >= row), 1.0, 0.0).astype(BF16)
        g_t = g_ref[...].T[0:N_GATE, :]
        r = _dot_nt(jnp.concatenate(_split3(g_t), axis=0), cum_b)
        b_t = r[0:N_GATE] + r[N_GATE:2 * N_GATE] + r[2 * N_GATE:3 * N_GATE]
        i0 = 2 * H_B * d
        u_t = g_t[i0:i0 + H_B, :] - b_t[i0 + H_B:i0 + 2 * H_B, :]
        u_cols = jnp.concatenate([u_t, jnp.zeros((LANES - H_B, CHUNK), F32)], axis=0).T
        q = q_ref[...]
        k = k_ref[...]
        v32 = v_ref[...].astype(F32)
        for hh in range(H_B):
            sl = slice(HD_B * hh, HD_B * (hh + 1))
            chains.append(dict(
                d=d, i=d * H_B + hh, sl=sl, vis=vis, qh=q[:, sl], kh=k[:, sl], v32=v32,
                u_col=u_cols[:, hh:hh + 1],
                b_row=b_t[i0 + H_B + hh:i0 + H_B + hh + 1, :],
                i_row=g_t[i0 + hh:i0 + hh + 1, :]))

    for ch in chains:
        ch["state"] = cs_ref[ch["i"]]
        ch["r"] = _dot_nt(jnp.concatenate([ch["kh"], ch["state"].astype(BF16)], axis=0), ch["qh"])
    for ch in chains:
        b_row, m_prev = ch["b_row"], ms_ref[ch["i"]:ch["i"] + 1, 0:1]
        dm = jnp.where(ch["vis"], ch["u_col"] + b_row, -jnp.inf)
        inter = b_row + m_prev
        m_t = jnp.maximum(inter, jnp.max(dm, axis=0, keepdims=True))
        ch["a"] = jnp.exp(inter - m_t)
        ch["w"] = jnp.exp(dm - m_t)
        ch["floor"] = jnp.exp(-m_t)
        b_end = jnp.broadcast_to(b_row[:, CHUNK - 1:CHUNK] if ch["d"] == 0 else b_row[:, 0:1], (1, CHUNK))
        gl = b_end - b_row + ch["i_row"]
        m_new = jnp.maximum(b_end + m_prev, jnp.max(gl, axis=-1, keepdims=True))
        ch["a_s"] = jnp.exp(b_end + m_prev - m_new)[:, 0:HD_B]
        ch["w_s"] = jnp.exp(gl - m_new)
        ch["m_new"] = m_new
    for ch in chains:
        r, a = ch["r"], ch["a"]
        p = r[0:CHUNK] * ch["w"]
        qn = r[CHUNK + HD_B:CHUNK + HD_B + 1] + r[CHUNK + HD_B + 1:CHUNK + HD_B + 2]
        den = a * qn + jnp.sum(p, axis=0, keepdims=True)
        ch["v_t"] = ch["v32"][:, ch["sl"]].T
        num = a * r[CHUNK:CHUNK + HD_B] + _dot(ch["v_t"].astype(BF16), p.astype(BF16))
        h_t = num * (1.0 / jnp.maximum(jnp.abs(den), ch["floor"]))
        if n_h == 2:
            h_refs[ch["d"]][:, ch["sl"]] = h_t.T
        else:
            ch["h_t"] = h_t
    if n_h == 1:
        for fwd_ch, bwd_ch in zip(chains[:H_B], chains[H_B:]):
            h_refs[0][:, fwd_ch["sl"]] = (fwd_ch["h_t"] + bwd_ch["h_t"]).T
    for ch in chains:
        i, a_s, state = ch["i"], ch["a_s"], ch["state"]
        lhs = jnp.concatenate([ch["v_t"] * ch["w_s"], tail_rows(ch["w_s"])], axis=0).astype(BF16)
        upd = _dot(lhs, ch["kh"])
        n_new = a_s * state[HD_B:HD_B + 1] + upd[HD_B:HD_B + 1] + upd[HD_B + 1:HD_B + 2]
        cs_ref[i, 0:HD_B, :] = a_s * state[0:HD_B] + upd[0:HD_B]
        cs_ref[i, HD_B:HD_B + STATE_TAIL, :] = tail_rows(n_new)
        ms_ref[i:i + 1, :] = ch["m_new"][:, 0:LANES]

    if want_state:
        @pl.when(c == n_chunks - 1)
        def _():
            cf_ref[0] = cs_ref[:, 0:HD_B, :]
            for i in range(N_STATE):
                nf_ref[0, i:i + 1, :] = cs_ref[i, HD_B:HD_B + 1, :]
            mf_ref[0] = ms_ref[...]


def _mlstm(qb, kb, vb, gt, row0, n_batch, seq_len, layer, init=None, want_state=False, c_prev=None):
    n_chunks = seq_len // CHUNK
    blk0 = row0 // CHUNK

    def fwd(w):
        return pl.BlockSpec((CHUNK, w), lambda b, c: (blk0 + b * n_chunks + c, 0))

    def bwd(w):
        return pl.BlockSpec((CHUNK, w), lambda b, c: (blk0 + b * n_chunks + n_chunks - 1 - c, 0))

    in_specs = [fwd(B_WIDTH), fwd(B_WIDTH), fwd(B_WIDTH), fwd(LANES),
                bwd(B_WIDTH), bwd(B_WIDTH), bwd(B_WIDTH), bwd(LANES)]
    args = [qb, kb, vb, gt, qb, kb, vb, gt]
    c_spec = pl.BlockSpec((1, None, N_STATE, HD_B, HD_B), lambda b, c: (b, layer, 0, 0, 0))
    if init is not None:
        in_specs += [c_spec,
                     pl.BlockSpec((1, None, N_STATE, LANES), lambda b, c: (b, layer, 0, 0)),
                     pl.BlockSpec((1, None, N_STATE, LANES), lambda b, c: (b, layer, 0, 0))]
        args += list(init)
    n_rows = n_batch * seq_len
    out_specs = [pl.BlockSpec((CHUNK, B_WIDTH), lambda b, c: (b * n_chunks + c, 0)),
                 pl.BlockSpec((CHUNK, B_WIDTH), lambda b, c: (b * n_chunks + n_chunks - 1 - c, 0))]
    if n_chunks == 1:
        out_specs = out_specs[:1]
    out_shape = [jax.ShapeDtypeStruct((n_rows, B_WIDTH), F32)] * len(out_specs)
    aliases = {}
    if c_prev is not None:
        aliases = {len(args): len(out_specs)}
        in_specs += [pl.BlockSpec(memory_space=pl.ANY)]
        args += [c_prev]
    if want_state:
        out_specs += [c_spec,
                      pl.BlockSpec((1, N_STATE, LANES), lambda b, c: (b, 0, 0)),
                      pl.BlockSpec((1, N_STATE, LANES), lambda b, c: (b, 0, 0))]
        out_shape += [jax.ShapeDtypeStruct((n_batch, DEPTH, N_STATE, HD_B, HD_B), F32),
                      jax.ShapeDtypeStruct((n_batch, N_STATE, LANES), F32),
                      jax.ShapeDtypeStruct((n_batch, N_STATE, LANES), F32)]
    return pl.pallas_call(
        functools.partial(_mlstm_kernel, n_chunks=n_chunks, has_init=init is not None, want_state=want_state,
                          n_alias=len(aliases)),
        grid=(n_batch, n_chunks),
        input_output_aliases=aliases,
        in_specs=in_specs,
        out_specs=out_specs,
        out_shape=out_shape,
        scratch_shapes=[pltpu.VMEM((N_STATE, HD_B + STATE_TAIL, HD_B), F32),
                        pltpu.VMEM((N_STATE, LANES), F32)],
        compiler_params=_params(("parallel", "arbitrary")),
        name="mlstm",
    )(*args)


def _mixout_ffn_kernel(*refs, split_out, n_hc, n_hl):
    x_ref, mod_ref, attc_ref, attl_ref = refs[:4]
    hc_refs = refs[4:4 + n_hc]
    hl_refs = refs[4 + n_hc:4 + n_hc + n_hl]
    n_in = 13 + n_hc + n_hl
    (ob_ref, br_ref, g_ref, ghn_ref, wpa_ref, wpb_ref, wo_ref, wup_ref,
     wdn_ref) = refs[4 + n_hc + n_hl:n_in]
    o_refs = refs[n_in:]
    is_ctx = pl.program_id(0) < N_CTX // TM
    att = jnp.where(is_ctx, attc_ref[...], attl_ref[...])
    hsum = jnp.where(is_ctx, sum(r[...] for r in hc_refs), sum(r[...] for r in hl_refs))
    parts = []
    for hh in range(H_B):
        seg = hsum[:, HD_B * hh:HD_B * (hh + 1)]
        parts.append(seg * lax.rsqrt(jnp.mean(seg * seg, axis=-1, keepdims=True) + EPS))
    hn = (jnp.concatenate(parts, axis=1) * ghn_ref[...]) * ob_ref[...].astype(F32)
    y = (br_ref[:, :D_MODEL].astype(F32) * _dot(att, wpa_ref[...])
         + br_ref[:, D_MODEL:].astype(F32) * _dot(hn.astype(BF16), wpb_ref[...]))
    x = x_ref[...] + _mod_row(mod_ref, 5) * _dot(y.astype(BF16), wo_ref[...])
    out = _swiglu_sub(x, g_ref[2:3, :], _mod_row(mod_ref, 6), _mod_row(mod_ref, 7), _mod_row(mod_ref, 8),
                      wup_ref, wdn_ref)
    _write_stream(o_refs, out, is_ctx, split_out)


def _mixout_ffn(x, mod, att_c, att_l, h_c, h_l, sob, sbr, g_norm, g_hn, w_pa, w_pb, w_o,
                w_up, w_dn, l, split_out=False):
    tok, ctx, lat = _tok_spec, _ctx_spec, _lat_spec
    if split_out:
        out_specs = [ctx(D_MODEL), lat(D_MODEL)]
        out_shape = [jax.ShapeDtypeStruct((N_CTX, D_MODEL), F32), jax.ShapeDtypeStruct((N_LAT, D_MODEL), F32)]
    else:
        out_specs = tok(D_MODEL)
        out_shape = jax.ShapeDtypeStruct((N_TOK, D_MODEL), F32)
    return pl.pallas_call(
        functools.partial(_mixout_ffn_kernel, split_out=split_out, n_hc=len(h_c), n_hl=len(h_l)),
        grid=(N_TOK // TM,),
        in_specs=[
            tok(D_MODEL),
            _layer((MOD_ROWS, N_MOD * D_MODEL), l),
            ctx(A_WIDTH), lat(A_WIDTH),
        ] + [ctx(B_WIDTH)] * len(h_c) + [lat(B_WIDTH)] * len(h_l) + [
            tok(B_WIDTH), tok(2 * D_MODEL),
            _layer((3, D_MODEL), l),
            _layer((1, B_WIDTH), l),
            _layer((A_WIDTH, D_MODEL), l),
            _layer((B_WIDTH, D_MODEL), l),
            _layer((D_MODEL, D_MODEL), l),
            _layer((D_MODEL, 2 * D_FF), l),
            _layer((D_FF, D_MODEL), l),
        ],
        out_specs=out_specs,
        out_shape=out_shape,
        compiler_params=_params(("arbitrary",)),
        name="mixout_ffn",
    )(x, mod, att_c, att_l, *h_c, *h_l, sob, sbr, g_norm, g_hn, w_pa, w_pb, w_o, w_up, w_dn)


def _rope_tables():
    half = HD_B // 2
    nf = half // 2
    inv = ROPE_THETA ** (-jnp.arange(nf, dtype=F32) / nf)
    t = jnp.arange(LAT_LEN)
    rowp = (t // GRID_W).astype(F32)
    colp = (t % GRID_W).astype(F32)
    ang = jnp.concatenate([rowp[:, None] * inv, colp[:, None] * inv], axis=-1)
    cos, sin = jnp.cos(ang), jnp.sin(ang)
    cos_l = jnp.tile(jnp.concatenate([cos, cos], axis=-1), (N_LAT_B, 1))
    sin_l = jnp.tile(jnp.concatenate([-sin, sin], axis=-1), (N_LAT_B, 1))
    cosf = jnp.concatenate([jnp.ones((N_CTX, HD_B), F32), cos_l], axis=0)
    sinf = jnp.concatenate([jnp.zeros((N_CTX, HD_B), F32), sin_l], axis=0)
    return cosf, sinf


def kernel(x_prompt, x_sample, cache_k, cache_v, state_C, state_n, state_m, c, c_ctx, w_ada, b_ada, g_norm,
           w_up1, w_dn1, w_in, b_gates, g_qn, g_kn, rpb, g_hn, w_pa, w_pb, w_o, w_up2, w_dn2):
    assert x_prompt.shape == (N_CTX_B, CTX_LEN, D_MODEL) and x_sample.shape == (N_LAT_B, LAT_LEN, D_MODEL)
    past_len = cache_k.shape[2]

    w_a = w_in[..., :QKV_COLS].astype(BF16)
    w_tail = w_in[..., QKV_COLS:].astype(BF16)
    b_gt = jnp.pad(b_gates, ((0, 0), (0, LANES - N_GATE))).reshape(DEPTH, 1, LANES)
    w_pab, w_pbb, w_ob, w_up2b, w_dn2b = (w.astype(BF16) for w in (w_pa, w_pb, w_o, w_up2, w_dn2))
    gq = jnp.tile(g_qn, (1, H_A)).reshape(DEPTH, 1, A_WIDTH)
    gk = jnp.tile(g_kn, (1, H_A)).reshape(DEPTH, 1, A_WIDTH)
    ghn = g_hn.reshape(DEPTH, 1, B_WIDTH)
    seg_np = (np.arange(A_WIDTH)[:, None] // HD_A == np.arange(LANES)[None, :]).astype(np.float32)
    seg = jnp.asarray(seg_np, BF16)
    segt = jnp.asarray(seg_np.T, BF16)
    cosf, sinf = _rope_tables()
    bias = _bias_tables(rpb)
    ck = cache_k.reshape(N_LAT_B, DEPTH, past_len, A_WIDTH)
    cv = cache_v.reshape(N_LAT_B, DEPTH, past_len, A_WIDTH)
    init = (state_C.reshape(N_LAT_B, DEPTH, N_STATE, HD_B, HD_B),
            state_n.reshape(N_LAT_B, DEPTH, N_STATE, HD_B),
            jnp.broadcast_to(state_m.reshape(N_LAT_B, DEPTH, N_STATE, 1), (N_LAT_B, DEPTH, N_STATE, LANES)))

    cond = jnp.concatenate([c_ctx[None], c, jnp.zeros((MOD_ROWS - 1 - N_LAT_B, D_MODEL), F32)], axis=0)
    mod = _ada(cond, w_ada, b_ada)

    x = (x_prompt.reshape(N_CTX, D_MODEL), x_sample.reshape(N_LAT, D_MODEL))
    caches = tuple(jnp.zeros((N_CTX_B, DEPTH, CTX_LEN, A_WIDTH), F32) for _ in range(2))
    c_new = jnp.zeros((N_CTX_B, DEPTH, N_STATE, HD_B, HD_B), F32)
    ns_out, ms_out = [], []
    for l in range(DEPTH):
        x = _ffn(x, mod, g_norm, w_up1, w_dn1, l, 0)
        qa, ka, va, k_new, v_new, qb, kb, vb, sob, gt, sbr = _mixin(
            x, mod, g_norm, w_a, w_tail, b_gt, gq, gk, seg, segt, cosf, sinf, l, caches)
        caches = (k_new, v_new)
        att_c = _ctx_attn(qa, ka, va)
        att_l = _na_attn(qa, ka, va, ck, cv, bias, l)
        *h_c, c_new, n_fin, m_fin = _mlstm(qb, kb, vb, gt, 0, N_CTX_B, CTX_LEN, l, want_state=True, c_prev=c_new)
        h_l = _mlstm(qb, kb, vb, gt, N_CTX, N_LAT_B, LAT_LEN, l, init=init)
        x = _mixout_ffn(x, mod, att_c, att_l, h_c, h_l, sob, sbr, g_norm, ghn, w_pab, w_pbb, w_ob,
                        w_up2b, w_dn2b, l, split_out=(l == DEPTH - 1))
        ns_out.append(n_fin.reshape(N_CTX_B, 2, H_B, HD_B))
        ms_out.append(m_fin[:, :, 0].reshape(N_CTX_B, 2, H_B))

    return (x[0].reshape(N_CTX_B, CTX_LEN, D_MODEL),
            x[1].reshape(N_LAT_B, LAT_LEN, D_MODEL),
            k_new.reshape(N_CTX_B, DEPTH, CTX_LEN, H_A, HD_A),
            v_new.reshape(N_CTX_B, DEPTH, CTX_LEN, H_A, HD_A),
            c_new.reshape(N_CTX_B, DEPTH, 2, H_B, HD_B, HD_B),
            jnp.stack(ns_out, axis=1), jnp.stack(ms_out, axis=1))
```

```python
import functools

import jax
import jax.numpy as jnp
import numpy as np
from jax import lax
from jax.experimental import pallas as pl
from jax.experimental.pallas import tpu as pltpu

F32 = jnp.float32
BF16 = jnp.bfloat16

D_MODEL = 1024
DEPTH = 4
N_CTX_B, CTX_LEN = 32, 256
N_LAT_B, LAT_LEN = 2, 2048
N_CTX = N_CTX_B * CTX_LEN
N_LAT = N_LAT_B * LAT_LEN
N_TOK = N_CTX + N_LAT
GRID_W, WIN_H, WIN_W = 64, 8, 16
LAT_ROWS = LAT_LEN // GRID_W
H_A, HD_A = 8, 64
H_B, HD_B = 4, 128
A_WIDTH = H_A * HD_A
B_WIDTH = H_B * HD_B
D_FF = 2816
N_MOD = 9
CHUNK = 256
ROPE_THETA = 10000.0
EPS = 1e-6
KSCALE = HD_B ** -0.5
ASCALE = HD_A ** -0.5
N_GATE = 4 * H_B
QKV_COLS = 3 * A_WIDTH + 4 * B_WIDTH
N_STATE = 2 * H_B
STATE_TAIL = 16
MOD_ROWS = 8

LANES = 128
TM = 512
TM_FFN = 512
FF_CHUNK = 256
VMEM_LIMIT = 52 * 1024 * 1024

NA_RB = WIN_H // 2
NA_KR = NA_RB + WIN_H - 1
NA_Q = NA_RB * GRID_W
NA_K = NA_KR * GRID_W
NA_BLOCKS = LAT_ROWS // NA_RB
assert LAT_ROWS % NA_RB == 0 and NA_BLOCKS >= 3 and N_CTX % N_LAT == 0 and 1 + N_LAT_B <= MOD_ROWS


def _dot(a, b):
    return jnp.dot(a, b, preferred_element_type=F32)


def _dot_nt(a, b):
    return lax.dot_general(a, b, (((1,), (1,)), ((), ())), preferred_element_type=F32)


def _split2(x):
    hi = x.astype(BF16)
    lo = (x - hi.astype(F32)).astype(BF16)
    return hi, lo


def _split3(x):
    hi = x.astype(BF16)
    r1 = x - hi.astype(F32)
    mid = r1.astype(BF16)
    lo = (r1 - mid.astype(F32)).astype(BF16)
    return hi, mid, lo


def _norm_mod(x, g, shift, scale):
    ms = jnp.mean(x * x, axis=-1, keepdims=True)
    return (x * lax.rsqrt(ms + EPS) * g) * (1.0 + scale) + shift


def _params(sem):
    return pltpu.CompilerParams(dimension_semantics=sem, vmem_limit_bytes=VMEM_LIMIT)


def _resident(shape):
    nd = len(shape)
    return pl.BlockSpec(shape, lambda *_: (0,) * nd, pipeline_mode=pl.Buffered(1))


def _layer(shape, l):
    nd = len(shape)
    return pl.BlockSpec((None,) + tuple(shape), lambda *_: (l,) + (0,) * nd, pipeline_mode=pl.Buffered(1))


def _tok_spec(w, tm=TM):
    return pl.BlockSpec((tm, w), lambda i: (i, 0))


def _ctx_spec(w, tm=TM):
    return pl.BlockSpec((tm, w), lambda i: (jnp.minimum(i, N_CTX // tm - 1), 0))


def _lat_spec(w, tm=TM):
    return pl.BlockSpec((tm, w), lambda i: (jnp.maximum(i - N_CTX // tm, 0), 0))


def _mod_row(mod_ref, k, tm=TM):
    i = pl.program_id(0)
    n_ctx_tiles = N_CTX // tm
    mod_set = jnp.where(i < n_ctx_tiles, 0, 1 + (i - n_ctx_tiles) // (LAT_LEN // tm))
    return mod_ref[pl.ds(mod_set, 1), D_MODEL * k:D_MODEL * (k + 1)]


def _ada_kernel(c_ref, w_ref, b_ref, o_ref):
    c = c_ref[...]
    s = c * jax.nn.sigmoid(c)
    sh, sl = _split2(s)
    wh, wl = _split2(w_ref[0])
    o_ref[0] = _dot(sh, wh) + _dot(sl, wh) + _dot(sh, wl) + b_ref[0]


def _ada(cond, w_ada, b_ada):
    tn = 1024
    n_out = N_MOD * D_MODEL
    return pl.pallas_call(
        _ada_kernel,
        grid=(DEPTH, n_out // tn),
        in_specs=[
            pl.BlockSpec((MOD_ROWS, D_MODEL), lambda l, j: (0, 0)),
            pl.BlockSpec((1, D_MODEL, tn), lambda l, j: (l, 0, j)),
            pl.BlockSpec((1, 1, tn), lambda l, j: (l, 0, j)),
        ],
        out_specs=pl.BlockSpec((1, MOD_ROWS, tn), lambda l, j: (l, 0, j)),
        out_shape=jax.ShapeDtypeStruct((DEPTH, MOD_ROWS, n_out), F32),
        compiler_params=_params(("parallel", "parallel")),
        name="ada",
    )(cond, w_ada, b_ada.reshape(DEPTH, 1, n_out))


N_FF_CHUNKS = D_FF // FF_CHUNK


FEED_SLOTS = 2


def _ffn_weight_scratch():
    return [pltpu.VMEM((D_MODEL, 2 * D_FF), BF16),
            pltpu.VMEM((D_FF, D_MODEL), BF16),
            pltpu.VMEM((FEED_SLOTS, D_MODEL, FF_CHUNK), F32),
            pltpu.VMEM((FEED_SLOTS, D_MODEL, FF_CHUNK), F32),
            pltpu.VMEM((FEED_SLOTS, FF_CHUNK, D_MODEL), F32),
            pltpu.SemaphoreType.DMA((3, FEED_SLOTS))]


def _ffn_weight_feed(wup_hbm, wdn_hbm, wup_ref, wdn_ref, st_a, st_u, st_d, sems):
    def copies(j):
        slot, c0 = j % FEED_SLOTS, j * FF_CHUNK
        return (pltpu.make_async_copy(wup_hbm.at[:, pl.ds(c0, FF_CHUNK)], st_a.at[slot], sems.at[0, slot]),
                pltpu.make_async_copy(wup_hbm.at[:, pl.ds(D_FF + c0, FF_CHUNK)], st_u.at[slot], sems.at[1, slot]),
                pltpu.make_async_copy(wdn_hbm.at[pl.ds(c0, FF_CHUNK), :], st_d.at[slot], sems.at[2, slot]))

    def before_chunk(j):
        if j == 0:
            for jj in range(min(FEED_SLOTS, N_FF_CHUNKS)):
                for cp in copies(jj):
                    cp.start()
        for cp in copies(j):
            cp.wait()
        slot, c0 = j % FEED_SLOTS, j * FF_CHUNK
        wup_ref[:, c0:c0 + FF_CHUNK] = st_a[slot].astype(BF16)
        wup_ref[:, D_FF + c0:D_FF + c0 + FF_CHUNK] = st_u[slot].astype(BF16)
        wdn_ref[c0:c0 + FF_CHUNK, :] = st_d[slot].astype(BF16)
        if j + FEED_SLOTS < N_FF_CHUNKS:
            for cp in copies(j + FEED_SLOTS):
                cp.start()

    return before_chunk


def _swiglu_sub(x, g, shift, scale, gate, wup_ref, wdn_ref, before_chunk=None):
    h = _norm_mod(x, g, shift, scale).astype(BF16)
    acc = jnp.zeros(x.shape, F32)
    for j in range(N_FF_CHUNKS):
        c0 = j * FF_CHUNK
        if before_chunk is not None:
            before_chunk(j)
        a = _dot(h, wup_ref[:, c0:c0 + FF_CHUNK])
        u = _dot(h, wup_ref[:, D_FF + c0:D_FF + c0 + FF_CHUNK])
        g_act = (a * jax.nn.sigmoid(a) * u).astype(BF16)
        acc = acc + _dot(g_act, wdn_ref[c0:c0 + FF_CHUNK, :])
    return x + 0.5 * gate * acc


def _write_stream(o_refs, y, is_ctx, split_out):
    if split_out:
        @pl.when(is_ctx)
        def _():
            o_refs[0][...] = y

        @pl.when(jnp.logical_not(is_ctx))
        def _():
            o_refs[1][...] = y
    else:
        o_refs[0][...] = y


def _ffn_kernel(*refs, layer, sub, split_in, split_out):
    n_x = 2 if split_in else 1
    mod_ref, g_ref, wup_hbm, wdn_hbm = refs[n_x:n_x + 4]
    n_out = 2 if split_out else 1
    o_refs = refs[n_x + 4:n_x + 4 + n_out]
    wup_ref, wdn_ref, *stage = refs[n_x + 4 + n_out:]
    is_ctx = pl.program_id(0) < N_CTX // TM_FFN
    mod_row = functools.partial(_mod_row, mod_ref, tm=TM_FFN)

    def run(before_chunk):
        x = jnp.where(is_ctx, refs[0][...], refs[1][...]) if split_in else refs[0][...]
        y = _swiglu_sub(x, g_ref[sub:sub + 1, :], mod_row(3 * sub), mod_row(3 * sub + 1), mod_row(3 * sub + 2),
                        wup_ref, wdn_ref, before_chunk)
        _write_stream(o_refs, y, is_ctx, split_out)

    @pl.when(pl.program_id(0) == 0)
    def _():
        run(_ffn_weight_feed(wup_hbm.at[layer], wdn_hbm.at[layer], wup_ref, wdn_ref, *stage))

    @pl.when(pl.program_id(0) != 0)
    def _():
        run(None)


def _ffn(xs, mod, g_norm, w_up, w_dn, l, sub, split_out=False):
    split_in = isinstance(xs, tuple)
    xs = xs if split_in else (xs,)
    tm = TM_FFN
    x_specs = [_ctx_spec(D_MODEL, tm), _lat_spec(D_MODEL, tm)] if split_in else [_tok_spec(D_MODEL, tm)]
    if split_out:
        out_specs = [_ctx_spec(D_MODEL, tm), _lat_spec(D_MODEL, tm)]
        out_shape = [jax.ShapeDtypeStruct((N_CTX, D_MODEL), F32), jax.ShapeDtypeStruct((N_LAT, D_MODEL), F32)]
    else:
        out_specs = _tok_spec(D_MODEL, tm)
        out_shape = jax.ShapeDtypeStruct((N_TOK, D_MODEL), F32)
    return pl.pallas_call(
        functools.partial(_ffn_kernel, layer=l, sub=sub, split_in=split_in, split_out=split_out),
        grid=(N_TOK // tm,),
        in_specs=x_specs + [
            _layer((MOD_ROWS, N_MOD * D_MODEL), l),
            _layer((3, D_MODEL), l),
            pl.BlockSpec(memory_space=pl.ANY),
            pl.BlockSpec(memory_space=pl.ANY),
        ],
        out_specs=out_specs,
        out_shape=out_shape,
        scratch_shapes=_ffn_weight_scratch(),
        compiler_params=_params(("arbitrary",)),
        name="ffn",
    )(*xs, mod, g_norm, w_up, w_dn)


N_MIXIN_IN = 12
N_MIXIN_OUT = 11


def _mixin_kernel(*refs, n_alias):
    (x_ref, mod_ref, g_ref, wa_ref, wtail_ref, bg_ref, gq_ref, gk_ref, seg_ref, segt_ref,
     cos_ref, sin_ref) = refs[:N_MIXIN_IN]
    (qa_ref, ka_ref, va_ref, kc_ref, vc_ref, qb_ref, kb_ref, vb_ref, ob_ref, gt_ref,
     br_ref) = refs[N_MIXIN_IN + n_alias:N_MIXIN_IN + n_alias + N_MIXIN_OUT]
    wbr_ref = refs[N_MIXIN_IN + n_alias + N_MIXIN_OUT]
    is_ctx = pl.program_id(0) < N_CTX // TM

    @pl.when(pl.program_id(0) == 0)
    def _():
        wbr_ref[...] = wtail_ref[:, N_GATE:N_GATE + 2 * D_MODEL]

    x = x_ref[...]
    h = _norm_mod(x, g_ref[1:2, :], _mod_row(mod_ref, 3), _mod_row(mod_ref, 4)).astype(BF16)

    def proj(c0, c1):
        return _dot(h, wa_ref[:, c0:c1])

    def head_sums(p):
        hi, lo = _split2(p * p)
        return _dot(hi, seg_ref[...]) + _dot(lo, seg_ref[...])

    def head_scale(ssum):
        rh, rl = _split2(lax.rsqrt(ssum * (1.0 / HD_A) + EPS))
        return _dot(rh, segt_ref[...]) + _dot(rl, segt_ref[...])

    cosf = cos_ref[...]
    sinf = sin_ref[...]

    def rope(p):
        parts = []
        for hh in range(H_B):
            ph = p[:, HD_B * hh:HD_B * (hh + 1)]
            parts.append(ph * cosf + pltpu.roll(ph, HD_B // 2, 1) * sinf)
        return jnp.concatenate(parts, axis=1)

    cols = [0]
    for w in (A_WIDTH,) * 3 + (B_WIDTH,) * 4:
        cols.append(cols[-1] + w)
    p_qa = proj(cols[0], cols[1])
    p_ka = proj(cols[1], cols[2])
    s_qa = head_sums(p_qa)
    s_ka = head_sums(p_ka)
    va = proj(cols[2], cols[3])
    p_qb = proj(cols[3], cols[4])
    r_qa = head_scale(s_qa)
    r_ka = head_scale(s_ka)
    p_kb = proj(cols[4], cols[5])
    p_vb = proj(cols[5], cols[6])
    p_ob = proj(cols[6], cols[7])

    qa_ref[...] = (p_qa * r_qa * gq_ref[...] * ASCALE).astype(BF16)
    ka = p_ka * r_ka * gk_ref[...]
    ka_ref[...] = ka.astype(BF16)
    va_ref[...] = va.astype(BF16)

    @pl.when(is_ctx)
    def _():
        kc_ref[...] = ka.reshape(kc_ref.shape)
        vc_ref[...] = va.reshape(vc_ref.shape)

    qb_ref[...] = rope(p_qb).astype(BF16)
    kb_ref[...] = (rope(p_kb) * KSCALE).astype(BF16)
    vb_ref[...] = p_vb.astype(BF16)
    ob_ref[...] = jax.nn.sigmoid(p_ob).astype(BF16)
    for j in range(2 * D_MODEL // 512):
        sl = slice(512 * j, 512 * (j + 1))
        br_ref[:, sl] = jax.nn.sigmoid(_dot(h, wbr_ref[:, sl])).astype(BF16)

    pg = _dot(h, wtail_ref[:, 0:LANES]) + bg_ref[...]
    ls = jnp.minimum(pg, 0.0) - jnp.log1p(jnp.exp(-jnp.abs(pg)))
    lane = lax.broadcasted_iota(jnp.int32, pg.shape, 1)
    is_f = ((lane >= H_B) & (lane < 2 * H_B)) | ((lane >= 3 * H_B) & (lane < 4 * H_B))
    gt_ref[...] = jnp.where(is_f, ls, pg)


def _mixin(x, mod, g_norm, w_in, b_gt, gq, gk, seg, segt, cosf, sinf, l, caches=None):
    assert N_GATE + 2 * D_MODEL <= QKV_COLS and w_in.shape[-1] == QKV_COLS + N_GATE + 2 * D_MODEL
    w_block = lambda j: pl.BlockSpec((None, D_MODEL, QKV_COLS), lambda i: (l, 0, j), pipeline_mode=pl.Buffered(1))
    tok = _tok_spec
    out = lambda w, dt: jax.ShapeDtypeStruct((N_TOK, w), dt)
    seqs = TM // CTX_LEN
    ctx_out = jax.ShapeDtypeStruct((N_CTX_B, DEPTH, CTX_LEN, A_WIDTH), F32)
    cache_spec = pl.BlockSpec((seqs, None, CTX_LEN, A_WIDTH),
                              lambda i: (jnp.minimum(i, N_CTX // TM - 1), l, 0, 0))
    caches = () if caches is None else tuple(caches)
    return pl.pallas_call(
        functools.partial(_mixin_kernel, n_alias=len(caches)),
        grid=(N_TOK // TM,),
        input_output_aliases={N_MIXIN_IN + j: 3 + j for j in range(len(caches))},
        in_specs=[
            tok(D_MODEL),
            _layer((MOD_ROWS, N_MOD * D_MODEL), l),
            _layer((3, D_MODEL), l),
            w_block(0),
            w_block(1),
            _layer((1, LANES), l),
            _layer((1, A_WIDTH), l),
            _layer((1, A_WIDTH), l),
            _resident((A_WIDTH, LANES)),
            _resident((LANES, A_WIDTH)),
            tok(HD_B),
            tok(HD_B),
        ] + [pl.BlockSpec(memory_space=pl.ANY)] * len(caches),
        out_specs=[tok(A_WIDTH), tok(A_WIDTH), tok(A_WIDTH), cache_spec, cache_spec,
                   tok(B_WIDTH), tok(B_WIDTH), tok(B_WIDTH), tok(B_WIDTH), tok(LANES), tok(2 * D_MODEL)],
        out_shape=[out(A_WIDTH, BF16), out(A_WIDTH, BF16), out(A_WIDTH, BF16), ctx_out, ctx_out,
                   out(B_WIDTH, BF16), out(B_WIDTH, BF16), out(B_WIDTH, BF16),
                   out(B_WIDTH, BF16), out(LANES, F32), out(2 * D_MODEL, BF16)],
        scratch_shapes=[pltpu.VMEM((D_MODEL, 2 * D_MODEL), BF16)],
        compiler_params=_params(("arbitrary",)),
        name="mixin",
    )(x, mod, g_norm, w_in, w_in, b_gt, gq, gk, seg, segt, cosf, sinf, *caches)


CTX_SEQS = 2


def _head_pair_masks():
    lane = lax.broadcasted_iota(jnp.int32, (1, LANES), 1)
    first = lane < HD_A
    return first, [jnp.where(first, 1.0, 0.0).astype(BF16), jnp.where(first, 0.0, 1.0).astype(BF16)]


def _ctx_attn_kernel(q_ref, k_ref, v_ref, o_ref):
    first, keep = _head_pair_masks()
    q = q_ref[...]
    k = k_ref[...]
    v = v_ref[...]
    items = [(slice(CTX_LEN * sq, CTX_LEN * (sq + 1)), slice(LANES * t, LANES * (t + 1)), half)
             for sq in range(CTX_SEQS) for t in range(A_WIDTH // LANES) for half in range(2)]
    scores = [_dot_nt(q[rows, tile] * keep[half], k[rows, tile]) for rows, tile, half in items]
    probs = [jnp.exp(s - jnp.max(s, axis=-1, keepdims=True)) for s in scores]
    outs = [_dot(p.astype(BF16), v[rows, tile]) / jnp.sum(p, axis=-1, keepdims=True)
            for p, (rows, tile, half) in zip(probs, items)]
    for (rows, tile, _), o_first, o_second in zip(items[0::2], outs[0::2], outs[1::2]):
        o_ref[rows, tile] = jnp.where(first, o_first, o_second).astype(BF16)


def _ctx_attn(qa, ka, va):
    blk = pl.BlockSpec((CTX_SEQS * CTX_LEN, A_WIDTH), lambda b: (b, 0))
    return pl.pallas_call(
        _ctx_attn_kernel,
        grid=(N_CTX_B // CTX_SEQS,),
        in_specs=[blk, blk, blk],
        out_specs=blk,
        out_shape=jax.ShapeDtypeStruct((N_CTX, A_WIDTH), BF16),
        compiler_params=_params(("parallel",)),
        name="ctx_attn",
    )(qa, ka, va)


def _na_key_row0(j):
    return jnp.clip(NA_RB * j - WIN_H // 2, 0, LAT_ROWS - NA_KR)


def _na_kernel(q_ref, k_ref, v_ref, ck_ref, cv_ref, bias_ref, o_ref):
    j = pl.program_id(0)
    b = pl.program_id(1)
    base = pl.multiple_of(b * LAT_LEN + _na_key_row0(j) * GRID_W, GRID_W)
    kw = k_ref[pl.ds(base, NA_K), :]
    vw = v_ref[pl.ds(base, NA_K), :]
    ck = ck_ref[b].astype(BF16)
    cv = cv_ref[b].astype(BF16)
    q = q_ref[...]
    heads = [slice(HD_A * hh, HD_A * (hh + 1)) for hh in range(H_A)]
    s_win = [_dot_nt(q[:, sl], kw[:, sl]) for sl in heads]
    s_ctx = [_dot_nt(q[:, sl], ck[:, sl]) for sl in heads]
    probs = []
    for hh in range(H_A):
        sw = s_win[hh] + bias_ref[hh]
        sx = s_ctx[hh]
        m = jnp.maximum(jnp.max(sw, axis=-1, keepdims=True), jnp.max(sx, axis=-1, keepdims=True))
        pw = jnp.exp(sw - m)
        px = jnp.exp(sx - m)
        l = jnp.sum(pw, axis=-1, keepdims=True) + jnp.sum(px, axis=-1, keepdims=True)
        probs.append((pw.astype(BF16), px.astype(BF16), l))
    outs = [(_dot(pw, vw[:, sl]) + _dot(px, cv[:, sl])) / l for (pw, px, l), sl in zip(probs, heads)]
    o_ref[...] = jnp.concatenate(outs, axis=1).astype(BF16)


def _na_attn(qa, ka, va, ck, cv, bias, l):
    past = ck.shape[2]
    lat_kv = pl.BlockSpec((N_LAT, A_WIDTH), lambda j, b: (N_CTX // N_LAT, 0), pipeline_mode=pl.Buffered(1))
    cache = pl.BlockSpec((N_LAT_B, None, past, A_WIDTH), lambda j, b: (0, l, 0, 0), pipeline_mode=pl.Buffered(1))
    bias_class = lambda j: jnp.where(j == 0, 0, jnp.where(j == NA_BLOCKS - 1, 2, 1))
    return pl.pallas_call(
        _na_kernel,
        grid=(NA_BLOCKS, N_LAT_B),
        in_specs=[
            pl.BlockSpec((NA_Q, A_WIDTH), lambda j, b: (N_CTX // NA_Q + b * NA_BLOCKS + j, 0)),
            lat_kv, lat_kv, cache, cache,
            pl.BlockSpec((None, None, H_A, NA_Q, NA_K), lambda j, b: (l, bias_class(j), 0, 0, 0)),
        ],
        out_specs=pl.BlockSpec((NA_Q, A_WIDTH), lambda j, b: (b * NA_BLOCKS + j, 0)),
        out_shape=jax.ShapeDtypeStruct((N_LAT, A_WIDTH), BF16),
        compiler_params=_params(("arbitrary", "arbitrary")),
        name="na_attn",
    )(qa, ka, va, ck, cv, bias)


def _na_class_geometry():
    out = []
    for j in (0, 1, NA_BLOCKS - 1):
        r = NA_RB * j + np.arange(NA_RB)
        kr = int(np.clip(NA_RB * j - WIN_H // 2, 0, LAT_ROWS - NA_KR)) + np.arange(NA_KR)
        start_r = np.clip(r - WIN_H // 2, 0, LAT_ROWS - WIN_H)
        row_ok = (kr[None, :] >= start_r[:, None]) & (kr[None, :] < start_r[:, None] + WIN_H)
        dr = kr[None, :] - r[:, None] + (WIN_H - 1)
        out.append((dr, row_ok))
    return out


def _bias_kernel(r_ref, o_ref):
    qc = lax.broadcasted_iota(jnp.int32, (GRID_W, GRID_W), 0)
    kc = lax.broadcasted_iota(jnp.int32, (GRID_W, GRID_W), 1)
    c0 = jnp.clip(qc - WIN_W // 2, 0, GRID_W - WIN_W)
    col_ok = (kc >= c0) & (kc < c0 + WIN_W)
    neg = jnp.full((GRID_W, GRID_W), -1e30, F32)
    toep = []
    for dr in range(2 * WIN_H - 1):
        rows = jnp.broadcast_to(r_ref[0, 0, dr:dr + 1, :], (GRID_W, LANES))
        t = pltpu.roll(rows, 0, 1, stride=1, stride_axis=0)[:, :GRID_W]
        toep.append(jnp.where(col_ok, t, neg))
    for cls, (dr_tab, row_ok) in enumerate(_na_class_geometry()):
        for a in range(NA_RB):
            for jj in range(NA_KR):
                o_ref[0, cls, 0, GRID_W * a:GRID_W * (a + 1), GRID_W * jj:GRID_W * (jj + 1)] = (
                    toep[int(dr_tab[a, jj])] if row_ok[a, jj] else neg)


def _bias_tables(rpb):
    n_dr, n_dc = 2 * WIN_H - 1, 2 * WIN_W - 1
    assert rpb.shape == (DEPTH, H_A, n_dr, n_dc)
    rows = jnp.concatenate([rpb[..., WIN_W - 1:], jnp.zeros((DEPTH, H_A, n_dr, LANES - n_dc), F32),
                            rpb[..., :WIN_W - 1]], axis=-1)
    rows = jnp.pad(rows, ((0, 0), (0, 0), (0, 16 - n_dr), (0, 0)))
    return pl.pallas_call(
        _bias_kernel,
        grid=(DEPTH, H_A),
        in_specs=[pl.BlockSpec((1, 1, 16, LANES), lambda l, h: (l, h, 0, 0))],
        out_specs=pl.BlockSpec((1, 3, 1, NA_Q, NA_K), lambda l, h: (l, 0, h, 0, 0)),
        out_shape=jax.ShapeDtypeStruct((DEPTH, 3, H_A, NA_Q, NA_K), F32),
        compiler_params=_params(("arbitrary", "arbitrary")),
        name="na_bias",
    )(rows)


def _mlstm_kernel(*refs, n_chunks, has_init, want_state, n_alias):
    dirs = (refs[0:4], refs[4:8])
    pos = 8
    if has_init:
        c0_ref, n0_ref, m0_ref = refs[pos:pos + 3]
        pos += 3
    pos += n_alias
    n_h = 1 if n_chunks == 1 else 2
    h_refs = refs[pos:pos + n_h]
    pos += n_h
    if want_state:
        cf_ref, nf_ref, mf_ref = refs[pos:pos + 3]
        pos += 3
    cs_ref, ms_ref = refs[pos:pos + 2]
    c = pl.program_id(1)

    def tail_rows(x):
        r = lax.broadcasted_iota(jnp.int32, (STATE_TAIL, x.shape[1]), 0)
        return jnp.where(r == 0, x, jnp.where(r == 1, x - x.astype(BF16).astype(F32), 0.0))

    @pl.when(c == 0)
    def _():
        if has_init:
            cs_ref[:, 0:HD_B, :] = c0_ref[0]
            for i in range(N_STATE):
                cs_ref[i, HD_B:HD_B + STATE_TAIL, :] = tail_rows(n0_ref[0, i:i + 1, :])
            ms_ref[...] = m0_ref[0]
        else:
            cs_ref[...] = jnp.zeros(cs_ref.shape, F32)
            ms_ref[...] = jnp.zeros(ms_ref.shape, F32)

    row = lax.broadcasted_iota(jnp.int32, (CHUNK, CHUNK), 0)
    col = lax.broadcasted_iota(jnp.int32, (CHUNK, CHUNK), 1)
    chains = []
    for d in range(2):
        q_ref, k_ref, v_ref, g_ref = dirs[d]
        vis = (row <= col) if d == 0 else (row >= col)
        cum_b = jnp.where((col <= row) if d == 0 else (col >= row), 1.0, 0.0).astype(BF16)
        g_t = g_ref[...].T[0:N_GATE, :]
        r = _dot_nt(jnp.concatenate(_split3(g_t), axis=0), cum_b)
        b_t = r[0:N_GATE] + r[N_GATE:2 * N_GATE] + r[2 * N_GATE:3 * N_GATE]
        i0 = 2 * H_B * d
        u_t = g_t[i0:i0 + H_B, :] - b_t[i0 + H_B:i0 + 2 * H_B, :]
        u_cols = jnp.concatenate([u_t, jnp.zeros((LANES - H_B, CHUNK), F32)], axis=0).T
        q = q_ref[...]
        k = k_ref[...]
        v32 = v_ref[...].astype(F32)
        for hh in range(H_B):
            sl = slice(HD_B * hh, HD_B * (hh + 1))
            chains.append(dict(
                d=d, i=d * H_B + hh, sl=sl, vis=vis, qh=q[:, sl], kh=k[:, sl], v32=v32,
                u_col=u_cols[:, hh:hh + 1],
                b_row=b_t[i0 + H_B + hh:i0 + H_B + hh + 1, :],
                i_row=g_t[i0 + hh:i0 + hh + 1, :]))

    for ch in chains:
        ch["state"] = cs_ref[ch["i"]]
        ch["r"] = _dot_nt(jnp.concatenate([ch["kh"], ch["state"].astype(BF16)], axis=0), ch["qh"])
    for ch in chains:
        b_row, m_prev = ch["b_row"], ms_ref[ch["i"]:ch["i"] + 1, 0:1]
        dm = jnp.where(ch["vis"], ch["u_col"] + b_row, -jnp.inf)
        inter = b_row + m_prev
        m_t = jnp.maximum(inter, jnp.max(dm, axis=0, keepdims=True))
        ch["a"] = jnp.exp(inter - m_t)
        ch["w"] = jnp.exp(dm - m_t)
        ch["floor"] = jnp.exp(-m_t)
        b_end = jnp.broadcast_to(b_row[:, CHUNK - 1:CHUNK] if ch["d"] == 0 else b_row[:, 0:1], (1, CHUNK))
        gl = b_end - b_row + ch["i_row"]
        m_new = jnp.maximum(b_end + m_prev, jnp.max(gl, axis=-1, keepdims=True))
        ch["a_s"] = jnp.exp(b_end + m_prev - m_new)[:, 0:HD_B]
        ch["w_s"] = jnp.exp(gl - m_new)
        ch["m_new"] = m_new
    for ch in chains:
        r, a = ch["r"], ch["a"]
        p = r[0:CHUNK] * ch["w"]
        qn = r[CHUNK + HD_B:CHUNK + HD_B + 1] + r[CHUNK + HD_B + 1:CHUNK + HD_B + 2]
        den = a * qn + jnp.sum(p, axis=0, keepdims=True)
        ch["v_t"] = ch["v32"][:, ch["sl"]].T
        num = a * r[CHUNK:CHUNK + HD_B] + _dot(ch["v_t"].astype(BF16), p.astype(BF16))
        h_t = num * (1.0 / jnp.maximum(jnp.abs(den), ch["floor"]))
        if n_h == 2:
            h_refs[ch["d"]][:, ch["sl"]] = h_t.T
        else:
            ch["h_t"] = h_t
    if n_h == 1:
        for fwd_ch, bwd_ch in zip(chains[:H_B], chains[H_B:]):
            h_refs[0][:, fwd_ch["sl"]] = (fwd_ch["h_t"] + bwd_ch["h_t"]).T
    for ch in chains:
        i, a_s, state = ch["i"], ch["a_s"], ch["state"]
        lhs = jnp.concatenate([ch["v_t"] * ch["w_s"], tail_rows(ch["w_s"])], axis=0).astype(BF16)
        upd = _dot(lhs, ch["kh"])
        n_new = a_s * state[HD_B:HD_B + 1] + upd[HD_B:HD_B + 1] + upd[HD_B + 1:HD_B + 2]
        cs_ref[i, 0:HD_B, :] = a_s * state[0:HD_B] + upd[0:HD_B]
        cs_ref[i, HD_B:HD_B + STATE_TAIL, :] = tail_rows(n_new)
        ms_ref[i:i + 1, :] = ch["m_new"][:, 0:LANES]

    if want_state:
        @pl.when(c == n_chunks - 1)
        def _():
            cf_ref[0] = cs_ref[:, 0:HD_B, :]
            for i in range(N_STATE):
                nf_ref[0, i:i + 1, :] = cs_ref[i, HD_B:HD_B + 1, :]
            mf_ref[0] = ms_ref[...]


def _mlstm(qb, kb, vb, gt, row0, n_batch, seq_len, layer, init=None, want_state=False, c_prev=None):
    n_chunks = seq_len // CHUNK
    blk0 = row0 // CHUNK

    def fwd(w):
        return pl.BlockSpec((CHUNK, w), lambda b, c: (blk0 + b * n_chunks + c, 0))

    def bwd(w):
        return pl.BlockSpec((CHUNK, w), lambda b, c: (blk0 + b * n_chunks + n_chunks - 1 - c, 0))

    in_specs = [fwd(B_WIDTH), fwd(B_WIDTH), fwd(B_WIDTH), fwd(LANES),
                bwd(B_WIDTH), bwd(B_WIDTH), bwd(B_WIDTH), bwd(LANES)]
    args = [qb, kb, vb, gt, qb, kb, vb, gt]
    c_spec = pl.BlockSpec((1, None, N_STATE, HD_B, HD_B), lambda b, c: (b, layer, 0, 0, 0))
    if init is not None:
        in_specs += [c_spec,
                     pl.BlockSpec((1, None, N_STATE, LANES), lambda b, c: (b, layer, 0, 0)),
                     pl.BlockSpec((1, None, N_STATE, LANES), lambda b, c: (b, layer, 0, 0))]
        args += list(init)
    n_rows = n_batch * seq_len
    out_specs = [pl.BlockSpec((CHUNK, B_WIDTH), lambda b, c: (b * n_chunks + c, 0)),
                 pl.BlockSpec((CHUNK, B_WIDTH), lambda b, c: (b * n_chunks + n_chunks - 1 - c, 0))]
    if n_chunks == 1:
        out_specs = out_specs[:1]
    out_shape = [jax.ShapeDtypeStruct((n_rows, B_WIDTH), F32)] * len(out_specs)
    aliases = {}
    if c_prev is not None:
        aliases = {len(args): len(out_specs)}
        in_specs += [pl.BlockSpec(memory_space=pl.ANY)]
        args += [c_prev]
    if want_state:
        out_specs += [c_spec,
                      pl.BlockSpec((1, N_STATE, LANES), lambda b, c: (b, 0, 0)),
                      pl.BlockSpec((1, N_STATE, LANES), lambda b, c: (b, 0, 0))]
        out_shape += [jax.ShapeDtypeStruct((n_batch, DEPTH, N_STATE, HD_B, HD_B), F32),
                      jax.ShapeDtypeStruct((n_batch, N_STATE, LANES), F32),
                      jax.ShapeDtypeStruct((n_batch, N_STATE, LANES), F32)]
    return pl.pallas_call(
        functools.partial(_mlstm_kernel, n_chunks=n_chunks, has_init=init is not None, want_state=want_state,
                          n_alias=len(aliases)),
        grid=(n_batch, n_chunks),
        input_output_aliases=aliases,
        in_specs=in_specs,
        out_specs=out_specs,
        out_shape=out_shape,
        scratch_shapes=[pltpu.VMEM((N_STATE, HD_B + STATE_TAIL, HD_B), F32),
                        pltpu.VMEM((N_STATE, LANES), F32)],
        compiler_params=_params(("parallel", "arbitrary")),
        name="mlstm",
    )(*args)


def _mixout_ffn_kernel(*refs, split_out, n_hc, n_hl):
    x_ref, mod_ref, attc_ref, attl_ref = refs[:4]
    hc_refs = refs[4:4 + n_hc]
    hl_refs = refs[4 + n_hc:4 + n_hc + n_hl]
    n_in = 13 + n_hc + n_hl
    (ob_ref, br_ref, g_ref, ghn_ref, wpa_ref, wpb_ref, wo_ref, wup_ref,
     wdn_ref) = refs[4 + n_hc + n_hl:n_in]
    o_refs = refs[n_in:]
    is_ctx = pl.program_id(0) < N_CTX // TM
    att = jnp.where(is_ctx, attc_ref[...], attl_ref[...])
    hsum = jnp.where(is_ctx, sum(r[...] for r in hc_refs), sum(r[...] for r in hl_refs))
    parts = []
    for hh in range(H_B):
        seg = hsum[:, HD_B * hh:HD_B * (hh + 1)]
        parts.append(seg * lax.rsqrt(jnp.mean(seg * seg, axis=-1, keepdims=True) + EPS))
    hn = (jnp.concatenate(parts, axis=1) * ghn_ref[...]) * ob_ref[...].astype(F32)
    y = (br_ref[:, :D_MODEL].astype(F32) * _dot(att, wpa_ref[...])
         + br_ref[:, D_MODEL:].astype(F32) * _dot(hn.astype(BF16), wpb_ref[...]))
    x = x_ref[...] + _mod_row(mod_ref, 5) * _dot(y.astype(BF16), wo_ref[...])
    out = _swiglu_sub(x, g_ref[2:3, :], _mod_row(mod_ref, 6), _mod_row(mod_ref, 7), _mod_row(mod_ref, 8),
                      wup_ref, wdn_ref)
    _write_stream(o_refs, out, is_ctx, split_out)


def _mixout_ffn(x, mod, att_c, att_l, h_c, h_l, sob, sbr, g_norm, g_hn, w_pa, w_pb, w_o,
                w_up, w_dn, l, split_out=False):
    tok, ctx, lat = _tok_spec, _ctx_spec, _lat_spec
    if split_out:
        out_specs = [ctx(D_MODEL), lat(D_MODEL)]
        out_shape = [jax.ShapeDtypeStruct((N_CTX, D_MODEL), F32), jax.ShapeDtypeStruct((N_LAT, D_MODEL), F32)]
    else:
        out_specs = tok(D_MODEL)
        out_shape = jax.ShapeDtypeStruct((N_TOK, D_MODEL), F32)
    return pl.pallas_call(
        functools.partial(_mixout_ffn_kernel, split_out=split_out, n_hc=len(h_c), n_hl=len(h_l)),
        grid=(N_TOK // TM,),
        in_specs=[
            tok(D_MODEL),
            _layer((MOD_ROWS, N_MOD * D_MODEL), l),
            ctx(A_WIDTH), lat(A_WIDTH),
        ] + [ctx(B_WIDTH)] * len(h_c) + [lat(B_WIDTH)] * len(h_l) + [
            tok(B_WIDTH), tok(2 * D_MODEL),
            _layer((3, D_MODEL), l),
            _layer((1, B_WIDTH), l),
            _layer((A_WIDTH, D_MODEL), l),
            _layer((B_WIDTH, D_MODEL), l),
            _layer((D_MODEL, D_MODEL), l),
            _layer((D_MODEL, 2 * D_FF), l),
            _layer((D_FF, D_MODEL), l),
        ],
        out_specs=out_specs,
        out_shape=out_shape,
        compiler_params=_params(("arbitrary",)),
        name="mixout_ffn",
    )(x, mod, att_c, att_l, *h_c, *h_l, sob, sbr, g_norm, g_hn, w_pa, w_pb, w_o, w_up, w_dn)


def _rope_tables():
    half = HD_B // 2
    nf = half // 2
    inv = ROPE_THETA ** (-jnp.arange(nf, dtype=F32) / nf)
    t = jnp.arange(LAT_LEN)
    rowp = (t // GRID_W).astype(F32)
    colp = (t % GRID_W).astype(F32)
    ang = jnp.concatenate([rowp[:, None] * inv, colp[:, None] * inv], axis=-1)
    cos, sin = jnp.cos(ang), jnp.sin(ang)
    cos_l = jnp.tile(jnp.concatenate([cos, cos], axis=-1), (N_LAT_B, 1))
    sin_l = jnp.tile(jnp.concatenate([-sin, sin], axis=-1), (N_LAT_B, 1))
    cosf = jnp.concatenate([jnp.ones((N_CTX, HD_B), F32), cos_l], axis=0)
    sinf = jnp.concatenate([jnp.zeros((N_CTX, HD_B), F32), sin_l], axis=0)
    return cosf, sinf


def kernel(x_prompt, x_sample, cache_k, cache_v, state_C, state_n, state_m, c, c_ctx, w_ada, b_ada, g_norm,
           w_up1, w_dn1, w_in, b_gates, g_qn, g_kn, rpb, g_hn, w_pa, w_pb, w_o, w_up2, w_dn2):
    assert x_prompt.shape == (N_CTX_B, CTX_LEN, D_MODEL) and x_sample.shape == (N_LAT_B, LAT_LEN, D_MODEL)
    past_len = cache_k.shape[2]

    w_inb = w_in.astype(BF16)
    b_gt = jnp.pad(b_gates, ((0, 0), (0, LANES - N_GATE))).reshape(DEPTH, 1, LANES)
    w_pab, w_pbb, w_ob, w_up2b, w_dn2b = (w.astype(BF16) for w in (w_pa, w_pb, w_o, w_up2, w_dn2))
    gq = jnp.tile(g_qn, (1, H_A)).reshape(DEPTH, 1, A_WIDTH)
    gk = jnp.tile(g_kn, (1, H_A)).reshape(DEPTH, 1, A_WIDTH)
    ghn = g_hn.reshape(DEPTH, 1, B_WIDTH)
    seg_np = (np.arange(A_WIDTH)[:, None] // HD_A == np.arange(LANES)[None, :]).astype(np.float32)
    seg = jnp.asarray(seg_np, BF16)
    segt = jnp.asarray(seg_np.T, BF16)
    cosf, sinf = _rope_tables()
    bias = _bias_tables(rpb)
    ck = cache_k.reshape(N_LAT_B, DEPTH, past_len, A_WIDTH)
    cv = cache_v.reshape(N_LAT_B, DEPTH, past_len, A_WIDTH)
    init = (state_C.reshape(N_LAT_B, DEPTH, N_STATE, HD_B, HD_B),
            state_n.reshape(N_LAT_B, DEPTH, N_STATE, HD_B),
            jnp.broadcast_to(state_m.reshape(N_LAT_B, DEPTH, N_STATE, 1), (N_LAT_B, DEPTH, N_STATE, LANES)))

    cond = jnp.concatenate([c_ctx[None], c, jnp.zeros((MOD_ROWS - 1 - N_LAT_B, D_MODEL), F32)], axis=0)
    mod = _ada(cond, w_ada, b_ada)

    x = (x_prompt.reshape(N_CTX, D_MODEL), x_sample.reshape(N_LAT, D_MODEL))
    caches = tuple(jnp.zeros((N_CTX_B, DEPTH, CTX_LEN, A_WIDTH), F32) for _ in range(2))
    c_new = jnp.zeros((N_CTX_B, DEPTH, N_STATE, HD_B, HD_B), F32)
    ns_out, ms_out = [], []
    for l in range(DEPTH):
        x = _ffn(x, mod, g_norm, w_up1, w_dn1, l, 0)
        qa, ka, va, k_new, v_new, qb, kb, vb, sob, gt, sbr = _mixin(
            x, mod, g_norm, w_inb, b_gt, gq, gk, seg, segt, cosf, sinf, l, caches)
        caches = (k_new, v_new)
        att_c = _ctx_attn(qa, ka, va)
        att_l = _na_attn(qa, ka, va, ck, cv, bias, l)
        *h_c, c_new, n_fin, m_fin = _mlstm(qb, kb, vb, gt, 0, N_CTX_B, CTX_LEN, l, want_state=True, c_prev=c_new)
        h_l = _mlstm(qb, kb, vb, gt, N_CTX, N_LAT_B, LAT_LEN, l, init=init)
        x = _mixout_ffn(x, mod, att_c, att_l, h_c, h_l, sob, sbr, g_norm, ghn, w_pab, w_pbb, w_ob,
                        w_up2b, w_dn2b, l, split_out=(l == DEPTH - 1))
        ns_out.append(n_fin.reshape(N_CTX_B, 2, H_B, HD_B))
        ms_out.append(m_fin[:, :, 0].reshape(N_CTX_B, 2, H_B))

    return (x[0].reshape(N_CTX_B, CTX_LEN, D_MODEL),
            x[1].reshape(N_LAT_B, LAT_LEN, D_MODEL),
            k_new.reshape(N_CTX_B, DEPTH, CTX_LEN, H_A, HD_A),
            v_new.reshape(N_CTX_B, DEPTH, CTX_LEN, H_A, HD_A),
            c_new.reshape(N_CTX_B, DEPTH, 2, H_B, HD_B, HD_B),
            jnp.stack(ns_out, axis=1), jnp.stack(ms_out, axis=1))
```

```python
import functools

import jax
import jax.numpy as jnp
import numpy as np
from jax import lax
from jax.experimental import pallas as pl
from jax.experimental.pallas import tpu as pltpu

F32 = jnp.float32
BF16 = jnp.bfloat16

D_MODEL = 1024
DEPTH = 4
N_CTX_B, CTX_LEN = 32, 256
N_LAT_B, LAT_LEN = 2, 2048
N_CTX = N_CTX_B * CTX_LEN
N_LAT = N_LAT_B * LAT_LEN
N_TOK = N_CTX + N_LAT
GRID_W, WIN_H, WIN_W = 64, 8, 16
LAT_ROWS = LAT_LEN // GRID_W
H_A, HD_A = 8, 64
H_B, HD_B = 4, 128
A_WIDTH = H_A * HD_A
B_WIDTH = H_B * HD_B
D_FF = 2816
N_MOD = 9
CHUNK = 256
ROPE_THETA = 10000.0
EPS = 1e-6
KSCALE = HD_B ** -0.5
ASCALE = HD_A ** -0.5
N_GATE = 4 * H_B
QKV_COLS = 3 * A_WIDTH + 4 * B_WIDTH
N_STATE = 2 * H_B
STATE_TAIL = 16
MOD_ROWS = 8

LANES = 128
TM = 512
TM_FFN = 512
FF_CHUNK = 256
VMEM_LIMIT = 52 * 1024 * 1024

NA_RB = WIN_H // 2
NA_KR = NA_RB + WIN_H - 1
NA_Q = NA_RB * GRID_W
NA_K = NA_KR * GRID_W
NA_BLOCKS = LAT_ROWS // NA_RB
assert LAT_ROWS % NA_RB == 0 and NA_BLOCKS >= 3 and N_CTX % N_LAT == 0 and 1 + N_LAT_B <= MOD_ROWS


def _dot(a, b):
    return jnp.dot(a, b, preferred_element_type=F32)


def _dot_nt(a, b):
    return lax.dot_general(a, b, (((1,), (1,)), ((), ())), preferred_element_type=F32)


def _split2(x):
    hi = x.astype(BF16)
    lo = (x - hi.astype(F32)).astype(BF16)
    return hi, lo


def _split3(x):
    hi = x.astype(BF16)
    r1 = x - hi.astype(F32)
    mid = r1.astype(BF16)
    lo = (r1 - mid.astype(F32)).astype(BF16)
    return hi, mid, lo


def _norm_mod(x, g, shift, scale):
    ms = jnp.mean(x * x, axis=-1, keepdims=True)
    return (x * lax.rsqrt(ms + EPS) * g) * (1.0 + scale) + shift


def _params(sem):
    return pltpu.CompilerParams(dimension_semantics=sem, vmem_limit_bytes=VMEM_LIMIT)


def _resident(shape):
    nd = len(shape)
    return pl.BlockSpec(shape, lambda *_: (0,) * nd, pipeline_mode=pl.Buffered(1))


def _layer(shape, l):
    nd = len(shape)
    return pl.BlockSpec((None,) + tuple(shape), lambda *_: (l,) + (0,) * nd, pipeline_mode=pl.Buffered(1))


def _tok_spec(w, tm=TM):
    return pl.BlockSpec((tm, w), lambda i: (i, 0))


def _ctx_spec(w, tm=TM):
    return pl.BlockSpec((tm, w), lambda i: (jnp.minimum(i, N_CTX // tm - 1), 0))


def _lat_spec(w, tm=TM):
    return pl.BlockSpec((tm, w), lambda i: (jnp.maximum(i - N_CTX // tm, 0), 0))


def _mod_row(mod_ref, k, tm=TM):
    i = pl.program_id(0)
    n_ctx_tiles = N_CTX // tm
    mod_set = jnp.where(i < n_ctx_tiles, 0, 1 + (i - n_ctx_tiles) // (LAT_LEN // tm))
    return mod_ref[pl.ds(mod_set, 1), D_MODEL * k:D_MODEL * (k + 1)]


def _ada_kernel(c_ref, w_ref, b_ref, o_ref):
    c = c_ref[...]
    s = c * jax.nn.sigmoid(c)
    sh, sl = _split2(s)
    wh, wl = _split2(w_ref[0])
    o_ref[0] = _dot(sh, wh) + _dot(sl, wh) + _dot(sh, wl) + b_ref[0]


def _ada(cond, w_ada, b_ada):
    tn = 1024
    n_out = N_MOD * D_MODEL
    return pl.pallas_call(
        _ada_kernel,
        grid=(DEPTH, n_out // tn),
        in_specs=[
            pl.BlockSpec((MOD_ROWS, D_MODEL), lambda l, j: (0, 0)),
            pl.BlockSpec((1, D_MODEL, tn), lambda l, j: (l, 0, j)),
            pl.BlockSpec((1, 1, tn), lambda l, j: (l, 0, j)),
        ],
        out_specs=pl.BlockSpec((1, MOD_ROWS, tn), lambda l, j: (l, 0, j)),
        out_shape=jax.ShapeDtypeStruct((DEPTH, MOD_ROWS, n_out), F32),
        compiler_params=_params(("parallel", "parallel")),
        name="ada",
    )(cond, w_ada, b_ada.reshape(DEPTH, 1, n_out))


N_FF_CHUNKS = D_FF // FF_CHUNK


FEED_SLOTS = 2


def _ffn_weight_scratch():
    return [pltpu.VMEM((D_MODEL, 2 * D_FF), BF16),
            pltpu.VMEM((D_FF, D_MODEL), BF16),
            pltpu.VMEM((FEED_SLOTS, D_MODEL, FF_CHUNK), F32),
            pltpu.VMEM((FEED_SLOTS, D_MODEL, FF_CHUNK), F32),
            pltpu.VMEM((FEED_SLOTS, FF_CHUNK, D_MODEL), F32),
            pltpu.SemaphoreType.DMA((3, FEED_SLOTS))]


def _ffn_weight_feed(wup_hbm, wdn_hbm, wup_ref, wdn_ref, st_a, st_u, st_d, sems):
    def copies(j):
        slot, c0 = j % FEED_SLOTS, j * FF_CHUNK
        return (pltpu.make_async_copy(wup_hbm.at[:, pl.ds(c0, FF_CHUNK)], st_a.at[slot], sems.at[0, slot]),
                pltpu.make_async_copy(wup_hbm.at[:, pl.ds(D_FF + c0, FF_CHUNK)], st_u.at[slot], sems.at[1, slot]),
                pltpu.make_async_copy(wdn_hbm.at[pl.ds(c0, FF_CHUNK), :], st_d.at[slot], sems.at[2, slot]))

    def before_chunk(j):
        if j == 0:
            for jj in range(min(FEED_SLOTS, N_FF_CHUNKS)):
                for cp in copies(jj):
                    cp.start()
        for cp in copies(j):
            cp.wait()
        slot, c0 = j % FEED_SLOTS, j * FF_CHUNK
        wup_ref[:, c0:c0 + FF_CHUNK] = st_a[slot].astype(BF16)
        wup_ref[:, D_FF + c0:D_FF + c0 + FF_CHUNK] = st_u[slot].astype(BF16)
        wdn_ref[c0:c0 + FF_CHUNK, :] = st_d[slot].astype(BF16)
        if j + FEED_SLOTS < N_FF_CHUNKS:
            for cp in copies(j + FEED_SLOTS):
                cp.start()

    return before_chunk


def _swiglu_sub(x, g, shift, scale, gate, wup_ref, wdn_ref, before_chunk=None):
    h = _norm_mod(x, g, shift, scale).astype(BF16)
    acc = jnp.zeros(x.shape, F32)
    for j in range(N_FF_CHUNKS):
        c0 = j * FF_CHUNK
        if before_chunk is not None:
            before_chunk(j)
        a = _dot(h, wup_ref[:, c0:c0 + FF_CHUNK])
        u = _dot(h, wup_ref[:, D_FF + c0:D_FF + c0 + FF_CHUNK])
        g_act = (a * jax.nn.sigmoid(a) * u).astype(BF16)
        acc = acc + _dot(g_act, wdn_ref[c0:c0 + FF_CHUNK, :])
    return x + 0.5 * gate * acc


def _write_stream(o_refs, y, is_ctx, split_out):
    if split_out:
        @pl.when(is_ctx)
        def _():
            o_refs[0][...] = y

        @pl.when(jnp.logical_not(is_ctx))
        def _():
            o_refs[1][...] = y
    else:
        o_refs[0][...] = y


def _ffn_kernel(*refs, layer, sub, split_in, split_out):
    n_x = 2 if split_in else 1
    mod_ref, g_ref, wup_hbm, wdn_hbm = refs[n_x:n_x + 4]
    n_out = 2 if split_out else 1
    o_refs = refs[n_x + 4:n_x + 4 + n_out]
    wup_ref, wdn_ref, *stage = refs[n_x + 4 + n_out:]
    is_ctx = pl.program_id(0) < N_CTX // TM_FFN
    mod_row = functools.partial(_mod_row, mod_ref, tm=TM_FFN)

    def run(before_chunk):
        x = jnp.where(is_ctx, refs[0][...], refs[1][...]) if split_in else refs[0][...]
        y = _swiglu_sub(x, g_ref[sub:sub + 1, :], mod_row(3 * sub), mod_row(3 * sub + 1), mod_row(3 * sub + 2),
                        wup_ref, wdn_ref, before_chunk)
        _write_stream(o_refs, y, is_ctx, split_out)

    @pl.when(pl.program_id(0) == 0)
    def _():
        run(_ffn_weight_feed(wup_hbm.at[layer], wdn_hbm.at[layer], wup_ref, wdn_ref, *stage))

    @pl.when(pl.program_id(0) != 0)
    def _():
        run(None)


def _ffn(xs, mod, g_norm, w_up, w_dn, l, sub, split_out=False):
    split_in = isinstance(xs, tuple)
    xs = xs if split_in else (xs,)
    tm = TM_FFN
    x_specs = [_ctx_spec(D_MODEL, tm), _lat_spec(D_MODEL, tm)] if split_in else [_tok_spec(D_MODEL, tm)]
    if split_out:
        out_specs = [_ctx_spec(D_MODEL, tm), _lat_spec(D_MODEL, tm)]
        out_shape = [jax.ShapeDtypeStruct((N_CTX, D_MODEL), F32), jax.ShapeDtypeStruct((N_LAT, D_MODEL), F32)]
    else:
        out_specs = _tok_spec(D_MODEL, tm)
        out_shape = jax.ShapeDtypeStruct((N_TOK, D_MODEL), F32)
    return pl.pallas_call(
        functools.partial(_ffn_kernel, layer=l, sub=sub, split_in=split_in, split_out=split_out),
        grid=(N_TOK // tm,),
        in_specs=x_specs + [
            _layer((MOD_ROWS, N_MOD * D_MODEL), l),
            _layer((3, D_MODEL), l),
            pl.BlockSpec(memory_space=pl.ANY),
            pl.BlockSpec(memory_space=pl.ANY),
        ],
        out_specs=out_specs,
        out_shape=out_shape,
        scratch_shapes=_ffn_weight_scratch(),
        compiler_params=_params(("arbitrary",)),
        name="ffn",
    )(*xs, mod, g_norm, w_up, w_dn)


N_MIXIN_IN = 12
N_MIXIN_OUT = 11
IN_CHUNK = 512
N_QKV_CHUNKS = QKV_COLS // IN_CHUNK
N_TAIL_CHUNKS = (N_GATE + 2 * D_MODEL) // IN_CHUNK
TAIL_COLS = (N_TAIL_CHUNKS + 1) * IN_CHUNK
TAIL_REST = N_GATE + 2 * D_MODEL - N_TAIL_CHUNKS * IN_CHUNK


def _mixin_weight_feed(win_hbm, wa_ref, wtail_ref, stage, sems):
    n_chunks = N_QKV_CHUNKS + N_TAIL_CHUNKS

    def copy(c):
        slot = c % FEED_SLOTS
        return pltpu.make_async_copy(win_hbm.at[:, pl.ds(c * IN_CHUNK, IN_CHUNK)], stage.at[slot], sems.at[slot])

    def feed(c):
        if c == 0:
            for cc in range(min(FEED_SLOTS, n_chunks)):
                copy(cc).start()
        copy(c).wait()
        block = stage[c % FEED_SLOTS].astype(BF16)
        if c < N_QKV_CHUNKS:
            wa_ref[:, c * IN_CHUNK:(c + 1) * IN_CHUNK] = block
        else:
            t = c - N_QKV_CHUNKS
            wtail_ref[:, t * IN_CHUNK:(t + 1) * IN_CHUNK] = block
        if c + FEED_SLOTS < n_chunks:
            copy(c + FEED_SLOTS).start()

    return feed


def _mixin_kernel(*refs, layer, n_alias):
    (x_ref, mod_ref, g_ref, win_hbm, wrest_ref, bg_ref, gq_ref, gk_ref, seg_ref, segt_ref,
     cos_ref, sin_ref) = refs[:N_MIXIN_IN]
    outs = refs[N_MIXIN_IN + n_alias:N_MIXIN_IN + n_alias + N_MIXIN_OUT]
    wa_ref, wtail_ref, wbr_ref, stage, sems = refs[N_MIXIN_IN + n_alias + N_MIXIN_OUT:]

    @pl.when(pl.program_id(0) == 0)
    def _():
        _mixin_body(x_ref, mod_ref, g_ref, wrest_ref, bg_ref, gq_ref, gk_ref, seg_ref, segt_ref, cos_ref, sin_ref,
                    outs, wa_ref, wtail_ref, wbr_ref,
                    _mixin_weight_feed(win_hbm.at[layer], wa_ref, wtail_ref, stage, sems))

    @pl.when(pl.program_id(0) != 0)
    def _():
        _mixin_body(x_ref, mod_ref, g_ref, wrest_ref, bg_ref, gq_ref, gk_ref, seg_ref, segt_ref, cos_ref, sin_ref,
                    outs, wa_ref, wtail_ref, wbr_ref, None)


def _mixin_body(x_ref, mod_ref, g_ref, wrest_ref, bg_ref, gq_ref, gk_ref, seg_ref, segt_ref, cos_ref, sin_ref,
                outs, wa_ref, wtail_ref, wbr_ref, feed):
    (qa_ref, ka_ref, va_ref, kc_ref, vc_ref, qb_ref, kb_ref, vb_ref, ob_ref, gt_ref, br_ref) = outs
    is_ctx = pl.program_id(0) < N_CTX // TM
    x = x_ref[...]
    h = _norm_mod(x, g_ref[1:2, :], _mod_row(mod_ref, 3), _mod_row(mod_ref, 4)).astype(BF16)

    def proj(c0, c1):
        if feed is not None:
            feed(c0 // IN_CHUNK)
        return _dot(h, wa_ref[:, c0:c1])

    def head_sums(p):
        hi, lo = _split2(p * p)
        return _dot(hi, seg_ref[...]) + _dot(lo, seg_ref[...])

    def head_scale(ssum):
        rh, rl = _split2(lax.rsqrt(ssum * (1.0 / HD_A) + EPS))
        return _dot(rh, segt_ref[...]) + _dot(rl, segt_ref[...])

    cosf = cos_ref[...]
    sinf = sin_ref[...]

    def rope(p):
        parts = []
        for hh in range(H_B):
            ph = p[:, HD_B * hh:HD_B * (hh + 1)]
            parts.append(ph * cosf + pltpu.roll(ph, HD_B // 2, 1) * sinf)
        return jnp.concatenate(parts, axis=1)

    cols = [0]
    for w in (A_WIDTH,) * 3 + (B_WIDTH,) * 4:
        cols.append(cols[-1] + w)
    p_qa = proj(cols[0], cols[1])
    p_ka = proj(cols[1], cols[2])
    s_qa = head_sums(p_qa)
    s_ka = head_sums(p_ka)
    va = proj(cols[2], cols[3])
    p_qb = proj(cols[3], cols[4])
    r_qa = head_scale(s_qa)
    r_ka = head_scale(s_ka)
    p_kb = proj(cols[4], cols[5])
    p_vb = proj(cols[5], cols[6])
    p_ob = proj(cols[6], cols[7])

    qa_ref[...] = (p_qa * r_qa * gq_ref[...] * ASCALE).astype(BF16)
    ka = p_ka * r_ka * gk_ref[...]
    ka_ref[...] = ka.astype(BF16)
    va_ref[...] = va.astype(BF16)

    @pl.when(is_ctx)
    def _():
        kc_ref[...] = ka.reshape(kc_ref.shape)
        vc_ref[...] = va.reshape(vc_ref.shape)

    qb_ref[...] = rope(p_qb).astype(BF16)
    kb_ref[...] = (rope(p_kb) * KSCALE).astype(BF16)
    vb_ref[...] = p_vb.astype(BF16)
    ob_ref[...] = jax.nn.sigmoid(p_ob).astype(BF16)
    if feed is not None:
        for t in range(N_TAIL_CHUNKS):
            feed(N_QKV_CHUNKS + t)
        wtail_ref[:, N_TAIL_CHUNKS * IN_CHUNK:N_TAIL_CHUNKS * IN_CHUNK + LANES] = wrest_ref[...]
        wbr_ref[...] = wtail_ref[:, N_GATE:N_GATE + 2 * D_MODEL]
    for j in range(2 * D_MODEL // 512):
        sl = slice(512 * j, 512 * (j + 1))
        br_ref[:, sl] = jax.nn.sigmoid(_dot(h, wbr_ref[:, sl])).astype(BF16)

    pg = _dot(h, wtail_ref[:, 0:LANES]) + bg_ref[...]
    ls = jnp.minimum(pg, 0.0) - jnp.log1p(jnp.exp(-jnp.abs(pg)))
    lane = lax.broadcasted_iota(jnp.int32, pg.shape, 1)
    is_f = ((lane >= H_B) & (lane < 2 * H_B)) | ((lane >= 3 * H_B) & (lane < 4 * H_B))
    gt_ref[...] = jnp.where(is_f, ls, pg)


def _mixin(x, mod, g_norm, w_in, w_rest, b_gt, gq, gk, seg, segt, cosf, sinf, l, caches=None):
    assert w_in.shape[-1] == QKV_COLS + N_GATE + 2 * D_MODEL and QKV_COLS % IN_CHUNK == 0
    tok = _tok_spec
    out = lambda w, dt: jax.ShapeDtypeStruct((N_TOK, w), dt)
    seqs = TM // CTX_LEN
    ctx_out = jax.ShapeDtypeStruct((N_CTX_B, DEPTH, CTX_LEN, A_WIDTH), F32)
    cache_spec = pl.BlockSpec((seqs, None, CTX_LEN, A_WIDTH),
                              lambda i: (jnp.minimum(i, N_CTX // TM - 1), l, 0, 0))
    caches = () if caches is None else tuple(caches)
    return pl.pallas_call(
        functools.partial(_mixin_kernel, layer=l, n_alias=len(caches)),
        grid=(N_TOK // TM,),
        input_output_aliases={N_MIXIN_IN + j: 3 + j for j in range(len(caches))},
        in_specs=[
            tok(D_MODEL),
            _layer((MOD_ROWS, N_MOD * D_MODEL), l),
            _layer((3, D_MODEL), l),
            pl.BlockSpec(memory_space=pl.ANY),
            _layer((D_MODEL, LANES), l),
            _layer((1, LANES), l),
            _layer((1, A_WIDTH), l),
            _layer((1, A_WIDTH), l),
            _resident((A_WIDTH, LANES)),
            _resident((LANES, A_WIDTH)),
            tok(HD_B),
            tok(HD_B),
        ] + [pl.BlockSpec(memory_space=pl.ANY)] * len(caches),
        out_specs=[tok(A_WIDTH), tok(A_WIDTH), tok(A_WIDTH), cache_spec, cache_spec,
                   tok(B_WIDTH), tok(B_WIDTH), tok(B_WIDTH), tok(B_WIDTH), tok(LANES), tok(2 * D_MODEL)],
        out_shape=[out(A_WIDTH, BF16), out(A_WIDTH, BF16), out(A_WIDTH, BF16), ctx_out, ctx_out,
                   out(B_WIDTH, BF16), out(B_WIDTH, BF16), out(B_WIDTH, BF16),
                   out(B_WIDTH, BF16), out(LANES, F32), out(2 * D_MODEL, BF16)],
        scratch_shapes=[pltpu.VMEM((D_MODEL, QKV_COLS), BF16),
                        pltpu.VMEM((D_MODEL, TAIL_COLS), BF16),
                        pltpu.VMEM((D_MODEL, 2 * D_MODEL), BF16),
                        pltpu.VMEM((FEED_SLOTS, D_MODEL, IN_CHUNK), F32),
                        pltpu.SemaphoreType.DMA((FEED_SLOTS,))],
        compiler_params=_params(("arbitrary",)),
        name="mixin",
    )(x, mod, g_norm, w_in, w_rest, b_gt, gq, gk, seg, segt, cosf, sinf, *caches)


CTX_SEQS = 2


def _head_pair_masks():
    lane = lax.broadcasted_iota(jnp.int32, (1, LANES), 1)
    first = lane < HD_A
    return first, [jnp.where(first, 1.0, 0.0).astype(BF16), jnp.where(first, 0.0, 1.0).astype(BF16)]


def _ctx_attn_kernel(q_ref, k_ref, v_ref, o_ref):
    first, keep = _head_pair_masks()
    q = q_ref[...]
    k = k_ref[...]
    v = v_ref[...]
    items = [(slice(CTX_LEN * sq, CTX_LEN * (sq + 1)), slice(LANES * t, LANES * (t + 1)), half)
             for sq in range(CTX_SEQS) for t in range(A_WIDTH // LANES) for half in range(2)]
    scores = [_dot_nt(q[rows, tile] * keep[half], k[rows, tile]) for rows, tile, half in items]
    probs = [jnp.exp(s - jnp.max(s, axis=-1, keepdims=True)) for s in scores]
    outs = [_dot(p.astype(BF16), v[rows, tile]) / jnp.sum(p, axis=-1, keepdims=True)
            for p, (rows, tile, half) in zip(probs, items)]
    for (rows, tile, _), o_first, o_second in zip(items[0::2], outs[0::2], outs[1::2]):
        o_ref[rows, tile] = jnp.where(first, o_first, o_second).astype(BF16)


def _ctx_attn(qa, ka, va):
    blk = pl.BlockSpec((CTX_SEQS * CTX_LEN, A_WIDTH), lambda b: (b, 0))
    return pl.pallas_call(
        _ctx_attn_kernel,
        grid=(N_CTX_B // CTX_SEQS,),
        in_specs=[blk, blk, blk],
        out_specs=blk,
        out_shape=jax.ShapeDtypeStruct((N_CTX, A_WIDTH), BF16),
        compiler_params=_params(("parallel",)),
        name="ctx_attn",
    )(qa, ka, va)


def _na_key_row0(j):
    return jnp.clip(NA_RB * j - WIN_H // 2, 0, LAT_ROWS - NA_KR)


def _na_kernel(q_ref, k_ref, v_ref, ck_ref, cv_ref, bias_ref, o_ref):
    j = pl.program_id(0)
    b = pl.program_id(1)
    base = pl.multiple_of(b * LAT_LEN + _na_key_row0(j) * GRID_W, GRID_W)
    kw = k_ref[pl.ds(base, NA_K), :]
    vw = v_ref[pl.ds(base, NA_K), :]
    ck = ck_ref[b].astype(BF16)
    cv = cv_ref[b].astype(BF16)
    q = q_ref[...]
    heads = [slice(HD_A * hh, HD_A * (hh + 1)) for hh in range(H_A)]
    s_win = [_dot_nt(q[:, sl], kw[:, sl]) for sl in heads]
    s_ctx = [_dot_nt(q[:, sl], ck[:, sl]) for sl in heads]
    probs = []
    for hh in range(H_A):
        sw = s_win[hh] + bias_ref[hh]
        sx = s_ctx[hh]
        m = jnp.maximum(jnp.max(sw, axis=-1, keepdims=True), jnp.max(sx, axis=-1, keepdims=True))
        pw = jnp.exp(sw - m)
        px = jnp.exp(sx - m)
        l = jnp.sum(pw, axis=-1, keepdims=True) + jnp.sum(px, axis=-1, keepdims=True)
        probs.append((pw.astype(BF16), px.astype(BF16), l))
    outs = [(_dot(pw, vw[:, sl]) + _dot(px, cv[:, sl])) / l for (pw, px, l), sl in zip(probs, heads)]
    o_ref[...] = jnp.concatenate(outs, axis=1).astype(BF16)


def _na_attn(qa, ka, va, ck, cv, bias, l):
    past = ck.shape[2]
    lat_kv = pl.BlockSpec((N_LAT, A_WIDTH), lambda j, b: (N_CTX // N_LAT, 0), pipeline_mode=pl.Buffered(1))
    cache = pl.BlockSpec((N_LAT_B, None, past, A_WIDTH), lambda j, b: (0, l, 0, 0), pipeline_mode=pl.Buffered(1))
    bias_class = lambda j: jnp.where(j == 0, 0, jnp.where(j == NA_BLOCKS - 1, 2, 1))
    return pl.pallas_call(
        _na_kernel,
        grid=(NA_BLOCKS, N_LAT_B),
        in_specs=[
            pl.BlockSpec((NA_Q, A_WIDTH), lambda j, b: (N_CTX // NA_Q + b * NA_BLOCKS + j, 0)),
            lat_kv, lat_kv, cache, cache,
            pl.BlockSpec((None, None, H_A, NA_Q, NA_K), lambda j, b: (l, bias_class(j), 0, 0, 0)),
        ],
        out_specs=pl.BlockSpec((NA_Q, A_WIDTH), lambda j, b: (b * NA_BLOCKS + j, 0)),
        out_shape=jax.ShapeDtypeStruct((N_LAT, A_WIDTH), BF16),
        compiler_params=_params(("arbitrary", "arbitrary")),
        name="na_attn",
    )(qa, ka, va, ck, cv, bias)


def _na_class_geometry():
    out = []
    for j in (0, 1, NA_BLOCKS - 1):
        r = NA_RB * j + np.arange(NA_RB)
        kr = int(np.clip(NA_RB * j - WIN_H // 2, 0, LAT_ROWS - NA_KR)) + np.arange(NA_KR)
        start_r = np.clip(r - WIN_H // 2, 0, LAT_ROWS - WIN_H)
        row_ok = (kr[None, :] >= start_r[:, None]) & (kr[None, :] < start_r[:, None] + WIN_H)
        dr = kr[None, :] - r[:, None] + (WIN_H - 1)
        out.append((dr, row_ok))
    return out


def _bias_kernel(r_ref, o_ref):
    qc = lax.broadcasted_iota(jnp.int32, (GRID_W, GRID_W), 0)
    kc = lax.broadcasted_iota(jnp.int32, (GRID_W, GRID_W), 1)
    c0 = jnp.clip(qc - WIN_W // 2, 0, GRID_W - WIN_W)
    col_ok = (kc >= c0) & (kc < c0 + WIN_W)
    neg = jnp.full((GRID_W, GRID_W), -1e30, F32)
    toep = []
    for dr in range(2 * WIN_H - 1):
        rows = jnp.broadcast_to(r_ref[0, 0, dr:dr + 1, :], (GRID_W, LANES))
        t = pltpu.roll(rows, 0, 1, stride=1, stride_axis=0)[:, :GRID_W]
        toep.append(jnp.where(col_ok, t, neg))
    for cls, (dr_tab, row_ok) in enumerate(_na_class_geometry()):
        for a in range(NA_RB):
            for jj in range(NA_KR):
                o_ref[0, cls, 0, GRID_W * a:GRID_W * (a + 1), GRID_W * jj:GRID_W * (jj + 1)] = (
                    toep[int(dr_tab[a, jj])] if row_ok[a, jj] else neg)


def _bias_tables(rpb):
    n_dr, n_dc = 2 * WIN_H - 1, 2 * WIN_W - 1
    assert rpb.shape == (DEPTH, H_A, n_dr, n_dc)
    rows = jnp.concatenate([rpb[..., WIN_W - 1:], jnp.zeros((DEPTH, H_A, n_dr, LANES - n_dc), F32),
                            rpb[..., :WIN_W - 1]], axis=-1)
    rows = jnp.pad(rows, ((0, 0), (0, 0), (0, 16 - n_dr), (0, 0)))
    return pl.pallas_call(
        _bias_kernel,
        grid=(DEPTH, H_A),
        in_specs=[pl.BlockSpec((1, 1, 16, LANES), lambda l, h: (l, h, 0, 0))],
        out_specs=pl.BlockSpec((1, 3, 1, NA_Q, NA_K), lambda l, h: (l, 0, h, 0, 0)),
        out_shape=jax.ShapeDtypeStruct((DEPTH, 3, H_A, NA_Q, NA_K), F32),
        compiler_params=_params(("arbitrary", "arbitrary")),
        name="na_bias",
    )(rows)


def _mlstm_kernel(*refs, n_chunks, has_init, want_state, n_alias):
    dirs = (refs[0:4], refs[4:8])
    pos = 8
    if has_init:
        c0_ref, n0_ref, m0_ref = refs[pos:pos + 3]
        pos += 3
    pos += n_alias
    n_h = 1 if n_chunks == 1 else 2
    h_refs = refs[pos:pos + n_h]
    pos += n_h
    if want_state:
        cf_ref, nf_ref, mf_ref = refs[pos:pos + 3]
        pos += 3
    cs_ref, ms_ref = refs[pos:pos + 2]
    c = pl.program_id(1)

    def tail_rows(x):
        r = lax.broadcasted_iota(jnp.int32, (STATE_TAIL, x.shape[1]), 0)
        return jnp.where(r == 0, x, jnp.where(r == 1, x - x.astype(BF16).astype(F32), 0.0))

    @pl.when(c == 0)
    def _():
        if has_init:
            cs_ref[:, 0:HD_B, :] = c0_ref[0]
            for i in range(N_STATE):
                cs_ref[i, HD_B:HD_B + STATE_TAIL, :] = tail_rows(n0_ref[0, i:i + 1, :])
            ms_ref[...] = m0_ref[0]
        else:
            cs_ref[...] = jnp.zeros(cs_ref.shape, F32)
            ms_ref[...] = jnp.zeros(ms_ref.shape, F32)

    row = lax.broadcasted_iota(jnp.int32, (CHUNK, CHUNK), 0)
    col = lax.broadcasted_iota(jnp.int32, (CHUNK, CHUNK), 1)
    chains = []
    for d in range(2):
        q_ref, k_ref, v_ref, g_ref = dirs[d]
        vis = (row <= col) if d == 0 else (row >= col)
        cum_b = jnp.where((col <= row) if d == 0 else (col >= row), 1.0, 0.0).astype(BF16)
        g_t = g_ref[...].T[0:N_GATE, :]
        r = _dot_nt(jnp.concatenate(_split3(g_t), axis=0), cum_b)
        b_t = r[0:N_GATE] + r[N_GATE:2 * N_GATE] + r[2 * N_GATE:3 * N_GATE]
        i0 = 2 * H_B * d
        u_t = g_t[i0:i0 + H_B, :] - b_t[i0 + H_B:i0 + 2 * H_B, :]
        u_cols = jnp.concatenate([u_t, jnp.zeros((LANES - H_B, CHUNK), F32)], axis=0).T
        q = q_ref[...]
        k = k_ref[...]
        v32 = v_ref[...].astype(F32)
        for hh in range(H_B):
            sl = slice(HD_B * hh, HD_B * (hh + 1))
            chains.append(dict(
                d=d, i=d * H_B + hh, sl=sl, vis=vis, qh=q[:, sl], kh=k[:, sl], v32=v32,
                u_col=u_cols[:, hh:hh + 1],
                b_row=b_t[i0 + H_B + hh:i0 + H_B + hh + 1, :],
                i_row=g_t[i0 + hh:i0 + hh + 1, :]))

    for ch in chains:
        ch["state"] = cs_ref[ch["i"]]
        ch["r"] = _dot_nt(jnp.concatenate([ch["kh"], ch["state"].astype(BF16)], axis=0), ch["qh"])
    for ch in chains:
        b_row, m_prev = ch["b_row"], ms_ref[ch["i"]:ch["i"] + 1, 0:1]
        dm = jnp.where(ch["vis"], ch["u_col"] + b_row, -jnp.inf)
        inter = b_row + m_prev
        m_t = jnp.maximum(inter, jnp.max(dm, axis=0, keepdims=True))
        ch["a"] = jnp.exp(inter - m_t)
        ch["w"] = jnp.exp(dm - m_t)
        ch["floor"] = jnp.exp(-m_t)
        b_end = jnp.broadcast_to(b_row[:, CHUNK - 1:CHUNK] if ch["d"] == 0 else b_row[:, 0:1], (1, CHUNK))
        gl = b_end - b_row + ch["i_row"]
        m_new = jnp.maximum(b_end + m_prev, jnp.max(gl, axis=-1, keepdims=True))
        ch["a_s"] = jnp.exp(b_end + m_prev - m_new)[:, 0:HD_B]
        ch["w_s"] = jnp.exp(gl - m_new)
        ch["m_new"] = m_new
    for ch in chains:
        r, a = ch["r"], ch["a"]
        p = r[0:CHUNK] * ch["w"]
        qn = r[CHUNK + HD_B:CHUNK + HD_B + 1] + r[CHUNK + HD_B + 1:CHUNK + HD_B + 2]
        den = a * qn + jnp.sum(p, axis=0, keepdims=True)
        ch["v_t"] = ch["v32"][:, ch["sl"]].T
        num = a * r[CHUNK:CHUNK + HD_B] + _dot(ch["v_t"].astype(BF16), p.astype(BF16))
        h_t = num * (1.0 / jnp.maximum(jnp.abs(den), ch["floor"]))
        if n_h == 2:
            h_refs[ch["d"]][:, ch["sl"]] = h_t.T
        else:
            ch["h_t"] = h_t
    if n_h == 1:
        for fwd_ch, bwd_ch in zip(chains[:H_B], chains[H_B:]):
            h_refs[0][:, fwd_ch["sl"]] = (fwd_ch["h_t"] + bwd_ch["h_t"]).T
    for ch in chains:
        i, a_s, state = ch["i"], ch["a_s"], ch["state"]
        lhs = jnp.concatenate([ch["v_t"] * ch["w_s"], tail_rows(ch["w_s"])], axis=0).astype(BF16)
        upd = _dot(lhs, ch["kh"])
        n_new = a_s * state[HD_B:HD_B + 1] + upd[HD_B:HD_B + 1] + upd[HD_B + 1:HD_B + 2]
        cs_ref[i, 0:HD_B, :] = a_s * state[0:HD_B] + upd[0:HD_B]
        cs_ref[i, HD_B:HD_B + STATE_TAIL, :] = tail_rows(n_new)
        ms_ref[i:i + 1, :] = ch["m_new"][:, 0:LANES]

    if want_state:
        @pl.when(c == n_chunks - 1)
        def _():
            cf_ref[0] = cs_ref[:, 0:HD_B, :]
            for i in range(N_STATE):
                nf_ref[0, i:i + 1, :] = cs_ref[i, HD_B:HD_B + 1, :]
            mf_ref[0] = ms_ref[...]


def _mlstm(qb, kb, vb, gt, row0, n_batch, seq_len, layer, init=None, want_state=False, c_prev=None):
    n_chunks = seq_len // CHUNK
    blk0 = row0 // CHUNK

    def fwd(w):
        return pl.BlockSpec((CHUNK, w), lambda b, c: (blk0 + b * n_chunks + c, 0))

    def bwd(w):
        return pl.BlockSpec((CHUNK, w), lambda b, c: (blk0 + b * n_chunks + n_chunks - 1 - c, 0))

    in_specs = [fwd(B_WIDTH), fwd(B_WIDTH), fwd(B_WIDTH), fwd(LANES),
                bwd(B_WIDTH), bwd(B_WIDTH), bwd(B_WIDTH), bwd(LANES)]
    args = [qb, kb, vb, gt, qb, kb, vb, gt]
    c_spec = pl.BlockSpec((1, None, N_STATE, HD_B, HD_B), lambda b, c: (b, layer, 0, 0, 0))
    if init is not None:
        in_specs += [c_spec,
                     pl.BlockSpec((1, None, N_STATE, LANES), lambda b, c: (b, layer, 0, 0)),
                     pl.BlockSpec((1, None, N_STATE, LANES), lambda b, c: (b, layer, 0, 0))]
        args += list(init)
    n_rows = n_batch * seq_len
    out_specs = [pl.BlockSpec((CHUNK, B_WIDTH), lambda b, c: (b * n_chunks + c, 0)),
                 pl.BlockSpec((CHUNK, B_WIDTH), lambda b, c: (b * n_chunks + n_chunks - 1 - c, 0))]
    if n_chunks == 1:
        out_specs = out_specs[:1]
    out_shape = [jax.ShapeDtypeStruct((n_rows, B_WIDTH), F32)] * len(out_specs)
    aliases = {}
    if c_prev is not None:
        aliases = {len(args): len(out_specs)}
        in_specs += [pl.BlockSpec(memory_space=pl.ANY)]
        args += [c_prev]
    if want_state:
        out_specs += [c_spec,
                      pl.BlockSpec((1, N_STATE, LANES), lambda b, c: (b, 0, 0)),
                      pl.BlockSpec((1, N_STATE, LANES), lambda b, c: (b, 0, 0))]
        out_shape += [jax.ShapeDtypeStruct((n_batch, DEPTH, N_STATE, HD_B, HD_B), F32),
                      jax.ShapeDtypeStruct((n_batch, N_STATE, LANES), F32),
                      jax.ShapeDtypeStruct((n_batch, N_STATE, LANES), F32)]
    return pl.pallas_call(
        functools.partial(_mlstm_kernel, n_chunks=n_chunks, has_init=init is not None, want_state=want_state,
                          n_alias=len(aliases)),
        grid=(n_batch, n_chunks),
        input_output_aliases=aliases,
        in_specs=in_specs,
        out_specs=out_specs,
        out_shape=out_shape,
        scratch_shapes=[pltpu.VMEM((N_STATE, HD_B + STATE_TAIL, HD_B), F32),
                        pltpu.VMEM((N_STATE, LANES), F32)],
        compiler_params=_params(("parallel", "arbitrary")),
        name="mlstm",
    )(*args)


def _mixout_ffn_kernel(*refs, split_out, n_hc, n_hl):
    x_ref, mod_ref, attc_ref, attl_ref = refs[:4]
    hc_refs = refs[4:4 + n_hc]
    hl_refs = refs[4 + n_hc:4 + n_hc + n_hl]
    n_in = 13 + n_hc + n_hl
    (ob_ref, br_ref, g_ref, ghn_ref, wpa_ref, wpb_ref, wo_ref, wup_ref,
     wdn_ref) = refs[4 + n_hc + n_hl:n_in]
    o_refs = refs[n_in:]
    is_ctx = pl.program_id(0) < N_CTX // TM
    att = jnp.where(is_ctx, attc_ref[...], attl_ref[...])
    hsum = jnp.where(is_ctx, sum(r[...] for r in hc_refs), sum(r[...] for r in hl_refs))
    parts = []
    for hh in range(H_B):
        seg = hsum[:, HD_B * hh:HD_B * (hh + 1)]
        parts.append(seg * lax.rsqrt(jnp.mean(seg * seg, axis=-1, keepdims=True) + EPS))
    hn = (jnp.concatenate(parts, axis=1) * ghn_ref[...]) * ob_ref[...].astype(F32)
    y = (br_ref[:, :D_MODEL].astype(F32) * _dot(att, wpa_ref[...])
         + br_ref[:, D_MODEL:].astype(F32) * _dot(hn.astype(BF16), wpb_ref[...]))
    x = x_ref[...] + _mod_row(mod_ref, 5) * _dot(y.astype(BF16), wo_ref[...])
    out = _swiglu_sub(x, g_ref[2:3, :], _mod_row(mod_ref, 6), _mod_row(mod_ref, 7), _mod_row(mod_ref, 8),
                      wup_ref, wdn_ref)
    _write_stream(o_refs, out, is_ctx, split_out)


def _mixout_ffn(x, mod, att_c, att_l, h_c, h_l, sob, sbr, g_norm, g_hn, w_pa, w_pb, w_o,
                w_up, w_dn, l, split_out=False):
    tok, ctx, lat = _tok_spec, _ctx_spec, _lat_spec
    if split_out:
        out_specs = [ctx(D_MODEL), lat(D_MODEL)]
        out_shape = [jax.ShapeDtypeStruct((N_CTX, D_MODEL), F32), jax.ShapeDtypeStruct((N_LAT, D_MODEL), F32)]
    else:
        out_specs = tok(D_MODEL)
        out_shape = jax.ShapeDtypeStruct((N_TOK, D_MODEL), F32)
    return pl.pallas_call(
        functools.partial(_mixout_ffn_kernel, split_out=split_out, n_hc=len(h_c), n_hl=len(h_l)),
        grid=(N_TOK // TM,),
        in_specs=[
            tok(D_MODEL),
            _layer((MOD_ROWS, N_MOD * D_MODEL), l),
            ctx(A_WIDTH), lat(A_WIDTH),
        ] + [ctx(B_WIDTH)] * len(h_c) + [lat(B_WIDTH)] * len(h_l) + [
            tok(B_WIDTH), tok(2 * D_MODEL),
            _layer((3, D_MODEL), l),
            _layer((1, B_WIDTH), l),
            _layer((A_WIDTH, D_MODEL), l),
            _layer((B_WIDTH, D_MODEL), l),
            _layer((D_MODEL, D_MODEL), l),
            _layer((D_MODEL, 2 * D_FF), l),
            _layer((D_FF, D_MODEL), l),
        ],
        out_specs=out_specs,
        out_shape=out_shape,
        compiler_params=_params(("arbitrary",)),
        name="mixout_ffn",
    )(x, mod, att_c, att_l, *h_c, *h_l, sob, sbr, g_norm, g_hn, w_pa, w_pb, w_o, w_up, w_dn)


def _rope_tables():
    half = HD_B // 2
    nf = half // 2
    inv = ROPE_THETA ** (-jnp.arange(nf, dtype=F32) / nf)
    t = jnp.arange(LAT_LEN)
    rowp = (t // GRID_W).astype(F32)
    colp = (t % GRID_W).astype(F32)
    ang = jnp.concatenate([rowp[:, None] * inv, colp[:, None] * inv], axis=-1)
    cos, sin = jnp.cos(ang), jnp.sin(ang)
    cos_l = jnp.tile(jnp.concatenate([cos, cos], axis=-1), (N_LAT_B, 1))
    sin_l = jnp.tile(jnp.concatenate([-sin, sin], axis=-1), (N_LAT_B, 1))
    cosf = jnp.concatenate([jnp.ones((N_CTX, HD_B), F32), cos_l], axis=0)
    sinf = jnp.concatenate([jnp.zeros((N_CTX, HD_B), F32), sin_l], axis=0)
    return cosf, sinf


def kernel(x_prompt, x_sample, cache_k, cache_v, state_C, state_n, state_m, c, c_ctx, w_ada, b_ada, g_norm,
           w_up1, w_dn1, w_in, b_gates, g_qn, g_kn, rpb, g_hn, w_pa, w_pb, w_o, w_up2, w_dn2):
    assert x_prompt.shape == (N_CTX_B, CTX_LEN, D_MODEL) and x_sample.shape == (N_LAT_B, LAT_LEN, D_MODEL)
    past_len = cache_k.shape[2]

    w_rest = jnp.pad(w_in[..., w_in.shape[-1] - TAIL_REST:], ((0, 0), (0, 0), (0, LANES - TAIL_REST))).astype(BF16)
    b_gt = jnp.pad(b_gates, ((0, 0), (0, LANES - N_GATE))).reshape(DEPTH, 1, LANES)
    w_pab, w_pbb, w_ob, w_up2b, w_dn2b = (w.astype(BF16) for w in (w_pa, w_pb, w_o, w_up2, w_dn2))
    gq = jnp.tile(g_qn, (1, H_A)).reshape(DEPTH, 1, A_WIDTH)
    gk = jnp.tile(g_kn, (1, H_A)).reshape(DEPTH, 1, A_WIDTH)
    ghn = g_hn.reshape(DEPTH, 1, B_WIDTH)
    seg_np = (np.arange(A_WIDTH)[:, None] // HD_A == np.arange(LANES)[None, :]).astype(np.float32)
    seg = jnp.asarray(seg_np, BF16)
    segt = jnp.asarray(seg_np.T, BF16)
    cosf, sinf = _rope_tables()
    bias = _bias_tables(rpb)
    ck = cache_k.reshape(N_LAT_B, DEPTH, past_len, A_WIDTH)
    cv = cache_v.reshape(N_LAT_B, DEPTH, past_len, A_WIDTH)
    init = (state_C.reshape(N_LAT_B, DEPTH, N_STATE, HD_B, HD_B),
            state_n.reshape(N_LAT_B, DEPTH, N_STATE, HD_B),
            jnp.broadcast_to(state_m.reshape(N_LAT_B, DEPTH, N_STATE, 1), (N_LAT_B, DEPTH, N_STATE, LANES)))

    cond = jnp.concatenate([c_ctx[None], c, jnp.zeros((MOD_ROWS - 1 - N_LAT_B, D_MODEL), F32)], axis=0)
    mod = _ada(cond, w_ada, b_ada)

    x = (x_prompt.reshape(N_CTX, D_MODEL), x_sample.reshape(N_LAT, D_MODEL))
    caches = tuple(jnp.zeros((N_CTX_B, DEPTH, CTX_LEN, A_WIDTH), F32) for _ in range(2))
    c_new = jnp.zeros((N_CTX_B, DEPTH, N_STATE, HD_B, HD_B), F32)
    ns_out, ms_out = [], []
    for l in range(DEPTH):
        x = _ffn(x, mod, g_norm, w_up1, w_dn1, l, 0)
        qa, ka, va, k_new, v_new, qb, kb, vb, sob, gt, sbr = _mixin(
            x, mod, g_norm, w_in, w_rest, b_gt, gq, gk, seg, segt, cosf, sinf, l, caches)
        caches = (k_new, v_new)
        att_c = _ctx_attn(qa, ka, va)
        att_l = _na_attn(qa, ka, va, ck, cv, bias, l)
        *h_c, c_new, n_fin, m_fin = _mlstm(qb, kb, vb, gt, 0, N_CTX_B, CTX_LEN, l, want_state=True, c_prev=c_new)
        h_l = _mlstm(qb, kb, vb, gt, N_CTX, N_LAT_B, LAT_LEN, l, init=init)
        x = _mixout_ffn(x, mod, att_c, att_l, h_c, h_l, sob, sbr, g_norm, ghn, w_pab, w_pbb, w_ob,
                        w_up2b, w_dn2b, l, split_out=(l == DEPTH - 1))
        ns_out.append(n_fin.reshape(N_CTX_B, 2, H_B, HD_B))
        ms_out.append(m_fin[:, :, 0].reshape(N_CTX_B, 2, H_B))

    return (x[0].reshape(N_CTX_B, CTX_LEN, D_MODEL),
            x[1].reshape(N_LAT_B, LAT_LEN, D_MODEL),
            k_new.reshape(N_CTX_B, DEPTH, CTX_LEN, H_A, HD_A),
            v_new.reshape(N_CTX_B, DEPTH, CTX_LEN, H_A, HD_A),
            c_new.reshape(N_CTX_B, DEPTH, 2, H_B, HD_B, HD_B),
            jnp.stack(ns_out, axis=1), jnp.stack(ms_out, axis=1))
```

```python
import functools

import jax
import jax.numpy as jnp
import numpy as np
from jax import lax
from jax.experimental import pallas as pl
from jax.experimental.pallas import tpu as pltpu

F32 = jnp.float32
BF16 = jnp.bfloat16

D_MODEL = 1024
DEPTH = 4
N_CTX_B, CTX_LEN = 32, 256
N_LAT_B, LAT_LEN = 2, 2048
N_CTX = N_CTX_B * CTX_LEN
N_LAT = N_LAT_B * LAT_LEN
N_TOK = N_CTX + N_LAT
GRID_W, WIN_H, WIN_W = 64, 8, 16
LAT_ROWS = LAT_LEN // GRID_W
H_A, HD_A = 8, 64
H_B, HD_B = 4, 128
A_WIDTH = H_A * HD_A
B_WIDTH = H_B * HD_B
D_FF = 2816
N_MOD = 9
CHUNK = 256
ROPE_THETA = 10000.0
EPS = 1e-6
KSCALE = HD_B ** -0.5
ASCALE = HD_A ** -0.5
N_GATE = 4 * H_B
QKV_COLS = 3 * A_WIDTH + 4 * B_WIDTH
N_STATE = 2 * H_B
STATE_TAIL = 16
MOD_ROWS = 8

LANES = 128
TM = 512
TM_FFN = 512
FF_CHUNK = 256
VMEM_LIMIT = 52 * 1024 * 1024

NA_RB = WIN_H // 2
NA_KR = NA_RB + WIN_H - 1
NA_Q = NA_RB * GRID_W
NA_K = NA_KR * GRID_W
NA_BLOCKS = LAT_ROWS // NA_RB
assert LAT_ROWS % NA_RB == 0 and NA_BLOCKS >= 3 and N_CTX % N_LAT == 0 and 1 + N_LAT_B <= MOD_ROWS


def _dot(a, b):
    return jnp.dot(a, b, preferred_element_type=F32)


def _dot_nt(a, b):
    return lax.dot_general(a, b, (((1,), (1,)), ((), ())), preferred_element_type=F32)


def _split2(x):
    hi = x.astype(BF16)
    lo = (x - hi.astype(F32)).astype(BF16)
    return hi, lo


def _split3(x):
    hi = x.astype(BF16)
    r1 = x - hi.astype(F32)
    mid = r1.astype(BF16)
    lo = (r1 - mid.astype(F32)).astype(BF16)
    return hi, mid, lo


def _norm_mod(x, g, shift, scale):
    ms = jnp.mean(x * x, axis=-1, keepdims=True)
    return (x * lax.rsqrt(ms + EPS) * g) * (1.0 + scale) + shift


def _params(sem):
    return pltpu.CompilerParams(dimension_semantics=sem, vmem_limit_bytes=VMEM_LIMIT)


def _resident(shape):
    nd = len(shape)
    return pl.BlockSpec(shape, lambda *_: (0,) * nd, pipeline_mode=pl.Buffered(1))


def _layer(shape, l):
    nd = len(shape)
    return pl.BlockSpec((None,) + tuple(shape), lambda *_: (l,) + (0,) * nd, pipeline_mode=pl.Buffered(1))


def _tok_spec(w, tm=TM):
    return pl.BlockSpec((tm, w), lambda i: (i, 0))


def _ctx_spec(w, tm=TM):
    return pl.BlockSpec((tm, w), lambda i: (jnp.minimum(i, N_CTX // tm - 1), 0))


def _lat_spec(w, tm=TM):
    return pl.BlockSpec((tm, w), lambda i: (jnp.maximum(i - N_CTX // tm, 0), 0))


def _mod_row(mod_ref, k, tm=TM):
    i = pl.program_id(0)
    n_ctx_tiles = N_CTX // tm
    mod_set = jnp.where(i < n_ctx_tiles, 0, 1 + (i - n_ctx_tiles) // (LAT_LEN // tm))
    return mod_ref[pl.ds(mod_set, 1), D_MODEL * k:D_MODEL * (k + 1)]


def _ada_kernel(c_ref, w_ref, b_ref, o_ref):
    c = c_ref[...]
    s = c * jax.nn.sigmoid(c)
    sh, sl = _split2(s)
    wh, wl = _split2(w_ref[0])
    o_ref[0] = _dot(sh, wh) + _dot(sl, wh) + _dot(sh, wl) + b_ref[0]


def _ada(cond, w_ada, b_ada):
    tn = 2304
    n_out = N_MOD * D_MODEL
    return pl.pallas_call(
        _ada_kernel,
        grid=(DEPTH, n_out // tn),
        in_specs=[
            pl.BlockSpec((MOD_ROWS, D_MODEL), lambda l, j: (0, 0)),
            pl.BlockSpec((1, D_MODEL, tn), lambda l, j: (l, 0, j)),
            pl.BlockSpec((1, 1, tn), lambda l, j: (l, 0, j)),
        ],
        out_specs=pl.BlockSpec((1, MOD_ROWS, tn), lambda l, j: (l, 0, j)),
        out_shape=jax.ShapeDtypeStruct((DEPTH, MOD_ROWS, n_out), F32),
        compiler_params=_params(("parallel", "parallel")),
        name="ada",
    )(cond, w_ada, b_ada.reshape(DEPTH, 1, n_out))


N_FF_CHUNKS = D_FF // FF_CHUNK


FEED_SLOTS = 2


def _ffn_weight_scratch():
    return [pltpu.VMEM((D_MODEL, 2 * D_FF), BF16),
            pltpu.VMEM((D_FF, D_MODEL), BF16),
            pltpu.VMEM((FEED_SLOTS, D_MODEL, FF_CHUNK), F32),
            pltpu.VMEM((FEED_SLOTS, D_MODEL, FF_CHUNK), F32),
            pltpu.VMEM((FEED_SLOTS, FF_CHUNK, D_MODEL), F32),
            pltpu.SemaphoreType.DMA((3, FEED_SLOTS))]


def _ffn_weight_feed(wup_hbm, wdn_hbm, wup_ref, wdn_ref, st_a, st_u, st_d, sems):
    def copies(j):
        slot, c0 = j % FEED_SLOTS, j * FF_CHUNK
        return (pltpu.make_async_copy(wup_hbm.at[:, pl.ds(c0, FF_CHUNK)], st_a.at[slot], sems.at[0, slot]),
                pltpu.make_async_copy(wup_hbm.at[:, pl.ds(D_FF + c0, FF_CHUNK)], st_u.at[slot], sems.at[1, slot]),
                pltpu.make_async_copy(wdn_hbm.at[pl.ds(c0, FF_CHUNK), :], st_d.at[slot], sems.at[2, slot]))

    def before_chunk(j):
        if j == 0:
            for jj in range(min(FEED_SLOTS, N_FF_CHUNKS)):
                for cp in copies(jj):
                    cp.start()
        for cp in copies(j):
            cp.wait()
        slot, c0 = j % FEED_SLOTS, j * FF_CHUNK
        wup_ref[:, c0:c0 + FF_CHUNK] = st_a[slot].astype(BF16)
        wup_ref[:, D_FF + c0:D_FF + c0 + FF_CHUNK] = st_u[slot].astype(BF16)
        wdn_ref[c0:c0 + FF_CHUNK, :] = st_d[slot].astype(BF16)
        if j + FEED_SLOTS < N_FF_CHUNKS:
            for cp in copies(j + FEED_SLOTS):
                cp.start()

    return before_chunk


def _swiglu_sub(x, g, shift, scale, gate, wup_ref, wdn_ref, before_chunk=None):
    h = _norm_mod(x, g, shift, scale).astype(BF16)
    acc = jnp.zeros(x.shape, F32)
    for j in range(N_FF_CHUNKS):
        c0 = j * FF_CHUNK
        if before_chunk is not None:
            before_chunk(j)
        a = _dot(h, wup_ref[:, c0:c0 + FF_CHUNK])
        u = _dot(h, wup_ref[:, D_FF + c0:D_FF + c0 + FF_CHUNK])
        g_act = (a * jax.nn.sigmoid(a) * u).astype(BF16)
        acc = acc + _dot(g_act, wdn_ref[c0:c0 + FF_CHUNK, :])
    return x + 0.5 * gate * acc


def _write_stream(o_refs, y, is_ctx, split_out):
    if split_out:
        @pl.when(is_ctx)
        def _():
            o_refs[0][...] = y

        @pl.when(jnp.logical_not(is_ctx))
        def _():
            o_refs[1][...] = y
    else:
        o_refs[0][...] = y


def _ffn_kernel(*refs, layer, sub, split_in, split_out):
    n_x = 2 if split_in else 1
    mod_ref, g_ref, wup_hbm, wdn_hbm = refs[n_x:n_x + 4]
    n_out = 2 if split_out else 1
    o_refs = refs[n_x + 4:n_x + 4 + n_out]
    wup_ref, wdn_ref, *stage = refs[n_x + 4 + n_out:]
    is_ctx = pl.program_id(0) < N_CTX // TM_FFN
    mod_row = functools.partial(_mod_row, mod_ref, tm=TM_FFN)

    def run(before_chunk):
        x = jnp.where(is_ctx, refs[0][...], refs[1][...]) if split_in else refs[0][...]
        y = _swiglu_sub(x, g_ref[sub:sub + 1, :], mod_row(3 * sub), mod_row(3 * sub + 1), mod_row(3 * sub + 2),
                        wup_ref, wdn_ref, before_chunk)
        _write_stream(o_refs, y, is_ctx, split_out)

    @pl.when(pl.program_id(0) == 0)
    def _():
        run(_ffn_weight_feed(wup_hbm.at[layer], wdn_hbm.at[layer], wup_ref, wdn_ref, *stage))

    @pl.when(pl.program_id(0) != 0)
    def _():
        run(None)


def _ffn(xs, mod, g_norm, w_up, w_dn, l, sub, split_out=False):
    split_in = isinstance(xs, tuple)
    xs = xs if split_in else (xs,)
    tm = TM_FFN
    x_specs = [_ctx_spec(D_MODEL, tm), _lat_spec(D_MODEL, tm)] if split_in else [_tok_spec(D_MODEL, tm)]
    if split_out:
        out_specs = [_ctx_spec(D_MODEL, tm), _lat_spec(D_MODEL, tm)]
        out_shape = [jax.ShapeDtypeStruct((N_CTX, D_MODEL), F32), jax.ShapeDtypeStruct((N_LAT, D_MODEL), F32)]
    else:
        out_specs = _tok_spec(D_MODEL, tm)
        out_shape = jax.ShapeDtypeStruct((N_TOK, D_MODEL), F32)
    return pl.pallas_call(
        functools.partial(_ffn_kernel, layer=l, sub=sub, split_in=split_in, split_out=split_out),
        grid=(N_TOK // tm,),
        in_specs=x_specs + [
            _layer((MOD_ROWS, N_MOD * D_MODEL), l),
            _layer((3, D_MODEL), l),
            pl.BlockSpec(memory_space=pl.ANY),
            pl.BlockSpec(memory_space=pl.ANY),
        ],
        out_specs=out_specs,
        out_shape=out_shape,
        scratch_shapes=_ffn_weight_scratch(),
        compiler_params=_params(("arbitrary",)),
        name="ffn",
    )(*xs, mod, g_norm, w_up, w_dn)


N_MIXIN_IN = 12
N_MIXIN_OUT = 11


def _mixin_kernel(*refs, n_alias):
    (x_ref, mod_ref, g_ref, wa_ref, wtail_ref, bg_ref, gq_ref, gk_ref, seg_ref, segt_ref,
     cos_ref, sin_ref) = refs[:N_MIXIN_IN]
    (qa_ref, ka_ref, va_ref, kc_ref, vc_ref, qb_ref, kb_ref, vb_ref, ob_ref, gt_ref,
     br_ref) = refs[N_MIXIN_IN + n_alias:N_MIXIN_IN + n_alias + N_MIXIN_OUT]
    wbr_ref = refs[N_MIXIN_IN + n_alias + N_MIXIN_OUT]
    is_ctx = pl.program_id(0) < N_CTX // TM

    @pl.when(pl.program_id(0) == 0)
    def _():
        wbr_ref[...] = wtail_ref[:, N_GATE:N_GATE + 2 * D_MODEL]

    x = x_ref[...]
    h = _norm_mod(x, g_ref[1:2, :], _mod_row(mod_ref, 3), _mod_row(mod_ref, 4)).astype(BF16)

    def proj(c0, c1):
        return _dot(h, wa_ref[:, c0:c1])

    def head_sums(p):
        hi, lo = _split2(p * p)
        return _dot(hi, seg_ref[...]) + _dot(lo, seg_ref[...])

    def head_scale(ssum):
        rh, rl = _split2(lax.rsqrt(ssum * (1.0 / HD_A) + EPS))
        return _dot(rh, segt_ref[...]) + _dot(rl, segt_ref[...])

    cosf = cos_ref[...]
    sinf = sin_ref[...]

    def rope(p):
        parts = []
        for hh in range(H_B):
            ph = p[:, HD_B * hh:HD_B * (hh + 1)]
            parts.append(ph * cosf + pltpu.roll(ph, HD_B // 2, 1) * sinf)
        return jnp.concatenate(parts, axis=1)

    cols = [0]
    for w in (A_WIDTH,) * 3 + (B_WIDTH,) * 4:
        cols.append(cols[-1] + w)
    p_qa = proj(cols[0], cols[1])
    p_ka = proj(cols[1], cols[2])
    s_qa = head_sums(p_qa)
    s_ka = head_sums(p_ka)
    va = proj(cols[2], cols[3])
    p_qb = proj(cols[3], cols[4])
    r_qa = head_scale(s_qa)
    r_ka = head_scale(s_ka)
    p_kb = proj(cols[4], cols[5])
    p_vb = proj(cols[5], cols[6])
    p_ob = proj(cols[6], cols[7])

    qa_ref[...] = (p_qa * r_qa * gq_ref[...] * ASCALE).astype(BF16)
    ka = p_ka * r_ka * gk_ref[...]
    ka_ref[...] = ka.astype(BF16)
    va_ref[...] = va.astype(BF16)

    @pl.when(is_ctx)
    def _():
        kc_ref[...] = ka.reshape(kc_ref.shape)
        vc_ref[...] = va.reshape(vc_ref.shape)

    qb_ref[...] = rope(p_qb).astype(BF16)
    kb_ref[...] = (rope(p_kb) * KSCALE).astype(BF16)
    vb_ref[...] = p_vb.astype(BF16)
    ob_ref[...] = jax.nn.sigmoid(p_ob).astype(BF16)
    for j in range(2 * D_MODEL // 512):
        sl = slice(512 * j, 512 * (j + 1))
        br_ref[:, sl] = jax.nn.sigmoid(_dot(h, wbr_ref[:, sl])).astype(BF16)

    pg = _dot(h, wtail_ref[:, 0:LANES]) + bg_ref[...]
    ls = jnp.minimum(pg, 0.0) - jnp.log1p(jnp.exp(-jnp.abs(pg)))
    lane = lax.broadcasted_iota(jnp.int32, pg.shape, 1)
    is_f = ((lane >= H_B) & (lane < 2 * H_B)) | ((lane >= 3 * H_B) & (lane < 4 * H_B))
    gt_ref[...] = jnp.where(is_f, ls, pg)


def _mixin(x, mod, g_norm, w_in, b_gt, gq, gk, seg, segt, cosf, sinf, l, caches=None):
    assert N_GATE + 2 * D_MODEL <= QKV_COLS and w_in.shape[-1] == QKV_COLS + N_GATE + 2 * D_MODEL
    w_block = lambda j: pl.BlockSpec((None, D_MODEL, QKV_COLS), lambda i: (l, 0, j), pipeline_mode=pl.Buffered(1))
    tok = _tok_spec
    out = lambda w, dt: jax.ShapeDtypeStruct((N_TOK, w), dt)
    seqs = TM // CTX_LEN
    ctx_out = jax.ShapeDtypeStruct((N_CTX_B, DEPTH, CTX_LEN, A_WIDTH), F32)
    cache_spec = pl.BlockSpec((seqs, None, CTX_LEN, A_WIDTH),
                              lambda i: (jnp.minimum(i, N_CTX // TM - 1), l, 0, 0))
    caches = () if caches is None else tuple(caches)
    return pl.pallas_call(
        functools.partial(_mixin_kernel, n_alias=len(caches)),
        grid=(N_TOK // TM,),
        input_output_aliases={N_MIXIN_IN + j: 3 + j for j in range(len(caches))},
        in_specs=[
            tok(D_MODEL),
            _layer((MOD_ROWS, N_MOD * D_MODEL), l),
            _layer((3, D_MODEL), l),
            w_block(0),
            w_block(1),
            _layer((1, LANES), l),
            _layer((1, A_WIDTH), l),
            _layer((1, A_WIDTH), l),
            _resident((A_WIDTH, LANES)),
            _resident((LANES, A_WIDTH)),
            tok(HD_B),
            tok(HD_B),
        ] + [pl.BlockSpec(memory_space=pl.ANY)] * len(caches),
        out_specs=[tok(A_WIDTH), tok(A_WIDTH), tok(A_WIDTH), cache_spec, cache_spec,
                   tok(B_WIDTH), tok(B_WIDTH), tok(B_WIDTH), tok(B_WIDTH), tok(LANES), tok(2 * D_MODEL)],
        out_shape=[out(A_WIDTH, BF16), out(A_WIDTH, BF16), out(A_WIDTH, BF16), ctx_out, ctx_out,
                   out(B_WIDTH, BF16), out(B_WIDTH, BF16), out(B_WIDTH, BF16),
                   out(B_WIDTH, BF16), out(LANES, F32), out(2 * D_MODEL, BF16)],
        scratch_shapes=[pltpu.VMEM((D_MODEL, 2 * D_MODEL), BF16)],
        compiler_params=_params(("arbitrary",)),
        name="mixin",
    )(x, mod, g_norm, w_in, w_in, b_gt, gq, gk, seg, segt, cosf, sinf, *caches)


CTX_SEQS = 2


def _head_pair_masks():
    lane = lax.broadcasted_iota(jnp.int32, (1, LANES), 1)
    first = lane < HD_A
    return first, [jnp.where(first, 1.0, 0.0).astype(BF16), jnp.where(first, 0.0, 1.0).astype(BF16)]


def _ctx_attn_kernel(q_ref, k_ref, v_ref, o_ref):
    first, keep = _head_pair_masks()
    q = q_ref[...]
    k = k_ref[...]
    v = v_ref[...]
    items = [(slice(CTX_LEN * sq, CTX_LEN * (sq + 1)), slice(LANES * t, LANES * (t + 1)), half)
             for sq in range(CTX_SEQS) for t in range(A_WIDTH // LANES) for half in range(2)]
    scores = [_dot_nt(q[rows, tile] * keep[half], k[rows, tile]) for rows, tile, half in items]
    probs = [jnp.exp(s - jnp.max(s, axis=-1, keepdims=True)) for s in scores]
    outs = [_dot(p.astype(BF16), v[rows, tile]) / jnp.sum(p, axis=-1, keepdims=True)
            for p, (rows, tile, half) in zip(probs, items)]
    for (rows, tile, _), o_first, o_second in zip(items[0::2], outs[0::2], outs[1::2]):
        o_ref[rows, tile] = jnp.where(first, o_first, o_second).astype(BF16)


def _ctx_attn(qa, ka, va):
    blk = pl.BlockSpec((CTX_SEQS * CTX_LEN, A_WIDTH), lambda b: (b, 0))
    return pl.pallas_call(
        _ctx_attn_kernel,
        grid=(N_CTX_B // CTX_SEQS,),
        in_specs=[blk, blk, blk],
        out_specs=blk,
        out_shape=jax.ShapeDtypeStruct((N_CTX, A_WIDTH), BF16),
        compiler_params=_params(("parallel",)),
        name="ctx_attn",
    )(qa, ka, va)


def _na_key_row0(j):
    return jnp.clip(NA_RB * j - WIN_H // 2, 0, LAT_ROWS - NA_KR)


def _na_kernel(q_ref, k_ref, v_ref, ck_ref, cv_ref, bias_ref, o_ref):
    j = pl.program_id(0)
    b = pl.program_id(1)
    base = pl.multiple_of(b * LAT_LEN + _na_key_row0(j) * GRID_W, GRID_W)
    kw = k_ref[pl.ds(base, NA_K), :]
    vw = v_ref[pl.ds(base, NA_K), :]
    ck = ck_ref[b].astype(BF16)
    cv = cv_ref[b].astype(BF16)
    q = q_ref[...]
    heads = [slice(HD_A * hh, HD_A * (hh + 1)) for hh in range(H_A)]
    s_win = [_dot_nt(q[:, sl], kw[:, sl]) for sl in heads]
    s_ctx = [_dot_nt(q[:, sl], ck[:, sl]) for sl in heads]
    probs = []
    for hh in range(H_A):
        sw = s_win[hh] + bias_ref[hh]
        sx = s_ctx[hh]
        m = jnp.maximum(jnp.max(sw, axis=-1, keepdims=True), jnp.max(sx, axis=-1, keepdims=True))
        pw = jnp.exp(sw - m)
        px = jnp.exp(sx - m)
        l = jnp.sum(pw, axis=-1, keepdims=True) + jnp.sum(px, axis=-1, keepdims=True)
        probs.append((pw.astype(BF16), px.astype(BF16), l))
    outs = [(_dot(pw, vw[:, sl]) + _dot(px, cv[:, sl])) / l for (pw, px, l), sl in zip(probs, heads)]
    o_ref[...] = jnp.concatenate(outs, axis=1).astype(BF16)


def _na_attn(qa, ka, va, ck, cv, bias, l):
    past = ck.shape[2]
    lat_kv = pl.BlockSpec((N_LAT, A_WIDTH), lambda j, b: (N_CTX // N_LAT, 0), pipeline_mode=pl.Buffered(1))
    cache = pl.BlockSpec((N_LAT_B, None, past, A_WIDTH), lambda j, b: (0, l, 0, 0), pipeline_mode=pl.Buffered(1))
    bias_class = lambda j: jnp.where(j == 0, 0, jnp.where(j == NA_BLOCKS - 1, 2, 1))
    return pl.pallas_call(
        _na_kernel,
        grid=(NA_BLOCKS, N_LAT_B),
        in_specs=[
            pl.BlockSpec((NA_Q, A_WIDTH), lambda j, b: (N_CTX // NA_Q + b * NA_BLOCKS + j, 0)),
            lat_kv, lat_kv, cache, cache,
            pl.BlockSpec((None, None, H_A, NA_Q, NA_K), lambda j, b: (l, bias_class(j), 0, 0, 0)),
        ],
        out_specs=pl.BlockSpec((NA_Q, A_WIDTH), lambda j, b: (b * NA_BLOCKS + j, 0)),
        out_shape=jax.ShapeDtypeStruct((N_LAT, A_WIDTH), BF16),
        compiler_params=_params(("arbitrary", "arbitrary")),
        name="na_attn",
    )(qa, ka, va, ck, cv, bias)


def _na_class_geometry():
    out = []
    for j in (0, 1, NA_BLOCKS - 1):
        r = NA_RB * j + np.arange(NA_RB)
        kr = int(np.clip(NA_RB * j - WIN_H // 2, 0, LAT_ROWS - NA_KR)) + np.arange(NA_KR)
        start_r = np.clip(r - WIN_H // 2, 0, LAT_ROWS - WIN_H)
        row_ok = (kr[None, :] >= start_r[:, None]) & (kr[None, :] < start_r[:, None] + WIN_H)
        dr = kr[None, :] - r[:, None] + (WIN_H - 1)
        out.append((dr, row_ok))
    return out


def _bias_kernel(r_ref, o_ref):
    qc = lax.broadcasted_iota(jnp.int32, (GRID_W, GRID_W), 0)
    kc = lax.broadcasted_iota(jnp.int32, (GRID_W, GRID_W), 1)
    c0 = jnp.clip(qc - WIN_W // 2, 0, GRID_W - WIN_W)
    col_ok = (kc >= c0) & (kc < c0 + WIN_W)
    neg = jnp.full((GRID_W, GRID_W), -1e30, F32)
    toep = []
    for dr in range(2 * WIN_H - 1):
        rows = jnp.broadcast_to(r_ref[0, 0, dr:dr + 1, :], (GRID_W, LANES))
        t = pltpu.roll(rows, 0, 1, stride=1, stride_axis=0)[:, :GRID_W]
        toep.append(jnp.where(col_ok, t, neg))
    for cls, (dr_tab, row_ok) in enumerate(_na_class_geometry()):
        for a in range(NA_RB):
            for jj in range(NA_KR):
                o_ref[0, cls, 0, GRID_W * a:GRID_W * (a + 1), GRID_W * jj:GRID_W * (jj + 1)] = (
                    toep[int(dr_tab[a, jj])] if row_ok[a, jj] else neg)


def _bias_tables(rpb):
    n_dr, n_dc = 2 * WIN_H - 1, 2 * WIN_W - 1
    assert rpb.shape == (DEPTH, H_A, n_dr, n_dc)
    rows = jnp.concatenate([rpb[..., WIN_W - 1:], jnp.zeros((DEPTH, H_A, n_dr, LANES - n_dc), F32),
                            rpb[..., :WIN_W - 1]], axis=-1)
    rows = jnp.pad(rows, ((0, 0), (0, 0), (0, 16 - n_dr), (0, 0)))
    return pl.pallas_call(
        _bias_kernel,
        grid=(DEPTH, H_A),
        in_specs=[pl.BlockSpec((1, 1, 16, LANES), lambda l, h: (l, h, 0, 0))],
        out_specs=pl.BlockSpec((1, 3, 1, NA_Q, NA_K), lambda l, h: (l, 0, h, 0, 0)),
        out_shape=jax.ShapeDtypeStruct((DEPTH, 3, H_A, NA_Q, NA_K), F32),
        compiler_params=_params(("arbitrary", "arbitrary")),
        name="na_bias",
    )(rows)


def _mlstm_kernel(*refs, n_chunks, has_init, want_state, n_alias, layer):
    dirs = (refs[0:4], refs[4:8])
    pos = 8
    if has_init:
        c0_ref, n0_ref, m0_ref = refs[pos:pos + 3]
        pos += 3
    pos += n_alias
    n_h = 1 if n_chunks == 1 else 2
    h_refs = refs[pos:pos + n_h]
    pos += n_h
    if want_state:
        cf_ref, nf_ref, mf_ref = refs[pos:pos + 3]
        pos += 3
    cs_ref, ms_ref = refs[pos:pos + 2]
    c = pl.program_id(1)

    def tail_rows(x):
        r = lax.broadcasted_iota(jnp.int32, (STATE_TAIL, x.shape[1]), 0)
        return jnp.where(r == 0, x, jnp.where(r == 1, x - x.astype(BF16).astype(F32), 0.0))

    @pl.when(c == 0)
    def _():
        if has_init:
            cs_ref[:, 0:HD_B, :] = c0_ref[0]
            for i in range(N_STATE):
                cs_ref[i, HD_B:HD_B + STATE_TAIL, :] = tail_rows(n0_ref[0, i:i + 1, :])
            ms_ref[...] = m0_ref[0]
        else:
            cs_ref[...] = jnp.zeros(cs_ref.shape, F32)
            ms_ref[...] = jnp.zeros(ms_ref.shape, F32)

    row = lax.broadcasted_iota(jnp.int32, (CHUNK, CHUNK), 0)
    col = lax.broadcasted_iota(jnp.int32, (CHUNK, CHUNK), 1)
    chains = []
    for d in range(2):
        q_ref, k_ref, v_ref, g_ref = dirs[d]
        vis = (row <= col) if d == 0 else (row >= col)
        cum_b = jnp.where((col <= row) if d == 0 else (col >= row), 1.0, 0.0).astype(BF16)
        g_t = g_ref[...].T[0:N_GATE, :]
        r = _dot_nt(jnp.concatenate(_split3(g_t), axis=0), cum_b)
        b_t = r[0:N_GATE] + r[N_GATE:2 * N_GATE] + r[2 * N_GATE:3 * N_GATE]
        i0 = 2 * H_B * d
        u_t = g_t[i0:i0 + H_B, :] - b_t[i0 + H_B:i0 + 2 * H_B, :]
        u_cols = jnp.concatenate([u_t, jnp.zeros((LANES - H_B, CHUNK), F32)], axis=0).T
        q = q_ref[...]
        k = k_ref[...]
        v32 = v_ref[...].astype(F32)
        for hh in range(H_B):
            sl = slice(HD_B * hh, HD_B * (hh + 1))
            chains.append(dict(
                d=d, i=d * H_B + hh, sl=sl, vis=vis, qh=q[:, sl], kh=k[:, sl], v32=v32,
                u_col=u_cols[:, hh:hh + 1],
                b_row=b_t[i0 + H_B + hh:i0 + H_B + hh + 1, :],
                i_row=g_t[i0 + hh:i0 + hh + 1, :]))

    for ch in chains:
        ch["state"] = cs_ref[ch["i"]]
        ch["r"] = _dot_nt(jnp.concatenate([ch["kh"], ch["state"].astype(BF16)], axis=0), ch["qh"])
    for ch in chains:
        b_row, m_prev = ch["b_row"], ms_ref[ch["i"]:ch["i"] + 1, 0:1]
        dm = jnp.where(ch["vis"], ch["u_col"] + b_row, -jnp.inf)
        inter = b_row + m_prev
        m_t = jnp.maximum(inter, jnp.max(dm, axis=0, keepdims=True))
        ch["a"] = jnp.exp(inter - m_t)
        ch["w"] = jnp.exp(dm - m_t)
        ch["floor"] = jnp.exp(-m_t)
        b_end = jnp.broadcast_to(b_row[:, CHUNK - 1:CHUNK] if ch["d"] == 0 else b_row[:, 0:1], (1, CHUNK))
        gl = b_end - b_row + ch["i_row"]
        m_new = jnp.maximum(b_end + m_prev, jnp.max(gl, axis=-1, keepdims=True))
        ch["a_s"] = jnp.exp(b_end + m_prev - m_new)[:, 0:HD_B]
        ch["w_s"] = jnp.exp(gl - m_new)
        ch["m_new"] = m_new
    for ch in chains:
        r, a = ch["r"], ch["a"]
        p = r[0:CHUNK] * ch["w"]
        qn = r[CHUNK + HD_B:CHUNK + HD_B + 1] + r[CHUNK + HD_B + 1:CHUNK + HD_B + 2]
        den = a * qn + jnp.sum(p, axis=0, keepdims=True)
        ch["v_t"] = ch["v32"][:, ch["sl"]].T
        num = a * r[CHUNK:CHUNK + HD_B] + _dot(ch["v_t"].astype(BF16), p.astype(BF16))
        h_t = num * (1.0 / jnp.maximum(jnp.abs(den), ch["floor"]))
        if n_h == 2:
            h_refs[ch["d"]][:, ch["sl"]] = h_t.T
        else:
            ch["h_t"] = h_t
    if n_h == 1:
        for fwd_ch, bwd_ch in zip(chains[:H_B], chains[H_B:]):
            h_refs[0][:, fwd_ch["sl"]] = (fwd_ch["h_t"] + bwd_ch["h_t"]).T
    for ch in chains:
        i, a_s, state = ch["i"], ch["a_s"], ch["state"]
        lhs = jnp.concatenate([ch["v_t"] * ch["w_s"], tail_rows(ch["w_s"])], axis=0).astype(BF16)
        upd = _dot(lhs, ch["kh"])
        n_new = a_s * state[HD_B:HD_B + 1] + upd[HD_B:HD_B + 1] + upd[HD_B + 1:HD_B + 2]
        cs_ref[i, 0:HD_B, :] = a_s * state[0:HD_B] + upd[0:HD_B]
        cs_ref[i, HD_B:HD_B + STATE_TAIL, :] = tail_rows(n_new)
        ms_ref[i:i + 1, :] = ch["m_new"][:, 0:LANES]

    if want_state:
        @pl.when(c == n_chunks - 1)
        def _():
            if n_alias:
                cf_ref[0] = cs_ref[:, 0:HD_B, :]
            else:
                cf_ref[...] = jnp.zeros(cf_ref.shape, F32)
                cf_ref[0, layer] = cs_ref[:, 0:HD_B, :]
            for i in range(N_STATE):
                nf_ref[0, i:i + 1, :] = cs_ref[i, HD_B:HD_B + 1, :]
            mf_ref[0] = ms_ref[...]


def _mlstm(qb, kb, vb, gt, row0, n_batch, seq_len, layer, init=None, want_state=False, c_prev=None):
    n_chunks = seq_len // CHUNK
    blk0 = row0 // CHUNK

    def fwd(w):
        return pl.BlockSpec((CHUNK, w), lambda b, c: (blk0 + b * n_chunks + c, 0))

    def bwd(w):
        return pl.BlockSpec((CHUNK, w), lambda b, c: (blk0 + b * n_chunks + n_chunks - 1 - c, 0))

    in_specs = [fwd(B_WIDTH), fwd(B_WIDTH), fwd(B_WIDTH), fwd(LANES),
                bwd(B_WIDTH), bwd(B_WIDTH), bwd(B_WIDTH), bwd(LANES)]
    args = [qb, kb, vb, gt, qb, kb, vb, gt]
    c_spec = pl.BlockSpec((1, None, N_STATE, HD_B, HD_B), lambda b, c: (b, layer, 0, 0, 0))
    if init is not None:
        in_specs += [c_spec,
                     pl.BlockSpec((1, None, N_STATE, LANES), lambda b, c: (b, layer, 0, 0)),
                     pl.BlockSpec((1, None, N_STATE, LANES), lambda b, c: (b, layer, 0, 0))]
        args += list(init)
    n_rows = n_batch * seq_len
    out_specs = [pl.BlockSpec((CHUNK, B_WIDTH), lambda b, c: (b * n_chunks + c, 0)),
                 pl.BlockSpec((CHUNK, B_WIDTH), lambda b, c: (b * n_chunks + n_chunks - 1 - c, 0))]
    if n_chunks == 1:
        out_specs = out_specs[:1]
    out_shape = [jax.ShapeDtypeStruct((n_rows, B_WIDTH), F32)] * len(out_specs)
    aliases = {}
    if c_prev is not None:
        aliases = {len(args): len(out_specs)}
        in_specs += [pl.BlockSpec(memory_space=pl.ANY)]
        args += [c_prev]
    if want_state:
        c_out = c_spec if c_prev is not None else pl.BlockSpec(
            (1, DEPTH, N_STATE, HD_B, HD_B), lambda b, c: (b, 0, 0, 0, 0))
        out_specs += [c_out,
                      pl.BlockSpec((1, N_STATE, LANES), lambda b, c: (b, 0, 0)),
                      pl.BlockSpec((1, N_STATE, LANES), lambda b, c: (b, 0, 0))]
        out_shape += [jax.ShapeDtypeStruct((n_batch, DEPTH, N_STATE, HD_B, HD_B), F32),
                      jax.ShapeDtypeStruct((n_batch, N_STATE, LANES), F32),
                      jax.ShapeDtypeStruct((n_batch, N_STATE, LANES), F32)]
    return pl.pallas_call(
        functools.partial(_mlstm_kernel, n_chunks=n_chunks, has_init=init is not None, want_state=want_state,
                          n_alias=len(aliases), layer=layer),
        grid=(n_batch, n_chunks),
        input_output_aliases=aliases,
        in_specs=in_specs,
        out_specs=out_specs,
        out_shape=out_shape,
        scratch_shapes=[pltpu.VMEM((N_STATE, HD_B + STATE_TAIL, HD_B), F32),
                        pltpu.VMEM((N_STATE, LANES), F32)],
        compiler_params=_params(("parallel", "arbitrary")),
        name="mlstm",
    )(*args)


def _mixout_ffn_kernel(*refs, split_out, n_hc, n_hl):
    x_ref, mod_ref, attc_ref, attl_ref = refs[:4]
    hc_refs = refs[4:4 + n_hc]
    hl_refs = refs[4 + n_hc:4 + n_hc + n_hl]
    n_in = 13 + n_hc + n_hl
    (ob_ref, br_ref, g_ref, ghn_ref, wpa_ref, wpb_ref, wo_ref, wup_ref,
     wdn_ref) = refs[4 + n_hc + n_hl:n_in]
    o_refs = refs[n_in:]
    is_ctx = pl.program_id(0) < N_CTX // TM
    att = jnp.where(is_ctx, attc_ref[...], attl_ref[...])
    hsum = jnp.where(is_ctx, sum(r[...] for r in hc_refs), sum(r[...] for r in hl_refs))
    parts = []
    for hh in range(H_B):
        seg = hsum[:, HD_B * hh:HD_B * (hh + 1)]
        parts.append(seg * lax.rsqrt(jnp.mean(seg * seg, axis=-1, keepdims=True) + EPS))
    hn = (jnp.concatenate(parts, axis=1) * ghn_ref[...]) * ob_ref[...].astype(F32)
    y = (br_ref[:, :D_MODEL].astype(F32) * _dot(att, wpa_ref[...])
         + br_ref[:, D_MODEL:].astype(F32) * _dot(hn.astype(BF16), wpb_ref[...]))
    x = x_ref[...] + _mod_row(mod_ref, 5) * _dot(y.astype(BF16), wo_ref[...])
    out = _swiglu_sub(x, g_ref[2:3, :], _mod_row(mod_ref, 6), _mod_row(mod_ref, 7), _mod_row(mod_ref, 8),
                      wup_ref, wdn_ref)
    _write_stream(o_refs, out, is_ctx, split_out)


def _mixout_ffn(x, mod, att_c, att_l, h_c, h_l, sob, sbr, g_norm, g_hn, w_pa, w_pb, w_o,
                w_up, w_dn, l, split_out=False):
    tok, ctx, lat = _tok_spec, _ctx_spec, _lat_spec
    if split_out:
        out_specs = [ctx(D_MODEL), lat(D_MODEL)]
        out_shape = [jax.ShapeDtypeStruct((N_CTX, D_MODEL), F32), jax.ShapeDtypeStruct((N_LAT, D_MODEL), F32)]
    else:
        out_specs = tok(D_MODEL)
        out_shape = jax.ShapeDtypeStruct((N_TOK, D_MODEL), F32)
    return pl.pallas_call(
        functools.partial(_mixout_ffn_kernel, split_out=split_out, n_hc=len(h_c), n_hl=len(h_l)),
        grid=(N_TOK // TM,),
        in_specs=[
            tok(D_MODEL),
            _layer((MOD_ROWS, N_MOD * D_MODEL), l),
            ctx(A_WIDTH), lat(A_WIDTH),
        ] + [ctx(B_WIDTH)] * len(h_c) + [lat(B_WIDTH)] * len(h_l) + [
            tok(B_WIDTH), tok(2 * D_MODEL),
            _layer((3, D_MODEL), l),
            _layer((1, B_WIDTH), l),
            _layer((A_WIDTH, D_MODEL), l),
            _layer((B_WIDTH, D_MODEL), l),
            _layer((D_MODEL, D_MODEL), l),
            _layer((D_MODEL, 2 * D_FF), l),
            _layer((D_FF, D_MODEL), l),
        ],
        out_specs=out_specs,
        out_shape=out_shape,
        compiler_params=_params(("arbitrary",)),
        name="mixout_ffn",
    )(x, mod, att_c, att_l, *h_c, *h_l, sob, sbr, g_norm, g_hn, w_pa, w_pb, w_o, w_up, w_dn)


def _rope_tables():
    half = HD_B // 2
    nf = half // 2
    inv = ROPE_THETA ** (-jnp.arange(nf, dtype=F32) / nf)
    t = jnp.arange(LAT_LEN)
    rowp = (t // GRID_W).astype(F32)
    colp = (t % GRID_W).astype(F32)
    ang = jnp.concatenate([rowp[:, None] * inv, colp[:, None] * inv], axis=-1)
    cos, sin = jnp.cos(ang), jnp.sin(ang)
    cos_l = jnp.tile(jnp.concatenate([cos, cos], axis=-1), (N_LAT_B, 1))
    sin_l = jnp.tile(jnp.concatenate([-sin, sin], axis=-1), (N_LAT_B, 1))
    cosf = jnp.concatenate([jnp.ones((N_CTX, HD_B), F32), cos_l], axis=0)
    sinf = jnp.concatenate([jnp.zeros((N_CTX, HD_B), F32), sin_l], axis=0)
    return cosf, sinf


def kernel(x_prompt, x_sample, cache_k, cache_v, state_C, state_n, state_m, c, c_ctx, w_ada, b_ada, g_norm,
           w_up1, w_dn1, w_in, b_gates, g_qn, g_kn, rpb, g_hn, w_pa, w_pb, w_o, w_up2, w_dn2):
    assert x_prompt.shape == (N_CTX_B, CTX_LEN, D_MODEL) and x_sample.shape == (N_LAT_B, LAT_LEN, D_MODEL)
    past_len = cache_k.shape[2]

    w_inb = w_in.astype(BF16)
    b_gt = jnp.pad(b_gates, ((0, 0), (0, LANES - N_GATE))).reshape(DEPTH, 1, LANES)
    w_pab, w_pbb, w_ob, w_up2b, w_dn2b = (w.astype(BF16) for w in (w_pa, w_pb, w_o, w_up2, w_dn2))
    gq = jnp.tile(g_qn, (1, H_A)).reshape(DEPTH, 1, A_WIDTH)
    gk = jnp.tile(g_kn, (1, H_A)).reshape(DEPTH, 1, A_WIDTH)
    ghn = g_hn.reshape(DEPTH, 1, B_WIDTH)
    seg_np = (np.arange(A_WIDTH)[:, None] // HD_A == np.arange(LANES)[None, :]).astype(np.float32)
    seg = jnp.asarray(seg_np, BF16)
    segt = jnp.asarray(seg_np.T, BF16)
    cosf, sinf = _rope_tables()
    bias = _bias_tables(rpb)
    ck = cache_k.reshape(N_LAT_B, DEPTH, past_len, A_WIDTH)
    cv = cache_v.reshape(N_LAT_B, DEPTH, past_len, A_WIDTH)
    init = (state_C.reshape(N_LAT_B, DEPTH, N_STATE, HD_B, HD_B),
            state_n.reshape(N_LAT_B, DEPTH, N_STATE, HD_B),
            jnp.broadcast_to(state_m.reshape(N_LAT_B, DEPTH, N_STATE, 1), (N_LAT_B, DEPTH, N_STATE, LANES)))

    cond = jnp.concatenate([c_ctx[None], c, jnp.zeros((MOD_ROWS - 1 - N_LAT_B, D_MODEL), F32)], axis=0)
    mod = _ada(cond, w_ada, b_ada)

    x = (x_prompt.reshape(N_CTX, D_MODEL), x_sample.reshape(N_LAT, D_MODEL))
    caches = tuple(jnp.zeros((N_CTX_B, DEPTH, CTX_LEN, A_WIDTH), F32) for _ in range(2))
    c_new, ns_out, ms_out = None, [], []
    for l in range(DEPTH):
        x = _ffn(x, mod, g_norm, w_up1, w_dn1, l, 0)
        qa, ka, va, k_new, v_new, qb, kb, vb, sob, gt, sbr = _mixin(
            x, mod, g_norm, w_inb, b_gt, gq, gk, seg, segt, cosf, sinf, l, caches)
        caches = (k_new, v_new)
        att_c = _ctx_attn(qa, ka, va)
        att_l = _na_attn(qa, ka, va, ck, cv, bias, l)
        *h_c, c_new, n_fin, m_fin = _mlstm(qb, kb, vb, gt, 0, N_CTX_B, CTX_LEN, l, want_state=True, c_prev=c_new)
        h_l = _mlstm(qb, kb, vb, gt, N_CTX, N_LAT_B, LAT_LEN, l, init=init)
        x = _mixout_ffn(x, mod, att_c, att_l, h_c, h_l, sob, sbr, g_norm, ghn, w_pab, w_pbb, w_ob,
                        w_up2b, w_dn2b, l, split_out=(l == DEPTH - 1))
        ns_out.append(n_fin.reshape(N_CTX_B, 2, H_B, HD_B))
        ms_out.append(m_fin[:, :, 0].reshape(N_CTX_B, 2, H_B))

    return (x[0].reshape(N_CTX_B, CTX_LEN, D_MODEL),
            x[1].reshape(N_LAT_B, LAT_LEN, D_MODEL),
            k_new.reshape(N_CTX_B, DEPTH, CTX_LEN, H_A, HD_A),
            v_new.reshape(N_CTX_B, DEPTH, CTX_LEN, H_A, HD_A),
            c_new.reshape(N_CTX_B, DEPTH, 2, H_B, HD_B, HD_B),
            jnp.stack(ns_out, axis=1), jnp.stack(ms_out, axis=1))
```

```python
import functools

import jax
import jax.numpy as jnp
import numpy as np
from jax import lax
from jax.experimental import pallas as pl
from jax.experimental.pallas import tpu as pltpu

F32 = jnp.float32
BF16 = jnp.bfloat16

D_MODEL = 1024
DEPTH = 4
N_CTX_B, CTX_LEN = 32, 256
N_LAT_B, LAT_LEN = 2, 2048
N_CTX = N_CTX_B * CTX_LEN
N_LAT = N_LAT_B * LAT_LEN
N_TOK = N_CTX + N_LAT
GRID_W, WIN_H, WIN_W = 64, 8, 16
LAT_ROWS = LAT_LEN // GRID_W
H_A, HD_A = 8, 64
H_B, HD_B = 4, 128
A_WIDTH = H_A * HD_A
B_WIDTH = H_B * HD_B
D_FF = 2816
N_MOD = 9
CHUNK = 256
ROPE_THETA = 10000.0
EPS = 1e-6
KSCALE = HD_B ** -0.5
ASCALE = HD_A ** -0.5
N_GATE = 4 * H_B
QKV_COLS = 3 * A_WIDTH + 4 * B_WIDTH
N_STATE = 2 * H_B
STATE_TAIL = 16
MOD_ROWS = 8

LANES = 128
TM = 512
TM_FFN = 512
FF_CHUNK = 256
VMEM_LIMIT = 52 * 1024 * 1024

NA_RB = WIN_H // 2
NA_KR = NA_RB + WIN_H - 1
NA_Q = NA_RB * GRID_W
NA_K = NA_KR * GRID_W
NA_BLOCKS = LAT_ROWS // NA_RB
assert LAT_ROWS % NA_RB == 0 and NA_BLOCKS >= 3 and N_CTX % N_LAT == 0 and 1 + N_LAT_B <= MOD_ROWS


def _dot(a, b):
    return jnp.dot(a, b, preferred_element_type=F32)


def _dot_nt(a, b):
    return lax.dot_general(a, b, (((1,), (1,)), ((), ())), preferred_element_type=F32)


def _split2(x):
    hi = x.astype(BF16)
    lo = (x - hi.astype(F32)).astype(BF16)
    return hi, lo


def _split3(x):
    hi = x.astype(BF16)
    r1 = x - hi.astype(F32)
    mid = r1.astype(BF16)
    lo = (r1 - mid.astype(F32)).astype(BF16)
    return hi, mid, lo


def _norm_mod(x, g, shift, scale):
    ms = jnp.mean(x * x, axis=-1, keepdims=True)
    return (x * lax.rsqrt(ms + EPS) * g) * (1.0 + scale) + shift


def _params(sem):
    return pltpu.CompilerParams(dimension_semantics=sem, vmem_limit_bytes=VMEM_LIMIT)


def _resident(shape):
    nd = len(shape)
    return pl.BlockSpec(shape, lambda *_: (0,) * nd, pipeline_mode=pl.Buffered(1))


def _layer(shape, l):
    nd = len(shape)
    return pl.BlockSpec((None,) + tuple(shape), lambda *_: (l,) + (0,) * nd, pipeline_mode=pl.Buffered(1))


def _tok_spec(w, tm=TM):
    return pl.BlockSpec((tm, w), lambda i: (i, 0))


def _ctx_spec(w, tm=TM):
    return pl.BlockSpec((tm, w), lambda i: (jnp.minimum(i, N_CTX // tm - 1), 0))


def _lat_spec(w, tm=TM):
    return pl.BlockSpec((tm, w), lambda i: (jnp.maximum(i - N_CTX // tm, 0), 0))


def _mod_row(mod_ref, k, tm=TM):
    i = pl.program_id(0)
    n_ctx_tiles = N_CTX // tm
    mod_set = jnp.where(i < n_ctx_tiles, 0, 1 + (i - n_ctx_tiles) // (LAT_LEN // tm))
    return mod_ref[pl.ds(mod_set, 1), D_MODEL * k:D_MODEL * (k + 1)]


def _ada_kernel(c_ref, w_ref, b_ref, o_ref):
    c = c_ref[...]
    s = c * jax.nn.sigmoid(c)
    sh, sl = _split2(s)
    wh, wl = _split2(w_ref[0])
    o_ref[0] = _dot(sh, wh) + _dot(sl, wh) + _dot(sh, wl) + b_ref[0]


def _ada(cond, w_ada, b_ada):
    tn = 2304
    n_out = N_MOD * D_MODEL
    return pl.pallas_call(
        _ada_kernel,
        grid=(DEPTH, n_out // tn),
        in_specs=[
            pl.BlockSpec((MOD_ROWS, D_MODEL), lambda l, j: (0, 0)),
            pl.BlockSpec((1, D_MODEL, tn), lambda l, j: (l, 0, j)),
            pl.BlockSpec((1, 1, tn), lambda l, j: (l, 0, j)),
        ],
        out_specs=pl.BlockSpec((1, MOD_ROWS, tn), lambda l, j: (l, 0, j)),
        out_shape=jax.ShapeDtypeStruct((DEPTH, MOD_ROWS, n_out), F32),
        compiler_params=_params(("parallel", "parallel")),
        name="ada",
    )(cond, w_ada, b_ada.reshape(DEPTH, 1, n_out))


N_FF_CHUNKS = D_FF // FF_CHUNK


FEED_SLOTS = 2


def _ffn_weight_scratch():
    return [pltpu.VMEM((D_MODEL, 2 * D_FF), BF16),
            pltpu.VMEM((D_FF, D_MODEL), BF16),
            pltpu.VMEM((FEED_SLOTS, D_MODEL, FF_CHUNK), F32),
            pltpu.VMEM((FEED_SLOTS, D_MODEL, FF_CHUNK), F32),
            pltpu.VMEM((FEED_SLOTS, FF_CHUNK, D_MODEL), F32),
            pltpu.SemaphoreType.DMA((3, FEED_SLOTS))]


def _ffn_weight_feed(wup_hbm, wdn_hbm, wup_ref, wdn_ref, st_a, st_u, st_d, sems):
    def copies(j):
        slot, c0 = j % FEED_SLOTS, j * FF_CHUNK
        return (pltpu.make_async_copy(wup_hbm.at[:, pl.ds(c0, FF_CHUNK)], st_a.at[slot], sems.at[0, slot]),
                pltpu.make_async_copy(wup_hbm.at[:, pl.ds(D_FF + c0, FF_CHUNK)], st_u.at[slot], sems.at[1, slot]),
                pltpu.make_async_copy(wdn_hbm.at[pl.ds(c0, FF_CHUNK), :], st_d.at[slot], sems.at[2, slot]))

    def before_chunk(j):
        if j == 0:
            for jj in range(min(FEED_SLOTS, N_FF_CHUNKS)):
                for cp in copies(jj):
                    cp.start()
        for cp in copies(j):
            cp.wait()
        slot, c0 = j % FEED_SLOTS, j * FF_CHUNK
        wup_ref[:, c0:c0 + FF_CHUNK] = st_a[slot].astype(BF16)
        wup_ref[:, D_FF + c0:D_FF + c0 + FF_CHUNK] = st_u[slot].astype(BF16)
        wdn_ref[c0:c0 + FF_CHUNK, :] = st_d[slot].astype(BF16)
        if j + FEED_SLOTS < N_FF_CHUNKS:
            for cp in copies(j + FEED_SLOTS):
                cp.start()

    return before_chunk


def _swiglu_sub(x, g, shift, scale, gate, wup_ref, wdn_ref, before_chunk=None):
    h = _norm_mod(x, g, shift, scale).astype(BF16)
    acc = jnp.zeros(x.shape, F32)
    for j in range(N_FF_CHUNKS):
        c0 = j * FF_CHUNK
        if before_chunk is not None:
            before_chunk(j)
        a = _dot(h, wup_ref[:, c0:c0 + FF_CHUNK])
        u = _dot(h, wup_ref[:, D_FF + c0:D_FF + c0 + FF_CHUNK])
        g_act = (a * jax.nn.sigmoid(a) * u).astype(BF16)
        acc = acc + _dot(g_act, wdn_ref[c0:c0 + FF_CHUNK, :])
    return x + 0.5 * gate * acc


def _write_stream(o_refs, y, is_ctx, split_out):
    if split_out:
        @pl.when(is_ctx)
        def _():
            o_refs[0][...] = y

        @pl.when(jnp.logical_not(is_ctx))
        def _():
            o_refs[1][...] = y
    else:
        o_refs[0][...] = y


def _ffn_kernel(*refs, layer, sub, split_in, split_out):
    n_x = 2 if split_in else 1
    mod_ref, g_ref, wup_hbm, wdn_hbm = refs[n_x:n_x + 4]
    n_out = 2 if split_out else 1
    o_refs = refs[n_x + 4:n_x + 4 + n_out]
    wup_ref, wdn_ref, *stage = refs[n_x + 4 + n_out:]
    is_ctx = pl.program_id(0) < N_CTX // TM_FFN
    mod_row = functools.partial(_mod_row, mod_ref, tm=TM_FFN)

    def run(before_chunk):
        x = jnp.where(is_ctx, refs[0][...], refs[1][...]) if split_in else refs[0][...]
        y = _swiglu_sub(x, g_ref[sub:sub + 1, :], mod_row(3 * sub), mod_row(3 * sub + 1), mod_row(3 * sub + 2),
                        wup_ref, wdn_ref, before_chunk)
        _write_stream(o_refs, y, is_ctx, split_out)

    @pl.when(pl.program_id(0) == 0)
    def _():
        run(_ffn_weight_feed(wup_hbm.at[layer], wdn_hbm.at[layer], wup_ref, wdn_ref, *stage))

    @pl.when(pl.program_id(0) != 0)
    def _():
        run(None)


def _ffn(xs, mod, g_norm, w_up, w_dn, l, sub, split_out=False):
    split_in = isinstance(xs, tuple)
    xs = xs if split_in else (xs,)
    tm = TM_FFN
    x_specs = [_ctx_spec(D_MODEL, tm), _lat_spec(D_MODEL, tm)] if split_in else [_tok_spec(D_MODEL, tm)]
    if split_out:
        out_specs = [_ctx_spec(D_MODEL, tm), _lat_spec(D_MODEL, tm)]
        out_shape = [jax.ShapeDtypeStruct((N_CTX, D_MODEL), F32), jax.ShapeDtypeStruct((N_LAT, D_MODEL), F32)]
    else:
        out_specs = _tok_spec(D_MODEL, tm)
        out_shape = jax.ShapeDtypeStruct((N_TOK, D_MODEL), F32)
    return pl.pallas_call(
        functools.partial(_ffn_kernel, layer=l, sub=sub, split_in=split_in, split_out=split_out),
        grid=(N_TOK // tm,),
        in_specs=x_specs + [
            _layer((MOD_ROWS, N_MOD * D_MODEL), l),
            _layer((3, D_MODEL), l),
            pl.BlockSpec(memory_space=pl.ANY),
            pl.BlockSpec(memory_space=pl.ANY),
        ],
        out_specs=out_specs,
        out_shape=out_shape,
        scratch_shapes=_ffn_weight_scratch(),
        compiler_params=_params(("arbitrary",)),
        name="ffn",
    )(*xs, mod, g_norm, w_up, w_dn)


N_MIXIN_IN = 12
N_MIXIN_OUT = 11


def _mixin_kernel(*refs, n_alias):
    (x_ref, mod_ref, g_ref, wa_ref, wtail_ref, bg_ref, gq_ref, gk_ref, seg_ref, segt_ref,
     cos_ref, sin_ref) = refs[:N_MIXIN_IN]
    (qa_ref, ka_ref, va_ref, kc_ref, vc_ref, qb_ref, kb_ref, vb_ref, ob_ref, gt_ref,
     br_ref) = refs[N_MIXIN_IN + n_alias:N_MIXIN_IN + n_alias + N_MIXIN_OUT]
    wbr_ref = refs[N_MIXIN_IN + n_alias + N_MIXIN_OUT]
    is_ctx = pl.program_id(0) < N_CTX // TM

    @pl.when(pl.program_id(0) == 0)
    def _():
        wbr_ref[...] = wtail_ref[:, N_GATE:N_GATE + 2 * D_MODEL]

    x = x_ref[...]
    h = _norm_mod(x, g_ref[1:2, :], _mod_row(mod_ref, 3), _mod_row(mod_ref, 4)).astype(BF16)

    def proj(c0, c1):
        return _dot(h, wa_ref[:, c0:c1])

    def head_sums(p):
        hi, lo = _split2(p * p)
        return _dot(hi, seg_ref[...]) + _dot(lo, seg_ref[...])

    def head_scale(ssum):
        rh, rl = _split2(lax.rsqrt(ssum * (1.0 / HD_A) + EPS))
        return _dot(rh, segt_ref[...]) + _dot(rl, segt_ref[...])

    cosf = cos_ref[...]
    sinf = sin_ref[...]

    def rope(p):
        parts = []
        for hh in range(H_B):
            ph = p[:, HD_B * hh:HD_B * (hh + 1)]
            parts.append(ph * cosf + pltpu.roll(ph, HD_B // 2, 1) * sinf)
        return jnp.concatenate(parts, axis=1)

    cols = [0]
    for w in (A_WIDTH,) * 3 + (B_WIDTH,) * 4:
        cols.append(cols[-1] + w)
    p_qa = proj(cols[0], cols[1])
    p_ka = proj(cols[1], cols[2])
    s_qa = head_sums(p_qa)
    s_ka = head_sums(p_ka)
    va = proj(cols[2], cols[3])
    p_qb = proj(cols[3], cols[4])
    r_qa = head_scale(s_qa)
    r_ka = head_scale(s_ka)
    p_kb = proj(cols[4], cols[5])
    p_vb = proj(cols[5], cols[6])
    p_ob = proj(cols[6], cols[7])

    qa_ref[...] = (p_qa * r_qa * gq_ref[...] * ASCALE).astype(BF16)
    ka = p_ka * r_ka * gk_ref[...]
    ka_ref[...] = ka.astype(BF16)
    va_ref[...] = va.astype(BF16)

    @pl.when(is_ctx)
    def _():
        kc_ref[...] = ka.reshape(kc_ref.shape)
        vc_ref[...] = va.reshape(vc_ref.shape)

    qb_ref[...] = rope(p_qb).astype(BF16)
    kb_ref[...] = (rope(p_kb) * KSCALE).astype(BF16)
    vb_ref[...] = p_vb.astype(BF16)
    ob_ref[...] = jax.nn.sigmoid(p_ob).astype(BF16)
    for j in range(2 * D_MODEL // 512):
        sl = slice(512 * j, 512 * (j + 1))
        br_ref[:, sl] = jax.nn.sigmoid(_dot(h, wbr_ref[:, sl])).astype(BF16)

    pg = _dot(h, wtail_ref[:, 0:LANES]) + bg_ref[...]
    ls = jnp.minimum(pg, 0.0) - jnp.log1p(jnp.exp(-jnp.abs(pg)))
    lane = lax.broadcasted_iota(jnp.int32, pg.shape, 1)
    is_f = ((lane >= H_B) & (lane < 2 * H_B)) | ((lane >= 3 * H_B) & (lane < 4 * H_B))
    gt_ref[...] = jnp.where(is_f, ls, pg)


def _mixin(x, mod, g_norm, w_in, b_gt, gq, gk, seg, segt, cosf, sinf, l, caches=None):
    assert N_GATE + 2 * D_MODEL <= QKV_COLS and w_in.shape[-1] == QKV_COLS + N_GATE + 2 * D_MODEL
    w_block = lambda j: pl.BlockSpec((None, D_MODEL, QKV_COLS), lambda i: (l, 0, j), pipeline_mode=pl.Buffered(1))
    tok = _tok_spec
    out = lambda w, dt: jax.ShapeDtypeStruct((N_TOK, w), dt)
    seqs = TM // CTX_LEN
    ctx_out = jax.ShapeDtypeStruct((N_CTX_B, DEPTH, CTX_LEN, A_WIDTH), F32)
    cache_spec = pl.BlockSpec((seqs, None, CTX_LEN, A_WIDTH),
                              lambda i: (jnp.minimum(i, N_CTX // TM - 1), l, 0, 0))
    caches = () if caches is None else tuple(caches)
    return pl.pallas_call(
        functools.partial(_mixin_kernel, n_alias=len(caches)),
        grid=(N_TOK // TM,),
        input_output_aliases={N_MIXIN_IN + j: 3 + j for j in range(len(caches))},
        in_specs=[
            tok(D_MODEL),
            _layer((MOD_ROWS, N_MOD * D_MODEL), l),
            _layer((3, D_MODEL), l),
            w_block(0),
            w_block(1),
            _layer((1, LANES), l),
            _layer((1, A_WIDTH), l),
            _layer((1, A_WIDTH), l),
            _resident((A_WIDTH, LANES)),
            _resident((LANES, A_WIDTH)),
            tok(HD_B),
            tok(HD_B),
        ] + [pl.BlockSpec(memory_space=pl.ANY)] * len(caches),
        out_specs=[tok(A_WIDTH), tok(A_WIDTH), tok(A_WIDTH), cache_spec, cache_spec,
                   tok(B_WIDTH), tok(B_WIDTH), tok(B_WIDTH), tok(B_WIDTH), tok(LANES), tok(2 * D_MODEL)],
        out_shape=[out(A_WIDTH, BF16), out(A_WIDTH, BF16), out(A_WIDTH, BF16), ctx_out, ctx_out,
                   out(B_WIDTH, BF16), out(B_WIDTH, BF16), out(B_WIDTH, BF16),
                   out(B_WIDTH, BF16), out(LANES, F32), out(2 * D_MODEL, BF16)],
        scratch_shapes=[pltpu.VMEM((D_MODEL, 2 * D_MODEL), BF16)],
        compiler_params=_params(("arbitrary",)),
        name="mixin",
    )(x, mod, g_norm, w_in, w_in, b_gt, gq, gk, seg, segt, cosf, sinf, *caches)


CTX_SEQS = 4


def _head_pair_masks():
    lane = lax.broadcasted_iota(jnp.int32, (1, LANES), 1)
    first = lane < HD_A
    return first, [jnp.where(first, 1.0, 0.0).astype(BF16), jnp.where(first, 0.0, 1.0).astype(BF16)]


def _ctx_attn_kernel(q_ref, k_ref, v_ref, o_ref):
    first, keep = _head_pair_masks()
    q = q_ref[...]
    k = k_ref[...]
    v = v_ref[...]
    items = [(slice(CTX_LEN * sq, CTX_LEN * (sq + 1)), slice(LANES * t, LANES * (t + 1)), half)
             for sq in range(CTX_SEQS) for t in range(A_WIDTH // LANES) for half in range(2)]
    scores = [_dot_nt(q[rows, tile] * keep[half], k[rows, tile]) for rows, tile, half in items]
    probs = [jnp.exp(s - jnp.max(s, axis=-1, keepdims=True)) for s in scores]
    outs = [_dot(p.astype(BF16), v[rows, tile]) / jnp.sum(p, axis=-1, keepdims=True)
            for p, (rows, tile, half) in zip(probs, items)]
    for (rows, tile, _), o_first, o_second in zip(items[0::2], outs[0::2], outs[1::2]):
        o_ref[rows, tile] = jnp.where(first, o_first, o_second).astype(BF16)


def _ctx_attn(qa, ka, va):
    blk = pl.BlockSpec((CTX_SEQS * CTX_LEN, A_WIDTH), lambda b: (b, 0))
    return pl.pallas_call(
        _ctx_attn_kernel,
        grid=(N_CTX_B // CTX_SEQS,),
        in_specs=[blk, blk, blk],
        out_specs=blk,
        out_shape=jax.ShapeDtypeStruct((N_CTX, A_WIDTH), BF16),
        compiler_params=_params(("parallel",)),
        name="ctx_attn",
    )(qa, ka, va)


def _na_key_row0(j):
    return jnp.clip(NA_RB * j - WIN_H // 2, 0, LAT_ROWS - NA_KR)


def _na_kernel(q_ref, k_ref, v_ref, ck_ref, cv_ref, bias_ref, o_ref):
    j = pl.program_id(0)
    b = pl.program_id(1)
    base = pl.multiple_of(b * LAT_LEN + _na_key_row0(j) * GRID_W, GRID_W)
    kw = k_ref[pl.ds(base, NA_K), :]
    vw = v_ref[pl.ds(base, NA_K), :]
    ck = ck_ref[b].astype(BF16)
    cv = cv_ref[b].astype(BF16)
    q = q_ref[...]
    heads = [slice(HD_A * hh, HD_A * (hh + 1)) for hh in range(H_A)]
    s_win = [_dot_nt(q[:, sl], kw[:, sl]) for sl in heads]
    s_ctx = [_dot_nt(q[:, sl], ck[:, sl]) for sl in heads]
    probs = []
    for hh in range(H_A):
        sw = s_win[hh] + bias_ref[hh]
        sx = s_ctx[hh]
        m = jnp.maximum(jnp.max(sw, axis=-1, keepdims=True), jnp.max(sx, axis=-1, keepdims=True))
        pw = jnp.exp(sw - m)
        px = jnp.exp(sx - m)
        l = jnp.sum(pw, axis=-1, keepdims=True) + jnp.sum(px, axis=-1, keepdims=True)
        probs.append((pw.astype(BF16), px.astype(BF16), l))
    outs = [(_dot(pw, vw[:, sl]) + _dot(px, cv[:, sl])) / l for (pw, px, l), sl in zip(probs, heads)]
    o_ref[...] = jnp.concatenate(outs, axis=1).astype(BF16)


def _na_attn(qa, ka, va, ck, cv, bias, l):
    past = ck.shape[2]
    lat_kv = pl.BlockSpec((N_LAT, A_WIDTH), lambda j, b: (N_CTX // N_LAT, 0), pipeline_mode=pl.Buffered(1))
    cache = pl.BlockSpec((N_LAT_B, None, past, A_WIDTH), lambda j, b: (0, l, 0, 0), pipeline_mode=pl.Buffered(1))
    bias_class = lambda j: jnp.where(j == 0, 0, jnp.where(j == NA_BLOCKS - 1, 2, 1))
    return pl.pallas_call(
        _na_kernel,
        grid=(NA_BLOCKS, N_LAT_B),
        in_specs=[
            pl.BlockSpec((NA_Q, A_WIDTH), lambda j, b: (N_CTX // NA_Q + b * NA_BLOCKS + j, 0)),
            lat_kv, lat_kv, cache, cache,
            pl.BlockSpec((None, None, H_A, NA_Q, NA_K), lambda j, b: (l, bias_class(j), 0, 0, 0)),
        ],
        out_specs=pl.BlockSpec((NA_Q, A_WIDTH), lambda j, b: (b * NA_BLOCKS + j, 0)),
        out_shape=jax.ShapeDtypeStruct((N_LAT, A_WIDTH), BF16),
        compiler_params=_params(("arbitrary", "arbitrary")),
        name="na_attn",
    )(qa, ka, va, ck, cv, bias)


def _na_class_geometry():
    out = []
    for j in (0, 1, NA_BLOCKS - 1):
        r = NA_RB * j + np.arange(NA_RB)
        kr = int(np.clip(NA_RB * j - WIN_H // 2, 0, LAT_ROWS - NA_KR)) + np.arange(NA_KR)
        start_r = np.clip(r - WIN_H // 2, 0, LAT_ROWS - WIN_H)
        row_ok = (kr[None, :] >= start_r[:, None]) & (kr[None, :] < start_r[:, None] + WIN_H)
        dr = kr[None, :] - r[:, None] + (WIN_H - 1)
        out.append((dr, row_ok))
    return out


def _bias_kernel(r_ref, o_ref):
    qc = lax.broadcasted_iota(jnp.int32, (GRID_W, GRID_W), 0)
    kc = lax.broadcasted_iota(jnp.int32, (GRID_W, GRID_W), 1)
    c0 = jnp.clip(qc - WIN_W // 2, 0, GRID_W - WIN_W)
    col_ok = (kc >= c0) & (kc < c0 + WIN_W)
    neg = jnp.full((GRID_W, GRID_W), -1e30, F32)
    toep = []
    for dr in range(2 * WIN_H - 1):
        rows = jnp.broadcast_to(r_ref[0, 0, dr:dr + 1, :], (GRID_W, LANES))
        t = pltpu.roll(rows, 0, 1, stride=1, stride_axis=0)[:, :GRID_W]
        toep.append(jnp.where(col_ok, t, neg))
    for cls, (dr_tab, row_ok) in enumerate(_na_class_geometry()):
        for a in range(NA_RB):
            for jj in range(NA_KR):
                o_ref[0, cls, 0, GRID_W * a:GRID_W * (a + 1), GRID_W * jj:GRID_W * (jj + 1)] = (
                    toep[int(dr_tab[a, jj])] if row_ok[a, jj] else neg)


def _bias_tables(rpb):
    n_dr, n_dc = 2 * WIN_H - 1, 2 * WIN_W - 1
    assert rpb.shape == (DEPTH, H_A, n_dr, n_dc)
    rows = jnp.concatenate([rpb[..., WIN_W - 1:], jnp.zeros((DEPTH, H_A, n_dr, LANES - n_dc), F32),
                            rpb[..., :WIN_W - 1]], axis=-1)
    rows = jnp.pad(rows, ((0, 0), (0, 0), (0, 16 - n_dr), (0, 0)))
    return pl.pallas_call(
        _bias_kernel,
        grid=(DEPTH, H_A),
        in_specs=[pl.BlockSpec((1, 1, 16, LANES), lambda l, h: (l, h, 0, 0))],
        out_specs=pl.BlockSpec((1, 3, 1, NA_Q, NA_K), lambda l, h: (l, 0, h, 0, 0)),
        out_shape=jax.ShapeDtypeStruct((DEPTH, 3, H_A, NA_Q, NA_K), F32),
        compiler_params=_params(("arbitrary", "arbitrary")),
        name="na_bias",
    )(rows)


def _mlstm_kernel(*refs, n_chunks, has_init, want_state, n_alias, layer):
    dirs = (refs[0:4], refs[4:8])
    pos = 8
    if has_init:
        c0_ref, n0_ref, m0_ref = refs[pos:pos + 3]
        pos += 3
    pos += n_alias
    n_h = 1 if n_chunks == 1 else 2
    h_refs = refs[pos:pos + n_h]
    pos += n_h
    if want_state:
        cf_ref, nf_ref, mf_ref = refs[pos:pos + 3]
        pos += 3
    cs_ref, ms_ref = refs[pos:pos + 2]
    c = pl.program_id(1)

    def tail_rows(x):
        r = lax.broadcasted_iota(jnp.int32, (STATE_TAIL, x.shape[1]), 0)
        return jnp.where(r == 0, x, jnp.where(r == 1, x - x.astype(BF16).astype(F32), 0.0))

    @pl.when(c == 0)
    def _():
        if has_init:
            cs_ref[:, 0:HD_B, :] = c0_ref[0]
            for i in range(N_STATE):
                cs_ref[i, HD_B:HD_B + STATE_TAIL, :] = tail_rows(n0_ref[0, i:i + 1, :])
            ms_ref[...] = m0_ref[0]
        else:
            cs_ref[...] = jnp.zeros(cs_ref.shape, F32)
            ms_ref[...] = jnp.zeros(ms_ref.shape, F32)

    row = lax.broadcasted_iota(jnp.int32, (CHUNK, CHUNK), 0)
    col = lax.broadcasted_iota(jnp.int32, (CHUNK, CHUNK), 1)
    chains = []
    for d in range(2):
        q_ref, k_ref, v_ref, g_ref = dirs[d]
        vis = (row <= col) if d == 0 else (row >= col)
        cum_b = jnp.where((col <= row) if d == 0 else (col >= row), 1.0, 0.0).astype(BF16)
        g_t = g_ref[...].T[0:N_GATE, :]
        r = _dot_nt(jnp.concatenate(_split3(g_t), axis=0), cum_b)
        b_t = r[0:N_GATE] + r[N_GATE:2 * N_GATE] + r[2 * N_GATE:3 * N_GATE]
        i0 = 2 * H_B * d
        u_t = g_t[i0:i0 + H_B, :] - b_t[i0 + H_B:i0 + 2 * H_B, :]
        u_cols = jnp.concatenate([u_t, jnp.zeros((LANES - H_B, CHUNK), F32)], axis=0).T
        q = q_ref[...]
        k = k_ref[...]
        v32 = v_ref[...].astype(F32)
        for hh in range(H_B):
            sl = slice(HD_B * hh, HD_B * (hh + 1))
            chains.append(dict(
                d=d, i=d * H_B + hh, sl=sl, vis=vis, qh=q[:, sl], kh=k[:, sl], v32=v32,
                u_col=u_cols[:, hh:hh + 1],
                b_row=b_t[i0 + H_B + hh:i0 + H_B + hh + 1, :],
                i_row=g_t[i0 + hh:i0 + hh + 1, :]))

    for ch in chains:
        ch["state"] = cs_ref[ch["i"]]
        ch["r"] = _dot_nt(jnp.concatenate([ch["kh"], ch["state"].astype(BF16)], axis=0), ch["qh"])
    for ch in chains:
        b_row, m_prev = ch["b_row"], ms_ref[ch["i"]:ch["i"] + 1, 0:1]
        dm = jnp.where(ch["vis"], ch["u_col"] + b_row, -jnp.inf)
        inter = b_row + m_prev
        m_t = jnp.maximum(inter, jnp.max(dm, axis=0, keepdims=True))
        ch["a"] = jnp.exp(inter - m_t)
        ch["w"] = jnp.exp(dm - m_t)
        ch["floor"] = jnp.exp(-m_t)
        b_end = jnp.broadcast_to(b_row[:, CHUNK - 1:CHUNK] if ch["d"] == 0 else b_row[:, 0:1], (1, CHUNK))
        gl = b_end - b_row + ch["i_row"]
        m_new = jnp.maximum(b_end + m_prev, jnp.max(gl, axis=-1, keepdims=True))
        ch["a_s"] = jnp.exp(b_end + m_prev - m_new)[:, 0:HD_B]
        ch["w_s"] = jnp.exp(gl - m_new)
        ch["m_new"] = m_new
    for ch in chains:
        r, a = ch["r"], ch["a"]
        p = r[0:CHUNK] * ch["w"]
        qn = r[CHUNK + HD_B:CHUNK + HD_B + 1] + r[CHUNK + HD_B + 1:CHUNK + HD_B + 2]
        den = a * qn + jnp.sum(p, axis=0, keepdims=True)
        ch["v_t"] = ch["v32"][:, ch["sl"]].T
        num = a * r[CHUNK:CHUNK + HD_B] + _dot(ch["v_t"].astype(BF16), p.astype(BF16))
        h_t = num * (1.0 / jnp.maximum(jnp.abs(den), ch["floor"]))
        if n_h == 2:
            h_refs[ch["d"]][:, ch["sl"]] = h_t.T
        else:
            ch["h_t"] = h_t
    if n_h == 1:
        for fwd_ch, bwd_ch in zip(chains[:H_B], chains[H_B:]):
            h_refs[0][:, fwd_ch["sl"]] = (fwd_ch["h_t"] + bwd_ch["h_t"]).T
    for ch in chains:
        i, a_s, state = ch["i"], ch["a_s"], ch["state"]
        lhs = jnp.concatenate([ch["v_t"] * ch["w_s"], tail_rows(ch["w_s"])], axis=0).astype(BF16)
        upd = _dot(lhs, ch["kh"])
        n_new = a_s * state[HD_B:HD_B + 1] + upd[HD_B:HD_B + 1] + upd[HD_B + 1:HD_B + 2]
        cs_ref[i, 0:HD_B, :] = a_s * state[0:HD_B] + upd[0:HD_B]
        cs_ref[i, HD_B:HD_B + STATE_TAIL, :] = tail_rows(n_new)
        ms_ref[i:i + 1, :] = ch["m_new"][:, 0:LANES]

    if want_state:
        @pl.when(c == n_chunks - 1)
        def _():
            if n_alias:
                cf_ref[0] = cs_ref[:, 0:HD_B, :]
            else:
                cf_ref[...] = jnp.zeros(cf_ref.shape, F32)
                cf_ref[0, layer] = cs_ref[:, 0:HD_B, :]
            for i in range(N_STATE):
                nf_ref[0, i:i + 1, :] = cs_ref[i, HD_B:HD_B + 1, :]
            mf_ref[0] = ms_ref[...]


def _mlstm(qb, kb, vb, gt, row0, n_batch, seq_len, layer, init=None, want_state=False, c_prev=None):
    n_chunks = seq_len // CHUNK
    blk0 = row0 // CHUNK

    def fwd(w):
        return pl.BlockSpec((CHUNK, w), lambda b, c: (blk0 + b * n_chunks + c, 0))

    def bwd(w):
        return pl.BlockSpec((CHUNK, w), lambda b, c: (blk0 + b * n_chunks + n_chunks - 1 - c, 0))

    in_specs = [fwd(B_WIDTH), fwd(B_WIDTH), fwd(B_WIDTH), fwd(LANES),
                bwd(B_WIDTH), bwd(B_WIDTH), bwd(B_WIDTH), bwd(LANES)]
    args = [qb, kb, vb, gt, qb, kb, vb, gt]
    c_spec = pl.BlockSpec((1, None, N_STATE, HD_B, HD_B), lambda b, c: (b, layer, 0, 0, 0))
    if init is not None:
        in_specs += [c_spec,
                     pl.BlockSpec((1, None, N_STATE, LANES), lambda b, c: (b, layer, 0, 0)),
                     pl.BlockSpec((1, None, N_STATE, LANES), lambda b, c: (b, layer, 0, 0))]
        args += list(init)
    n_rows = n_batch * seq_len
    out_specs = [pl.BlockSpec((CHUNK, B_WIDTH), lambda b, c: (b * n_chunks + c, 0)),
                 pl.BlockSpec((CHUNK, B_WIDTH), lambda b, c: (b * n_chunks + n_chunks - 1 - c, 0))]
    if n_chunks == 1:
        out_specs = out_specs[:1]
    out_shape = [jax.ShapeDtypeStruct((n_rows, B_WIDTH), F32)] * len(out_specs)
    aliases = {}
    if c_prev is not None:
        aliases = {len(args): len(out_specs)}
        in_specs += [pl.BlockSpec(memory_space=pl.ANY)]
        args += [c_prev]
    if want_state:
        c_out = c_spec if c_prev is not None else pl.BlockSpec(
            (1, DEPTH, N_STATE, HD_B, HD_B), lambda b, c: (b, 0, 0, 0, 0))
        out_specs += [c_out,
                      pl.BlockSpec((1, N_STATE, LANES), lambda b, c: (b, 0, 0)),
                      pl.BlockSpec((1, N_STATE, LANES), lambda b, c: (b, 0, 0))]
        out_shape += [jax.ShapeDtypeStruct((n_batch, DEPTH, N_STATE, HD_B, HD_B), F32),
                      jax.ShapeDtypeStruct((n_batch, N_STATE, LANES), F32),
                      jax.ShapeDtypeStruct((n_batch, N_STATE, LANES), F32)]
    return pl.pallas_call(
        functools.partial(_mlstm_kernel, n_chunks=n_chunks, has_init=init is not None, want_state=want_state,
                          n_alias=len(aliases), layer=layer),
        grid=(n_batch, n_chunks),
        input_output_aliases=aliases,
        in_specs=in_specs,
        out_specs=out_specs,
        out_shape=out_shape,
        scratch_shapes=[pltpu.VMEM((N_STATE, HD_B + STATE_TAIL, HD_B), F32),
                        pltpu.VMEM((N_STATE, LANES), F32)],
        compiler_params=_params(("parallel", "arbitrary")),
        name="mlstm",
    )(*args)


def _mixout_ffn_kernel(*refs, split_out, n_hc, n_hl):
    x_ref, mod_ref, attc_ref, attl_ref = refs[:4]
    hc_refs = refs[4:4 + n_hc]
    hl_refs = refs[4 + n_hc:4 + n_hc + n_hl]
    n_in = 13 + n_hc + n_hl
    (ob_ref, br_ref, g_ref, ghn_ref, wpa_ref, wpb_ref, wo_ref, wup_ref,
     wdn_ref) = refs[4 + n_hc + n_hl:n_in]
    o_refs = refs[n_in:]
    is_ctx = pl.program_id(0) < N_CTX // TM
    att = jnp.where(is_ctx, attc_ref[...], attl_ref[...])
    hsum = jnp.where(is_ctx, sum(r[...] for r in hc_refs), sum(r[...] for r in hl_refs))
    parts = []
    for hh in range(H_B):
        seg = hsum[:, HD_B * hh:HD_B * (hh + 1)]
        parts.append(seg * lax.rsqrt(jnp.mean(seg * seg, axis=-1, keepdims=True) + EPS))
    hn = (jnp.concatenate(parts, axis=1) * ghn_ref[...]) * ob_ref[...].astype(F32)
    y = (br_ref[:, :D_MODEL].astype(F32) * _dot(att, wpa_ref[...])
         + br_ref[:, D_MODEL:].astype(F32) * _dot(hn.astype(BF16), wpb_ref[...]))
    x = x_ref[...] + _mod_row(mod_ref, 5) * _dot(y.astype(BF16), wo_ref[...])
    out = _swiglu_sub(x, g_ref[2:3, :], _mod_row(mod_ref, 6), _mod_row(mod_ref, 7), _mod_row(mod_ref, 8),
                      wup_ref, wdn_ref)
    _write_stream(o_refs, out, is_ctx, split_out)


def _mixout_ffn(x, mod, att_c, att_l, h_c, h_l, sob, sbr, g_norm, g_hn, w_pa, w_pb, w_o,
                w_up, w_dn, l, split_out=False):
    tok, ctx, lat = _tok_spec, _ctx_spec, _lat_spec
    if split_out:
        out_specs = [ctx(D_MODEL), lat(D_MODEL)]
        out_shape = [jax.ShapeDtypeStruct((N_CTX, D_MODEL), F32), jax.ShapeDtypeStruct((N_LAT, D_MODEL), F32)]
    else:
        out_specs = tok(D_MODEL)
        out_shape = jax.ShapeDtypeStruct((N_TOK, D_MODEL), F32)
    return pl.pallas_call(
        functools.partial(_mixout_ffn_kernel, split_out=split_out, n_hc=len(h_c), n_hl=len(h_l)),
        grid=(N_TOK // TM,),
        in_specs=[
            tok(D_MODEL),
            _layer((MOD_ROWS, N_MOD * D_MODEL), l),
            ctx(A_WIDTH), lat(A_WIDTH),
        ] + [ctx(B_WIDTH)] * len(h_c) + [lat(B_WIDTH)] * len(h_l) + [
            tok(B_WIDTH), tok(2 * D_MODEL),
            _layer((3, D_MODEL), l),
            _layer((1, B_WIDTH), l),
            _layer((A_WIDTH, D_MODEL), l),
            _layer((B_WIDTH, D_MODEL), l),
            _layer((D_MODEL, D_MODEL), l),
            _layer((D_MODEL, 2 * D_FF), l),
            _layer((D_FF, D_MODEL), l),
        ],
        out_specs=out_specs,
        out_shape=out_shape,
        compiler_params=_params(("arbitrary",)),
        name="mixout_ffn",
    )(x, mod, att_c, att_l, *h_c, *h_l, sob, sbr, g_norm, g_hn, w_pa, w_pb, w_o, w_up, w_dn)


def _rope_tables():
    half = HD_B // 2
    nf = half // 2
    inv = ROPE_THETA ** (-jnp.arange(nf, dtype=F32) / nf)
    t = jnp.arange(LAT_LEN)
    rowp = (t // GRID_W).astype(F32)
    colp = (t % GRID_W).astype(F32)
    ang = jnp.concatenate([rowp[:, None] * inv, colp[:, None] * inv], axis=-1)
    cos, sin = jnp.cos(ang), jnp.sin(ang)
    cos_l = jnp.tile(jnp.concatenate([cos, cos], axis=-1), (N_LAT_B, 1))
    sin_l = jnp.tile(jnp.concatenate([-sin, sin], axis=-1), (N_LAT_B, 1))
    cosf = jnp.concatenate([jnp.ones((N_CTX, HD_B), F32), cos_l], axis=0)
    sinf = jnp.concatenate([jnp.zeros((N_CTX, HD_B), F32), sin_l], axis=0)
    return cosf, sinf


def kernel(x_prompt, x_sample, cache_k, cache_v, state_C, state_n, state_m, c, c_ctx, w_ada, b_ada, g_norm,
           w_up1, w_dn1, w_in, b_gates, g_qn, g_kn, rpb, g_hn, w_pa, w_pb, w_o, w_up2, w_dn2):
    assert x_prompt.shape == (N_CTX_B, CTX_LEN, D_MODEL) and x_sample.shape == (N_LAT_B, LAT_LEN, D_MODEL)
    past_len = cache_k.shape[2]

    w_inb = w_in.astype(BF16)
    b_gt = jnp.pad(b_gates, ((0, 0), (0, LANES - N_GATE))).reshape(DEPTH, 1, LANES)
    w_pab, w_pbb, w_ob, w_up2b, w_dn2b = (w.astype(BF16) for w in (w_pa, w_pb, w_o, w_up2, w_dn2))
    gq = jnp.tile(g_qn, (1, H_A)).reshape(DEPTH, 1, A_WIDTH)
    gk = jnp.tile(g_kn, (1, H_A)).reshape(DEPTH, 1, A_WIDTH)
    ghn = g_hn.reshape(DEPTH, 1, B_WIDTH)
    seg_np = (np.arange(A_WIDTH)[:, None] // HD_A == np.arange(LANES)[None, :]).astype(np.float32)
    seg = jnp.asarray(seg_np, BF16)
    segt = jnp.asarray(seg_np.T, BF16)
    cosf, sinf = _rope_tables()
    bias = _bias_tables(rpb)
    ck = cache_k.reshape(N_LAT_B, DEPTH, past_len, A_WIDTH)
    cv = cache_v.reshape(N_LAT_B, DEPTH, past_len, A_WIDTH)
    init = (state_C.reshape(N_LAT_B, DEPTH, N_STATE, HD_B, HD_B),
            state_n.reshape(N_LAT_B, DEPTH, N_STATE, HD_B),
            jnp.broadcast_to(state_m.reshape(N_LAT_B, DEPTH, N_STATE, 1), (N_LAT_B, DEPTH, N_STATE, LANES)))

    cond = jnp.concatenate([c_ctx[None], c, jnp.zeros((MOD_ROWS - 1 - N_LAT_B, D_MODEL), F32)], axis=0)
    mod = _ada(cond, w_ada, b_ada)

    x = (x_prompt.reshape(N_CTX, D_MODEL), x_sample.reshape(N_LAT, D_MODEL))
    caches = tuple(jnp.zeros((N_CTX_B, DEPTH, CTX_LEN, A_WIDTH), F32) for _ in range(2))
    c_new, ns_out, ms_out = None, [], []
    for l in range(DEPTH):
        x = _ffn(x, mod, g_norm, w_up1, w_dn1, l, 0)
        qa, ka, va, k_new, v_new, qb, kb, vb, sob, gt, sbr = _mixin(
            x, mod, g_norm, w_inb, b_gt, gq, gk, seg, segt, cosf, sinf, l, caches)
        caches = (k_new, v_new)
        att_c = _ctx_attn(qa, ka, va)
        att_l = _na_attn(qa, ka, va, ck, cv, bias, l)
        *h_c, c_new, n_fin, m_fin = _mlstm(qb, kb, vb, gt, 0, N_CTX_B, CTX_LEN, l, want_state=True, c_prev=c_new)
        h_l = _mlstm(qb, kb, vb, gt, N_CTX, N_LAT_B, LAT_LEN, l, init=init)
        x = _mixout_ffn(x, mod, att_c, att_l, h_c, h_l, sob, sbr, g_norm, ghn, w_pab, w_pbb, w_ob,
                        w_up2b, w_dn2b, l, split_out=(l == DEPTH - 1))
        ns_out.append(n_fin.reshape(N_CTX_B, 2, H_B, HD_B))
        ms_out.append(m_fin[:, :, 0].reshape(N_CTX_B, 2, H_B))

    return (x[0].reshape(N_CTX_B, CTX_LEN, D_MODEL),
            x[1].reshape(N_LAT_B, LAT_LEN, D_MODEL),
            k_new.reshape(N_CTX_B, DEPTH, CTX_LEN, H_A, HD_A),
            v_new.reshape(N_CTX_B, DEPTH, CTX_LEN, H_A, HD_A),
            c_new.reshape(N_CTX_B, DEPTH, 2, H_B, HD_B, HD_B),
            jnp.stack(ns_out, axis=1), jnp.stack(ms_out, axis=1))
```
